```python
import math, functools
import jax, jax.numpy as jnp
from jax import lax
import numpy as np

D_MODEL = 4096
BATCH = 4
SEQ = 2048
DEPTH = 1

CTX_LEN = 256
GRID_W = 64
HY_WIDTH = 2048
HY_ORDER = 2
HY_SHORT = 3
HY_EMB = 33
HY_FFN = 64
HY_FAST_DECAY = 0.3
HY_SLOW_DECAY = 1.5
HY_TARGET = 1e-2
HG_WIDTH = 2048
HG_HEADS = 16
HG_HEAD = HG_WIDTH // HG_HEADS
HG_CHUNK = 64
N_EXPERTS = 16
EXPERT_FF = 2048
EC_CAPACITY = 2
EPS = 1e-6
OFF_HG = 3 * HY_WIDTH
OFF_F = OFF_HG + HG_WIDTH
OFF_GATE = OFF_HG + 5 * HG_WIDTH
IN_COLS = OFF_GATE + 2 * D_MODEL

kernel_name = 'hybrid_hyena_hgrn2_ec_dit_block'

F32 = jnp.float32


def rms_norm(x, g):
    xf = x.astype(F32)
    y = xf * lax.rsqrt(jnp.mean(xf * xf, axis=-1, keepdims=True) + EPS)
    return (y * g.astype(F32)).astype(x.dtype)


def ada_params(cond, w, b):
    m = jax.nn.silu(cond) @ w + b
    return jnp.split(m, 6, axis=-1)


def modulate(x, shift, scale):
    return x * (1.0 + scale[:, None]) + shift[:, None]


def hyena_positions(L):
    t = jnp.linspace(0.0, 1.0, L, dtype=F32)[:, None]
    bands = (HY_EMB - 1) // 2
    w = 2.0 * math.pi * jnp.arange(L, dtype=F32) / L
    f = jnp.linspace(1e-4, bands - 1, bands, dtype=F32)
    fw = w[:, None] * f[None, :]
    return jnp.concatenate([t, jnp.cos(fw), -jnp.sin(fw)], axis=-1), t


def hyena_filters(L, w1, b1, w2, b2, w3, b3, freq, wout):
    z, t = hyena_positions(L)
    fr = freq.astype(F32)
    h = jnp.sin(fr * (z @ w1.astype(F32) + b1.astype(F32)))
    h = jnp.sin(fr * (h @ w2.astype(F32) + b2.astype(F32)))
    h = jnp.sin(fr * (h @ w3.astype(F32) + b3.astype(F32)))
    h = (h @ wout.astype(F32)).reshape(L, 2 * HY_ORDER, HY_WIDTH)
    deltas = jnp.linspace(math.log(HY_TARGET) / HY_SLOW_DECAY,
                          math.log(HY_TARGET) / HY_FAST_DECAY, HY_WIDTH, dtype=F32)
    decay = jnp.exp(-t * jnp.abs(deltas)[None, :])
    return h * decay[:, None, :]


def long_conv(u, h_fwd, h_bwd, skip):
    L = u.shape[1]
    kern = jnp.concatenate([h_fwd, jnp.zeros_like(h_fwd[:1]), h_bwd[:0:-1]], axis=0)
    U = jnp.fft.rfft(u, n=2 * L, axis=1)
    K = jnp.fft.rfft(kern, n=2 * L, axis=0)
    y = jnp.fft.irfft(U * K[None], n=2 * L, axis=1)[:, :L]
    return y + u * skip


def short_conv(u, w, b, n_rows):
    B, L, C = u.shape
    ur = u.reshape(B, n_rows, L // n_rows, C)
    up = jnp.pad(ur, ((0, 0), (0, 0), (1, 1), (0, 0)))
    y = up[:, :, :-2] * w[0] + up[:, :, 1:-1] * w[1] + up[:, :, 2:] * w[2] + b
    return y.reshape(B, L, C)


def hyena_branch(u, conv_w, conv_b, filt, skip, n_rows):
    u = short_conv(u.astype(F32), conv_w.astype(F32), conv_b.astype(F32), n_rows)
    v, x1, x2 = jnp.split(u, 3, axis=-1)
    sk = skip.astype(F32)
    z = x1 * long_conv(v, filt[:, 0], filt[:, 1], sk[0])
    return x2 * long_conv(z, filt[:, 2], filt[:, 3], sk[1])


def heads(a):
    return a.reshape(a.shape[0], a.shape[1], HG_HEADS, HG_HEAD)


def forget_gate(z, lb):
    f = lb + (1.0 - lb) * jax.nn.sigmoid(z)
    return heads(jnp.log(f)), heads(1.0 - f)


def gla_scan(q, k, v, logf, s0):
    B, L, H, K = q.shape
    n = L // HG_CHUNK

    def to_chunks(a):
        return a.reshape(B, n, HG_CHUNK, H, a.shape[-1]).transpose(1, 0, 3, 2, 4)

    mask = jnp.tril(jnp.ones((HG_CHUNK, HG_CHUNK), dtype=bool))[:, :, None]

    def step(S, inp):
        qc, kc, vc, gc = inp
        b = jnp.cumsum(gc, axis=2)
        diff = b[:, :, :, None, :] - b[:, :, None, :, :]
        dec = jnp.where(mask, jnp.exp(jnp.where(mask, diff, 0.0)), 0.0)
        A = jnp.einsum('bhtk,bhsk,bhtsk->bhts', qc, kc, dec)
        o = jnp.einsum('bhts,bhsv->bhtv', A, vc) + jnp.einsum('bhtk,bhkv->bhtv', qc * jnp.exp(b), S)
        bL = b[:, :, -1:, :]
        S_new = jnp.exp(bL[:, :, 0])[..., None] * S + jnp.einsum('bhsk,bhsv->bhkv', kc * jnp.exp(bL - b), vc)
        return S_new, o

    S_fin, o = lax.scan(step, s0, (to_chunks(q), to_chunks(k), to_chunks(v), to_chunks(logf)))
    o = o.transpose(1, 0, 3, 2, 4).reshape(B, L, H, v.shape[-1])
    return o, S_fin


def final_state(k, v, logf):
    b = jnp.cumsum(logf, axis=1)
    return jnp.einsum('blhk,blhv->bhkv', k * jnp.exp(b[:, -1:] - b), v)


def hgrn_branch(u, lb_f, lb_b, gn, s0_f, s0_b):
    u = u.astype(F32)
    B, L, _ = u.shape
    q, zf, zb, i, g = jnp.split(u, 5, axis=-1)
    q = heads(jax.nn.silu(q)) * HG_HEAD ** -0.5
    logf_f, k_f = forget_gate(zf, lb_f)
    logf_b, k_b = forget_gate(zb, lb_b)
    v = heads(i)
    rev = lambda a: a[:, ::-1]
    o_f, S_f = gla_scan(q, k_f, v, logf_f, s0_f)
    o_b, S_b = gla_scan(rev(q), rev(k_b), rev(v), rev(logf_b), s0_b)
    o = o_f + rev(o_b)
    o = o * lax.rsqrt(jnp.mean(o * o, axis=-1, keepdims=True) + EPS)
    y = o.reshape(B, L, HG_WIDTH) * gn.astype(F32) * jax.nn.silu(g)
    return y, S_f, S_b


def hgrn_context_states(u, lb_f, lb_b):
    u = u.astype(F32)
    zf, zb, i = jnp.split(u, 3, axis=-1)
    logf_f, k_f = forget_gate(zf, lb_f)
    logf_b, k_b = forget_gate(zb, lb_b)
    v = heads(i)
    S_f = final_state(k_f, v, logf_f)
    S_b = final_state(k_b[:, ::-1], v[:, ::-1], logf_b[:, ::-1])
    return S_f, S_b


def token_mixer(xm, n_rows, filt, s0_f, s0_b, *, w_in, conv_w, conv_b, skip, lb_f, lb_b, gn,
                w_proj_hy, w_proj_hg, w_out):
    proj = xm @ w_in
    y_hy = hyena_branch(proj[..., :OFF_HG], conv_w, conv_b, filt, skip, n_rows).astype(xm.dtype)
    y_hg, S_f, S_b = hgrn_branch(proj[..., OFF_HG:OFF_GATE], lb_f, lb_b, gn, s0_f, s0_b)
    g_hy, g_hg = jnp.split(proj[..., OFF_GATE:], 2, axis=-1)
    merged = (jax.nn.sigmoid(g_hy) * (y_hy @ w_proj_hy)
              + jax.nn.sigmoid(g_hg) * (y_hg.astype(xm.dtype) @ w_proj_hg))
    return merged @ w_out, S_f, S_b


def ec_moe(xm, *, w_router, w_gate, w_up, w_down):
    B, L, D = xm.shape
    cap = EC_CAPACITY * L // N_EXPERTS
    aff = jax.nn.softmax((xm @ w_router).astype(F32), axis=-1)
    g, idx = lax.top_k(jnp.swapaxes(aff, 1, 2), cap)
    flat = (jnp.arange(B)[:, None, None] * L + idx).reshape(-1)
    xg = xm.reshape(B * L, D)[flat].reshape(B, N_EXPERTS, cap, D)
    h = jax.nn.silu(jnp.einsum('becd,edf->becf', xg, w_gate)) * jnp.einsum('becd,edf->becf', xg, w_up)
    y = jnp.einsum('becf,efd->becd', h, w_down) * g[..., None].astype(xm.dtype)
    return jnp.zeros((B * L, D), y.dtype).at[flat].add(y.reshape(-1, D)).reshape(B, L, D)


def setup_inputs(seed: int = 0) -> dict:
    key = jax.random.key(seed)
    ks = jax.random.split(key, 32)
    nrm = lambda k, shape, s: jax.random.normal(k, shape, F32) * s
    D = D_MODEL
    return {
        'x': nrm(ks[0], (BATCH, SEQ, D), 1.0),
        'c': nrm(ks[1], (BATCH, D), 1.0),
        'ctx': nrm(ks[2], (BATCH, CTX_LEN, D), 1.0),
        'c_ctx': nrm(ks[3], (D,), 1.0),
        'w_ada': nrm(ks[4], (DEPTH, D, 6 * D), 0.5 * D ** -0.5),
        'b_ada': nrm(ks[5], (DEPTH, 6 * D), 0.01),
        'norm1': 1.0 + nrm(ks[6], (DEPTH, D), 0.01),
        'norm2': 1.0 + nrm(ks[7], (DEPTH, D), 0.01),
        'w_in': nrm(ks[8], (DEPTH, D, IN_COLS), D ** -0.5),
        'hy_conv_w': nrm(ks[9], (DEPTH, HY_SHORT, 3 * HY_WIDTH), 0.5),
        'hy_conv_b': nrm(ks[10], (DEPTH, 3 * HY_WIDTH), 0.01),
        'hy_w1': nrm(ks[11], (DEPTH, HY_EMB, HY_FFN), HY_EMB ** -0.5),
        'hy_b1': nrm(ks[12], (DEPTH, HY_FFN), 0.1),
        'hy_w2': nrm(ks[13], (DEPTH, HY_FFN, HY_FFN), HY_FFN ** -0.5),
        'hy_b2': nrm(ks[14], (DEPTH, HY_FFN), 0.1),
        'hy_w3': nrm(ks[15], (DEPTH, HY_FFN, HY_FFN), HY_FFN ** -0.5),
        'hy_b3': nrm(ks[16], (DEPTH, HY_FFN), 0.1),
        'hy_freq': 1.0 + nrm(ks[17], (DEPTH, HY_FFN), 0.01),
        'hy_wout': nrm(ks[18], (DEPTH, HY_FFN, 2 * HY_ORDER * HY_WIDTH), 0.1 * HY_FFN ** -0.5),
        'hy_skip': nrm(ks[19], (DEPTH, HY_ORDER, HY_WIDTH), 0.5),
        'hg_lb': nrm(ks[20], (DEPTH + 1, 2, HG_WIDTH), 1.0),
        'hg_norm': 1.0 + nrm(ks[21], (DEPTH, HG_WIDTH), 0.01),
        'w_proj_hy': nrm(ks[22], (DEPTH, HY_WIDTH, D), HY_WIDTH ** -0.5),
        'w_proj_hg': nrm(ks[23], (DEPTH, HG_WIDTH, D), HG_WIDTH ** -0.5),
        'w_out': nrm(ks[24], (DEPTH, D, D), D ** -0.5),
        'w_router': nrm(ks[25], (DEPTH, D, N_EXPERTS), D ** -0.5),
        'w_gate': nrm(ks[26], (DEPTH, N_EXPERTS, D, EXPERT_FF), D ** -0.5),
        'w_up': nrm(ks[27], (DEPTH, N_EXPERTS, D, EXPERT_FF), D ** -0.5),
        'w_down': nrm(ks[28], (DEPTH, N_EXPERTS, EXPERT_FF, D), EXPERT_FF ** -0.5),
        'norm_f': 1.0 + nrm(ks[29], (D,), 0.01),
    }


def reference(x, c, ctx, c_ctx, w_ada, b_ada, norm1, norm2, w_in, hy_conv_w, hy_conv_b,
              hy_w1, hy_b1, hy_w2, hy_b2, hy_w3, hy_b3, hy_freq, hy_wout, hy_skip,
              hg_lb, hg_norm, w_proj_hy, w_proj_hg, w_out, w_router, w_gate, w_up, w_down, norm_f):
    B, L, _ = x.shape
    ROWS = L // GRID_W
    Lc = ctx.shape[1]
    lb_all = jnp.cumsum(jax.nn.softmax(hg_lb.astype(F32), axis=0), axis=0)
    zeros0 = jnp.zeros((B, HG_HEADS, HG_HEAD, HG_HEAD), F32)
    xc = ctx
    for l in range(DEPTH):
        last = l == DEPTH - 1
        sh1, sc1, g1, sh2, sc2, g2 = ada_params(c, w_ada[l], b_ada[l])
        csh1, csc1, cg1, csh2, csc2, cg2 = ada_params(c_ctx[None], w_ada[l], b_ada[l])
        mix = functools.partial(token_mixer, w_in=w_in[l], conv_w=hy_conv_w[l], conv_b=hy_conv_b[l],
                                skip=hy_skip[l], lb_f=lb_all[l, 0], lb_b=lb_all[l, 1], gn=hg_norm[l],
                                w_proj_hy=w_proj_hy[l], w_proj_hg=w_proj_hg[l], w_out=w_out[l])
        moe = functools.partial(ec_moe, w_router=w_router[l], w_gate=w_gate[l], w_up=w_up[l], w_down=w_down[l])
        filt_w = (hy_w1[l], hy_b1[l], hy_w2[l], hy_b2[l], hy_w3[l], hy_b3[l], hy_freq[l], hy_wout[l])
        xc_m = modulate(rms_norm(xc, norm1[l]), csh1, csc1)
        if last:
            S_f, S_b = hgrn_context_states(xc_m @ w_in[l][:, OFF_F:OFF_F + 3 * HG_WIDTH],
                                           lb_all[l, 0], lb_all[l, 1])
        else:
            out_c, S_f, S_b = mix(xc_m, 1, hyena_filters(Lc, *filt_w), zeros0, zeros0)
            xc = xc + cg1[:, None] * out_c
            xc = xc + cg2[:, None] * moe(modulate(rms_norm(xc, norm2[l]), csh2, csc2))
        x_m = modulate(rms_norm(x, norm1[l]), sh1, sc1)
        out, _, _ = mix(x_m, ROWS, hyena_filters(L, *filt_w), S_f, S_b)
        x = x + g1[:, None] * out
        x = x + g2[:, None] * moe(modulate(rms_norm(x, norm2[l]), sh2, sc2))
    return rms_norm(x, norm_f)
```

```python
import functools
import math

import jax
import jax.numpy as jnp
from jax import lax
from jax.experimental import pallas as pl
from jax.experimental.pallas import tpu as pltpu

F32 = jnp.float32
BF16 = jnp.bfloat16

EPS = 1e-6
GRID_W = 64
HY_FAST_DECAY = 0.3
HY_SLOW_DECAY = 1.5
HY_TARGET = 1e-2
HG_HEAD = 128
HG_CHUNK = 64
HG_SUB = 16
EC_CAPACITY = 2
LANES = 128
VMEM_LIMIT_BYTES = 56 * 1024 * 1024


def _cparams(n_axes):
    return pltpu.CompilerParams(dimension_semantics=("arbitrary",) * n_axes,
                                vmem_limit_bytes=VMEM_LIMIT_BYTES)


def _tile(pref, dim):
    t = min(pref, dim)
    while dim % t:
        t //= 2
    return t


def _dot(a, b):
    return jnp.dot(a, b, preferred_element_type=F32)


def _dot_nt(a, b):
    return lax.dot_general(a, b, (((1,), (1,)), ((), ())), preferred_element_type=F32)


def _dot_tn(a, b):
    return lax.dot_general(a, b, (((0,), (0,)), ((), ())), preferred_element_type=F32)


def _split(x):
    hi = x.astype(BF16)
    return hi, (x - hi.astype(F32)).astype(BF16)


def _dot3(a, b):
    ah, al = _split(a)
    bh, bl = _split(b)
    return _dot(ah, bh) + _dot(al, bh) + _dot(ah, bl)


def _silu(x):
    return x * jax.nn.sigmoid(x)


def _ada_kernel(c_ref, w_ref, b_ref, o_ref):
    o_ref[...] = _dot3(_silu(c_ref[...]), w_ref[...]) + b_ref[...]


def _ada(cond, w, b):
    R, D = cond.shape
    N = w.shape[1]
    tn = _tile(512, N)
    return pl.pallas_call(
        _ada_kernel, grid=(N // tn,),
        in_specs=[pl.BlockSpec((R, D), lambda n: (0, 0)),
                  pl.BlockSpec((D, tn), lambda n: (0, n)),
                  pl.BlockSpec((1, tn), lambda n: (0, n))],
        out_specs=pl.BlockSpec((R, tn), lambda n: (0, n)),
        out_shape=jax.ShapeDtypeStruct((R, N), F32),
        compiler_params=_cparams(1), name="ada")(cond, w, b.reshape(1, N))


def _rms_mod(x, g, sh, sc):
    y = x * lax.rsqrt(jnp.mean(x * x, axis=-1, keepdims=True) + EPS) * g
    return y * (1.0 + sc) + sh


def _normmod_kernel(x_ref, g_ref, sh_ref, sc_ref, o_ref):
    o_ref[0] = _rms_mod(x_ref[0], g_ref[...], sh_ref[0], sc_ref[0]).astype(o_ref.dtype)


def _normmod(x, g, sh, sc):
    B, L, D = x.shape
    tm = _tile(256, L)
    row = pl.BlockSpec((1, tm, D), lambda b, m: (b, m, 0))
    par = pl.BlockSpec((1, 1, D), lambda b, m: (b, 0, 0))
    return pl.pallas_call(
        _normmod_kernel, grid=(B, L // tm),
        in_specs=[row, pl.BlockSpec((1, D), lambda b, m: (0, 0)), par, par],
        out_specs=row, out_shape=jax.ShapeDtypeStruct((B, L, D), BF16),
        compiler_params=_cparams(2), name="normmod")(x, g.reshape(1, D), sh.reshape(B, 1, D), sc.reshape(B, 1, D))


def _rmsnorm_kernel(x_ref, g_ref, o_ref):
    x = x_ref[...]
    o_ref[...] = x * lax.rsqrt(jnp.mean(x * x, axis=-1, keepdims=True) + EPS) * g_ref[...]


def _rmsnorm(x, g):
    M, D = x.shape
    tm = _tile(256, M)
    row = pl.BlockSpec((tm, D), lambda m: (m, 0))
    return pl.pallas_call(
        _rmsnorm_kernel, grid=(M // tm,),
        in_specs=[row, pl.BlockSpec((1, D), lambda m: (0, 0))],
        out_specs=row, out_shape=jax.ShapeDtypeStruct((M, D), F32),
        compiler_params=_cparams(1), name="final_norm")(x, g.reshape(1, D))


def _mm_kernel(x_ref, w_ref, o_ref, wbf_ref):
    @pl.when(pl.program_id(1) == 0)
    def _():
        wbf_ref[...] = w_ref[...].astype(BF16)
    o_ref[...] = _dot(x_ref[...], wbf_ref[...]).astype(o_ref.dtype)


def _mm(x, w, col0, ncols, tm, tn, out_dtype, name):
    M, K = x.shape
    tm, tn = _tile(tm, M), _tile(tn, ncols)
    assert col0 % tn == 0
    c0 = col0 // tn
    return pl.pallas_call(
        _mm_kernel, grid=(ncols // tn, M // tm),
        in_specs=[pl.BlockSpec((tm, K), lambda n, m: (m, 0)),
                  pl.BlockSpec((K, tn), lambda n, m: (0, c0 + n))],
        out_specs=pl.BlockSpec((tm, tn), lambda n, m: (m, n)),
        out_shape=jax.ShapeDtypeStruct((M, ncols), out_dtype),
        scratch_shapes=[pltpu.VMEM((K, tn), BF16)],
        compiler_params=_cparams(2), name=name)(x, w)


def _mm_res_kernel(x_ref, w_ref, r_ref, g_ref, o_ref, wbf_ref):
    @pl.when(pl.program_id(1) == 0)
    def _():
        wbf_ref[...] = w_ref[...].astype(BF16)
    o_ref[...] = r_ref[...] + g_ref[0] * _dot(x_ref[...], wbf_ref[...])


def _mm_res(x, w, res, gate, rows_per_batch, tm, tn):
    M, K = x.shape
    N = w.shape[1]
    tm, tn = _tile(tm, rows_per_batch), _tile(tn, N)
    mpb = rows_per_batch // tm
    B = gate.shape[0]
    return pl.pallas_call(
        _mm_res_kernel, grid=(N // tn, M // tm),
        in_specs=[pl.BlockSpec((tm, K), lambda n, m: (m, 0)),
                  pl.BlockSpec((K, tn), lambda n, m: (0, n)),
                  pl.BlockSpec((tm, tn), lambda n, m: (m, n)),
                  pl.BlockSpec((1, 1, tn), lambda n, m: (m // mpb, 0, n))],
        out_specs=pl.BlockSpec((tm, tn), lambda n, m: (m, n)),
        out_shape=jax.ShapeDtypeStruct((M, N), F32),
        scratch_shapes=[pltpu.VMEM((K, tn), BF16)],
        compiler_params=_cparams(2), name="out_proj")(x, w, res, gate.reshape(B, 1, N))


def _merge_kernel(yhy_ref, yhg_ref, why_ref, whg_ref, ghy_ref, ghg_ref, o_ref):
    a = _dot(yhy_ref[...], why_ref[...].astype(BF16))
    b = _dot(yhg_ref[...], whg_ref[...].astype(BF16))
    o_ref[...] = (jax.nn.sigmoid(ghy_ref[...].astype(F32)) * a
                  + jax.nn.sigmoid(ghg_ref[...].astype(F32)) * b).astype(o_ref.dtype)


def _merge(yhy, yhg, why, whg, proj, off_gate):
    M, K1 = yhy.shape
    K2 = yhg.shape[1]
    D = why.shape[1]
    tm, tn = _tile(1024, M), _tile(512, D)
    assert off_gate % tn == 0
    g0, g1 = off_gate // tn, (off_gate + D) // tn
    return pl.pallas_call(
        _merge_kernel, grid=(D // tn, M // tm),
        in_specs=[pl.BlockSpec((tm, K1), lambda n, m: (m, 0)),
                  pl.BlockSpec((tm, K2), lambda n, m: (m, 0)),
                  pl.BlockSpec((K1, tn), lambda n, m: (0, n)),
                  pl.BlockSpec((K2, tn), lambda n, m: (0, n)),
                  pl.BlockSpec((tm, tn), lambda n, m: (m, g0 + n)),
                  pl.BlockSpec((tm, tn), lambda n, m: (m, g1 + n))],
        out_specs=pl.BlockSpec((tm, tn), lambda n, m: (m, n)),
        out_shape=jax.ShapeDtypeStruct((M, D), BF16),
        compiler_params=_cparams(2), name="merge")(yhy, yhg, why, whg, proj, proj)


def _sconv_kernel(p_ref, w_ref, b_ref, o_ref):
    x = p_ref[0].astype(F32)
    tl = x.shape[0]
    pos = lax.broadcasted_iota(jnp.int32, x.shape, 0) % GRID_W
    prev = jnp.where(pos != 0, pltpu.roll(x, 1, 0), 0.0)
    nxt = jnp.where(pos != GRID_W - 1, pltpu.roll(x, tl - 1, 0), 0.0)
    w = w_ref[...]
    o_ref[0] = (prev * w[0:1] + x * w[1:2] + nxt * w[2:3] + b_ref[...]).astype(o_ref.dtype)


def _short_conv(proj3, w, b):
    B, L, _ = proj3.shape
    C = w.shape[1]
    tl, tc = _tile(512, L), _tile(1024, C)
    assert tl % GRID_W == 0
    blk = pl.BlockSpec((1, tl, tc), lambda b_, l, c: (b_, l, c))
    return pl.pallas_call(
        _sconv_kernel, grid=(B, L // tl, C // tc),
        in_specs=[blk, pl.BlockSpec((3, tc), lambda b_, l, c: (0, c)),
                  pl.BlockSpec((1, tc), lambda b_, l, c: (0, c))],
        out_specs=blk, out_shape=jax.ShapeDtypeStruct((B, L, C), BF16),
        compiler_params=_cparams(3), name="short_conv")(proj3, w, b.reshape(1, C))


def _dft_tables(L):
    n = 2 * L
    k = jnp.arange(L, dtype=jnp.int32)
    ang = ((k[:, None] * k[None, :]) % n).astype(F32) * (2.0 * math.pi / n)
    alt = jnp.where(k % 2 == 0, 1.0, -1.0).astype(F32)
    cs, sn = jnp.cos(ang), -jnp.sin(ang)
    fwd = jnp.concatenate([cs, sn.at[0].set(alt)], axis=0)
    inv = jnp.concatenate([(2.0 / n) * cs.at[:, 0].set(0.5),
                           (2.0 / n) * sn.at[:, 0].set(0.5 * alt)], axis=1)
    return fwd.astype(BF16), inv.astype(BF16)


def _dftf_kernel(u_ref, fc_ref, fs_ref, kre_ref, kim_ref, kd_ref, yre_ref, yim_ref):
    u = u_ref[0]
    ure = _dot(fc_ref[...], u)
    uim = _dot(fs_ref[...], u)
    kim = kim_ref[...]
    yre_ref[0] = (ure * kre_ref[...] - uim * kim).astype(yre_ref.dtype)
    yim_ref[0] = (ure * kim + uim * kd_ref[...]).astype(yim_ref.dtype)


def _dfti_kernel(yre_ref, yim_ref, gc_ref, gs_ref, u_ref, gate_ref, skip_ref, o_ref):
    y = _dot(gc_ref[...], yre_ref[0]) + _dot(gs_ref[...], yim_ref[0])
    u = u_ref[0].astype(F32)
    o_ref[0] = (gate_ref[0].astype(F32) * (y + skip_ref[...] * u)).astype(o_ref.dtype)


def _long_conv_gated(u_arr, u_col0, gate_arr, gate_col0, C, fwd, inv, kre, kim, kd, skip):
    B, L, _ = u_arr.shape
    tm, tn = _tile(512, L), _tile(512, C)
    assert u_col0 % tn == 0 and gate_col0 % tn == 0
    uc, gc = u_col0 // tn, gate_col0 // tn
    nm = L // tm
    grid = (C // tn, nm, B)
    kspec = pl.BlockSpec((tm, tn), lambda n, m, b: (m, n))
    yre, yim = pl.pallas_call(
        _dftf_kernel, grid=grid,
        in_specs=[pl.BlockSpec((1, L, tn), lambda n, m, b: (b, 0, uc + n)),
                  pl.BlockSpec((tm, L), lambda n, m, b: (m, 0)),
                  pl.BlockSpec((tm, L), lambda n, m, b: (nm + m, 0)),
                  kspec, kspec, kspec],
        out_specs=[pl.BlockSpec((1, tm, tn), lambda n, m, b: (b, m, n))] * 2,
        out_shape=[jax.ShapeDtypeStruct((B, L, C), BF16)] * 2,
        compiler_params=_cparams(3), name="dft_fwd")(u_arr, fwd, fwd, kre, kim, kd)
    yspec = pl.BlockSpec((1, L, tn), lambda n, m, b: (b, 0, n))
    return pl.pallas_call(
        _dfti_kernel, grid=grid,
        in_specs=[yspec, yspec,
                  pl.BlockSpec((tm, L), lambda n, m, b: (m, 0)),
                  pl.BlockSpec((tm, L), lambda n, m, b: (m, 1)),
                  pl.BlockSpec((1, tm, tn), lambda n, m, b: (b, m, uc + n)),
                  pl.BlockSpec((1, tm, tn), lambda n, m, b: (b, m, gc + n)),
                  pl.BlockSpec((1, tn), lambda n, m, b: (0, n))],
        out_specs=pl.BlockSpec((1, tm, tn), lambda n, m, b: (b, m, n)),
        out_shape=jax.ShapeDtypeStruct((B, L, C), BF16),
        compiler_params=_cparams(3), name="dft_inv")(yre, yim, inv, inv, u_arr, gate_arr, skip.reshape(1, C))


def _hyena_filters(L, w1, b1, w2, b2, w3, b3, freq, wout, width):
    hp = lax.Precision.HIGHEST
    emb = w1.shape[0]
    t = jnp.linspace(0.0, 1.0, L, dtype=F32)[:, None]
    bands = (emb - 1) // 2
    w = 2.0 * math.pi * jnp.arange(L, dtype=F32) / L
    f = jnp.linspace(1e-4, bands - 1, bands, dtype=F32)
    fw = w[:, None] * f[None, :]
    z = jnp.concatenate([t, jnp.cos(fw), -jnp.sin(fw)], axis=-1)
    h = jnp.sin(freq * (jnp.dot(z, w1, precision=hp) + b1))
    h = jnp.sin(freq * (jnp.dot(h, w2, precision=hp) + b2))
    h = jnp.sin(freq * (jnp.dot(h, w3, precision=hp) + b3))
    h = jnp.dot(h, wout, precision=hp).reshape(L, wout.shape[1] // width, width)
    deltas = jnp.linspace(math.log(HY_TARGET) / HY_SLOW_DECAY, math.log(HY_TARGET) / HY_FAST_DECAY,
                          width, dtype=F32)
    return h * jnp.exp(-t * jnp.abs(deltas)[None, :])[:, None, :]


def _filter_spectrum(h_fwd, h_bwd):
    L = h_fwd.shape[0]
    kern = jnp.concatenate([h_fwd, jnp.zeros_like(h_fwd[:1]), h_bwd[:0:-1]], axis=0)
    K = jnp.fft.rfft(kern, n=2 * L, axis=0)
    kre, kim = jnp.real(K[:L]), jnp.imag(K[:L])
    return kre, kim, kre.at[0].set(jnp.real(K[L]))


def _hg_gate(z, lb):
    f = lb + (1.0 - lb) * jax.nn.sigmoid(z)
    return jnp.log(f), 1.0 - f


def _chunk_cumsum(g, rev):
    n = g.shape[0]
    pos = lax.broadcasted_iota(jnp.int32, g.shape, 0) % HG_CHUNK
    d = 1
    while d < HG_CHUNK:
        if rev:
            g = g + jnp.where(pos < HG_CHUNK - d, pltpu.roll(g, n - d, 0), 0.0)
        else:
            g = g + jnp.where(pos >= d, pltpu.roll(g, d, 0), 0.0)
        d *= 2
    return g


def _row_at(b, step, rev):
    r = HG_CHUNK - 1 - step if rev else step
    return b[r:r + 1, :]


def _hg_state(k, v, b, st, rev):
    bl = _row_at(b, HG_CHUNK - 1, rev)
    return st * jnp.exp(bl) + _dot_tn(v.astype(BF16), (k * jnp.exp(bl - b)).astype(BF16))


def _hg_chunk(q, k, v, b, st, rev):
    C, SB = HG_CHUNK, HG_SUB
    nsb = C // SB
    row = lax.broadcasted_iota(jnp.int32, q.shape, 0)
    p = (C - 1 - row) if rev else row
    mids = _row_at(b, SB // 2, rev)
    ends = _row_at(b, SB - 1, rev)
    for i in range(1, nsb):
        mids = jnp.where(p >= i * SB, _row_at(b, i * SB + SB // 2, rev), mids)
        ends = jnp.where(p >= i * SB, _row_at(b, i * SB + SB - 1, rev), ends)
    tt = lax.broadcasted_iota(jnp.int32, (C, C), 0)
    ss = lax.broadcasted_iota(jnp.int32, (C, C), 1)
    if rev:
        tt, ss = C - 1 - tt, C - 1 - ss
    ad = _dot_nt((q * jnp.exp(b - mids)).astype(BF16), (k * jnp.exp(mids - b)).astype(BF16))
    a = jnp.where((tt // SB == ss // SB) & (ss <= tt), ad, 0.0)
    ke = k * jnp.exp(ends - b)
    qs, ks = [], []
    for j in range(nsb - 1):
        ej = _row_at(b, j * SB + SB - 1, rev)
        qs.append(jnp.where(p >= (j + 1) * SB, q * jnp.exp(jnp.minimum(b - ej, 0.0)), 0.0).astype(BF16))
        ks.append(jnp.where((p >= j * SB) & (p < (j + 1) * SB), ke, 0.0).astype(BF16))
    a = a + _dot_nt(jnp.concatenate(qs, axis=1), jnp.concatenate(ks, axis=1))
    o = _dot(a.astype(BF16), v.astype(BF16)) + _dot_nt((q * jnp.exp(b)).astype(BF16), st.astype(BF16))
    return o, _hg_state(k, v, b, st, rev)


def _hgrn_kernel(q_ref, zf_ref, zb_ref, i_ref, g_ref, lbf_ref, lbb_ref, gn_ref, sf_ref, sb_ref, o_ref,
                 q_s, kf_s, kb_s, v_s, bf_s, cb_s, of_s, ob_s):
    C = HG_CHUNK
    nc = q_s.shape[0] // C
    q_s[...] = _silu(q_ref[0].astype(F32)) * (HG_HEAD ** -0.5)
    gf, kf = _hg_gate(zf_ref[0].astype(F32), lbf_ref[0])
    kf_s[...] = kf
    bf_s[...] = _chunk_cumsum(gf, False)
    gb, kb = _hg_gate(zb_ref[0].astype(F32), lbb_ref[0])
    kb_s[...] = kb
    cb_s[...] = _chunk_cumsum(gb, True)
    v_s[...] = i_ref[0].astype(F32)

    def body(c, carry):
        stf, stb = carry
        rf = pl.ds(pl.multiple_of(c * C, C), C)
        rb = pl.ds(pl.multiple_of((nc - 1 - c) * C, C), C)
        of, stf = _hg_chunk(q_s[rf, :], kf_s[rf, :], v_s[rf, :], bf_s[rf, :], stf, False)
        ob, stb = _hg_chunk(q_s[rb, :], kb_s[rb, :], v_s[rb, :], cb_s[rb, :], stb, True)
        of_s[rf, :] = of
        ob_s[rb, :] = ob
        return stf, stb

    lax.fori_loop(0, nc, body, (sf_ref[0, 0], sb_ref[0, 0]))
    o = of_s[...] + ob_s[...]
    o = o * lax.rsqrt(jnp.mean(o * o, axis=-1, keepdims=True) + EPS)
    o_ref[0] = (o * gn_ref[0] * _silu(g_ref[0].astype(F32))).astype(o_ref.dtype)


def _hgrn(proj3, off_hg, width, lb_f, lb_b, gn, st_f, st_b):
    B, L, _ = proj3.shape
    H = width // HG_HEAD
    assert L % HG_CHUNK == 0 and off_hg % HG_HEAD == 0
    c0 = off_hg // HG_HEAD

    def col(i):
        return pl.BlockSpec((1, L, HG_HEAD), lambda b, h: (b, 0, c0 + i * H + h))

    par = pl.BlockSpec((1, 1, HG_HEAD), lambda b, h: (h, 0, 0))
    st = pl.BlockSpec((1, 1, HG_HEAD, HG_HEAD), lambda b, h: (b, h, 0, 0))
    return pl.pallas_call(
        _hgrn_kernel, grid=(B, H),
        in_specs=[col(0), col(1), col(2), col(3), col(4), par, par, par, st, st],
        out_specs=pl.BlockSpec((1, L, HG_HEAD), lambda b, h: (b, 0, h)),
        out_shape=jax.ShapeDtypeStruct((B, L, width), BF16),
        scratch_shapes=[pltpu.VMEM((L, HG_HEAD), F32)] * 8,
        compiler_params=_cparams(2), name="hgrn")(
            proj3, proj3, proj3, proj3, proj3,
            lb_f.reshape(H, 1, HG_HEAD), lb_b.reshape(H, 1, HG_HEAD), gn.reshape(H, 1, HG_HEAD), st_f, st_b)


def _hgctx_kernel(zf_ref, zb_ref, i_ref, lbf_ref, lbb_ref, sf_ref, sb_ref):
    C = HG_CHUNK
    nc = zf_ref.shape[1] // C
    gf, kf = _hg_gate(zf_ref[0].astype(F32), lbf_ref[0])
    gb, kb = _hg_gate(zb_ref[0].astype(F32), lbb_ref[0])
    bf = _chunk_cumsum(gf, False)
    cb = _chunk_cumsum(gb, True)
    v = i_ref[0].astype(F32)
    stf = jnp.zeros((HG_HEAD, HG_HEAD), F32)
    stb = jnp.zeros((HG_HEAD, HG_HEAD), F32)
    for c in range(nc):
        sl = slice(c * C, (c + 1) * C)
        stf = _hg_state(kf[sl], v[sl], bf[sl], stf, False)
    for c in reversed(range(nc)):
        sl = slice(c * C, (c + 1) * C)
        stb = _hg_state(kb[sl], v[sl], cb[sl], stb, True)
    sf_ref[0, 0] = stf
    sb_ref[0, 0] = stb


def _hgrn_context_states(pc3, width, lb_f, lb_b):
    B, Lc, _ = pc3.shape
    H = width // HG_HEAD
    assert Lc % HG_CHUNK == 0

    def col(i):
        return pl.BlockSpec((1, Lc, HG_HEAD), lambda b, h: (b, 0, i * H + h))

    par = pl.BlockSpec((1, 1, HG_HEAD), lambda b, h: (h, 0, 0))
    st = pl.BlockSpec((1, 1, HG_HEAD, HG_HEAD), lambda b, h: (b, h, 0, 0))
    shp = jax.ShapeDtypeStruct((B, H, HG_HEAD, HG_HEAD), F32)
    return pl.pallas_call(
        _hgctx_kernel, grid=(B, H),
        in_specs=[col(0), col(1), col(2), par, par],
        out_specs=[st, st], out_shape=[shp, shp],
        compiler_params=_cparams(2), name="hgrn_ctx")(
            pc3, pc3, pc3, lb_f.reshape(H, 1, HG_HEAD), lb_b.reshape(H, 1, HG_HEAD))


def _norm_router_kernel(x_ref, g_ref, sh_ref, sc_ref, wr_ref, xm_ref, aff_ref, *, n_experts):
    xm = _rms_mod(x_ref[0], g_ref[...], sh_ref[0], sc_ref[0])
    xm_ref[0] = xm.astype(xm_ref.dtype)
    logits = _dot3(xm, wr_ref[...])
    lane = lax.broadcasted_iota(jnp.int32, logits.shape, 1)
    logits = jnp.where(lane < n_experts, logits, -1e30)
    e = jnp.exp(logits - jnp.max(logits, axis=-1, keepdims=True))
    aff_ref[0] = e / jnp.sum(e, axis=-1, keepdims=True)


def _norm_router(x, g, sh, sc, w_router):
    B, L, D = x.shape
    E = w_router.shape[1]
    tm = _tile(256, L)
    wr = jnp.pad(w_router, ((0, 0), (0, LANES - E)))
    row = pl.BlockSpec((1, tm, D), lambda b, m: (b, m, 0))
    par = pl.BlockSpec((1, 1, D), lambda b, m: (b, 0, 0))
    return pl.pallas_call(
        functools.partial(_norm_router_kernel, n_experts=E), grid=(B, L // tm),
        in_specs=[row, pl.BlockSpec((1, D), lambda b, m: (0, 0)), par, par,
                  pl.BlockSpec((D, LANES), lambda b, m: (0, 0))],
        out_specs=[row, pl.BlockSpec((1, tm, LANES), lambda b, m: (b, m, 0))],
        out_shape=[jax.ShapeDtypeStruct((B, L, D), BF16), jax.ShapeDtypeStruct((B, L, LANES), F32)],
        compiler_params=_cparams(2), name="norm_router")(
            x, g.reshape(1, D), sh.reshape(B, 1, D), sc.reshape(B, 1, D), wr)


def _rank_kernel(arow_ref, acol_ref, rank_ref):
    arow = arow_ref[0, 0]
    L = arow.shape[1]
    ts = _tile(256, L)
    tidx = lax.broadcasted_iota(jnp.int32, (ts, L), 1)
    srow = lax.broadcasted_iota(jnp.int32, (ts, L), 0)

    def body(i, acc):
        s0 = pl.multiple_of(i * ts, ts)
        acol = acol_ref[0, 0, pl.ds(s0, ts), :]
        ahead = (acol > arow) | ((acol == arow) & (srow + s0 < tidx))
        return acc + jnp.sum(jnp.where(ahead, 1.0, 0.0), axis=0, keepdims=True)

    acc = lax.fori_loop(0, L // ts, body, jnp.zeros((1, L), F32))
    rank_ref[0, 0] = acc.astype(jnp.int32)


def _rank(aff_t):
    B, E, L = aff_t.shape
    rowspec = pl.BlockSpec((1, 1, 1, L), lambda b, e: (b, e, 0, 0))
    return pl.pallas_call(
        _rank_kernel, grid=(B, E),
        in_specs=[rowspec, pl.BlockSpec((1, 1, L, 1), lambda b, e: (b, e, 0, 0))],
        out_specs=rowspec, out_shape=jax.ShapeDtypeStruct((B, E, 1, L), jnp.int32),
        compiler_params=_cparams(2), name="route_rank")(aff_t.reshape(B, E, 1, L), aff_t.reshape(B, E, L, 1))


def _gather_kernel(rank_ref, xm_ref, o_ref):
    rk = rank_ref[0, 0]
    cap = o_ref.shape[2]
    slot = lax.broadcasted_iota(jnp.int32, (cap, rk.shape[1]), 0)
    sel = jnp.where(slot == rk, 1.0, 0.0).astype(BF16)
    o_ref[0, 0] = _dot(sel, xm_ref[0]).astype(o_ref.dtype)


def _gather(rank_row, xm, cap):
    B, E, _, L = rank_row.shape
    D = xm.shape[2]
    td = _tile(1024, D)
    return pl.pallas_call(
        _gather_kernel, grid=(B, D // td, E),
        in_specs=[pl.BlockSpec((1, 1, 1, L), lambda b, d, e: (b, e, 0, 0)),
                  pl.BlockSpec((1, L, td), lambda b, d, e: (b, 0, d))],
        out_specs=pl.BlockSpec((1, 1, cap, td), lambda b, d, e: (e, b, 0, d)),
        out_shape=jax.ShapeDtypeStruct((E, B, cap, D), BF16),
        compiler_params=_cparams(3), name="moe_gather")(rank_row, xm)


def _expert_up_kernel(x_ref, wg_ref, wu_ref, h_ref):
    x = x_ref[0]
    a = _dot(x, wg_ref[0].astype(BF16))
    u = _dot(x, wu_ref[0].astype(BF16))
    h_ref[0] = (_silu(a) * u).astype(h_ref.dtype)


def _expert_down_kernel(h_ref, wd_ref, y_ref):
    y_ref[0] = _dot(h_ref[0], wd_ref[0].astype(BF16)).astype(y_ref.dtype)


def _experts(xg, w_gate, w_up, w_down):
    E, M, D = xg.shape
    FF = w_gate.shape[2]
    tf, td = _tile(256, FF), _tile(512, D)
    h = pl.pallas_call(
        _expert_up_kernel, grid=(E, FF // tf),
        in_specs=[pl.BlockSpec((1, M, D), lambda e, n: (e, 0, 0)),
                  pl.BlockSpec((1, D, tf), lambda e, n: (e, 0, n)),
                  pl.BlockSpec((1, D, tf), lambda e, n: (e, 0, n))],
        out_specs=pl.BlockSpec((1, M, tf), lambda e, n: (e, 0, n)),
        out_shape=jax.ShapeDtypeStruct((E, M, FF), BF16),
        compiler_params=_cparams(2), name="expert_up")(xg, w_gate, w_up)
    return pl.pallas_call(
        _expert_down_kernel, grid=(E, D // td),
        in_specs=[pl.BlockSpec((1, M, FF), lambda e, n: (e, 0, 0)),
                  pl.BlockSpec((1, FF, td), lambda e, n: (e, 0, n))],
        out_specs=pl.BlockSpec((1, M, td), lambda e, n: (e, 0, n)),
        out_shape=jax.ShapeDtypeStruct((E, M, D), BF16),
        compiler_params=_cparams(2), name="expert_down")(h, w_down)


def _combine_kernel(rk_ref, af_ref, y_ref, x_ref, g_ref, o_ref, w_s):
    E, _, cap, td = y_ref.shape

    @pl.when(pl.program_id(2) == 0)
    def _():
        slot = lax.broadcasted_iota(jnp.int32, (w_s.shape[0], cap), 1)
        for e in range(E):
            w_s[:, e * cap:(e + 1) * cap] = jnp.where(rk_ref[0, e] == slot, af_ref[0, e], 0.0).astype(w_s.dtype)

    y = y_ref[:, 0].reshape(E * cap, td)
    o_ref[0] = x_ref[0] + g_ref[0] * _dot(w_s[...], y)


def _combine(rank_col, aff_col, y, x, gate):
    B, E, L, _ = rank_col.shape
    cap, D = y.shape[2], y.shape[3]
    tm, td = _tile(512, L), _tile(512, D)
    colspec = pl.BlockSpec((1, E, tm, 1), lambda b, m, d: (b, 0, m, 0))
    xspec = pl.BlockSpec((1, tm, td), lambda b, m, d: (b, m, d))
    return pl.pallas_call(
        _combine_kernel, grid=(B, L // tm, D // td),
        in_specs=[colspec, colspec,
                  pl.BlockSpec((E, 1, cap, td), lambda b, m, d: (0, b, 0, d)),
                  xspec, pl.BlockSpec((1, 1, td), lambda b, m, d: (b, 0, d))],
        out_specs=xspec, out_shape=jax.ShapeDtypeStruct((B, L, D), F32),
        scratch_shapes=[pltpu.VMEM((tm, E * cap), BF16)],
        compiler_params=_cparams(3), name="moe_combine")(rank_col, aff_col, y, x, gate.reshape(B, 1, D))


def kernel(x, c, ctx, c_ctx, w_ada, b_ada, norm1, norm2, w_in, hy_conv_w, hy_conv_b, hy_w1, hy_b1, hy_w2, hy_b2, hy_w3, hy_b3, hy_freq, hy_wout, hy_skip, hg_lb, hg_norm, w_proj_hy, w_proj_hg, w_out, w_router, w_gate, w_up, w_down, norm_f):
    assert w_ada.shape[0] == 1, "single-layer block"
    B, L, D = x.shape
    Lc = ctx.shape[1]
    hyw, hgw = hy_skip.shape[-1], hg_norm.shape[-1]
    H = hgw // HG_HEAD
    E = w_router.shape[-1]
    off_hg = 3 * hyw
    off_f = off_hg + hgw
    off_gate = off_hg + 5 * hgw
    in_cols = off_gate + 2 * D
    assert w_in.shape[-1] == in_cols and L % GRID_W == 0
    cap = EC_CAPACITY * L // E

    lb_all = jnp.cumsum(jax.nn.softmax(hg_lb.astype(F32), axis=0), axis=0)
    lb_f, lb_b = lb_all[0, 0], lb_all[0, 1]

    rows = -(-(B + 1) // 8) * 8
    cond = jnp.concatenate([c, c_ctx[None], jnp.zeros((rows - B - 1, D), F32)], axis=0)
    ada = _ada(cond, w_ada[0], b_ada[0])
    sh1, sc1, g1, sh2, sc2, g2 = [ada[:B, i * D:(i + 1) * D] for i in range(6)]
    csh1 = jnp.broadcast_to(ada[B:B + 1, 0:D], (B, D))
    csc1 = jnp.broadcast_to(ada[B:B + 1, D:2 * D], (B, D))

    xc_m = _normmod(ctx, norm1[0], csh1, csc1)
    pc = _mm(xc_m.reshape(B * Lc, D), w_in[0], off_f, 3 * hgw, 1024, 512, BF16, "ctx_proj")
    st_f, st_b = _hgrn_context_states(pc.reshape(B, Lc, 3 * hgw), hgw, lb_f, lb_b)

    x_m = _normmod(x, norm1[0], sh1, sc1)
    proj = _mm(x_m.reshape(B * L, D), w_in[0], 0, in_cols, 1024, 512, BF16, "in_proj")
    proj3 = proj.reshape(B, L, in_cols)

    filt = _hyena_filters(L, hy_w1[0], hy_b1[0], hy_w2[0], hy_b2[0], hy_w3[0], hy_b3[0], hy_freq[0],
                          hy_wout[0], hyw)
    hyp = _short_conv(proj3, hy_conv_w[0], hy_conv_b[0])
    fwd, inv = _dft_tables(L)
    k1 = _filter_spectrum(filt[:, 0], filt[:, 1])
    k2 = _filter_spectrum(filt[:, 2], filt[:, 3])
    z = _long_conv_gated(hyp, 0, hyp, hyw, hyw, fwd, inv, *k1, hy_skip[0, 0])
    y_hy = _long_conv_gated(z, 0, hyp, 2 * hyw, hyw, fwd, inv, *k2, hy_skip[0, 1])

    y_hg = _hgrn(proj3, off_hg, hgw, lb_f, lb_b, hg_norm[0], st_f, st_b)

    merged = _merge(y_hy.reshape(B * L, hyw), y_hg.reshape(B * L, hgw), w_proj_hy[0], w_proj_hg[0],
                    proj, off_gate)
    x1 = _mm_res(merged, w_out[0], x.reshape(B * L, D), g1, L, 1024, 512).reshape(B, L, D)

    xm2, aff = _norm_router(x1, norm2[0], sh2, sc2, w_router[0])
    aff_t = jnp.swapaxes(aff[:, :, :E], 1, 2)
    rank_row = _rank(aff_t)
    xg = _gather(rank_row, xm2, cap)
    y = _experts(xg.reshape(E, B * cap, D), w_gate[0], w_up[0], w_down[0]).reshape(E, B, cap, D)
    x2 = _combine(rank_row.reshape(B, E, L, 1), aff_t.reshape(B, E, L, 1), y, x1, g2)

    return _rmsnorm(x2.reshape(B * L, D), norm_f).reshape(B, L, D)
```

```python
import functools
import math

import jax
import jax.numpy as jnp
from jax import lax
from jax.experimental import pallas as pl
from jax.experimental.pallas import tpu as pltpu

F32 = jnp.float32
BF16 = jnp.bfloat16

EPS = 1e-6
GRID_W = 64
HY_FAST_DECAY = 0.3
HY_SLOW_DECAY = 1.5
HY_TARGET = 1e-2
HG_HEAD = 128
HG_CHUNK = 64
HG_SUB = 16
HG_HEADS_PER_STEP = 2
EC_CAPACITY = 2
LANES = 128
VMEM_LIMIT_BYTES = 56 * 1024 * 1024


def _cparams(n_axes):
    return pltpu.CompilerParams(dimension_semantics=("arbitrary",) * n_axes,
                                vmem_limit_bytes=VMEM_LIMIT_BYTES)


def _tile(pref, dim):
    t = min(pref, dim)
    while dim % t:
        t //= 2
    return t


def _dot(a, b):
    return jnp.dot(a, b, preferred_element_type=F32)


def _dot_nt(a, b):
    return lax.dot_general(a, b, (((1,), (1,)), ((), ())), preferred_element_type=F32)


def _dot_tn(a, b):
    return lax.dot_general(a, b, (((0,), (0,)), ((), ())), preferred_element_type=F32)


def _split(x):
    hi = x.astype(BF16)
    return hi, (x - hi.astype(F32)).astype(BF16)


def _dot3(a, b):
    ah, al = _split(a)
    bh, bl = _split(b)
    return _dot(ah, bh) + _dot(al, bh) + _dot(ah, bl)


def _silu(x):
    return x * jax.nn.sigmoid(x)


def _ada_kernel(c_ref, w_ref, b_ref, o_ref):
    o_ref[...] = _dot3(_silu(c_ref[...]), w_ref[...]) + b_ref[...]


def _ada(cond, w, b):
    R, D = cond.shape
    N = w.shape[1]
    tn = _tile(512, N)
    return pl.pallas_call(
        _ada_kernel, grid=(N // tn,),
        in_specs=[pl.BlockSpec((R, D), lambda n: (0, 0)),
                  pl.BlockSpec((D, tn), lambda n: (0, n)),
                  pl.BlockSpec((1, tn), lambda n: (0, n))],
        out_specs=pl.BlockSpec((R, tn), lambda n: (0, n)),
        out_shape=jax.ShapeDtypeStruct((R, N), F32),
        compiler_params=_cparams(1), name="ada")(cond, w, b.reshape(1, N))


def _rms_mod(x, g, sh, sc):
    y = x * lax.rsqrt(jnp.mean(x * x, axis=-1, keepdims=True) + EPS) * g
    return y * (1.0 + sc) + sh


def _normmod_kernel(x_ref, g_ref, sh_ref, sc_ref, o_ref):
    o_ref[0] = _rms_mod(x_ref[0], g_ref[...], sh_ref[0], sc_ref[0]).astype(o_ref.dtype)


def _normmod(x, g, sh, sc):
    B, L, D = x.shape
    tm = _tile(256, L)
    row = pl.BlockSpec((1, tm, D), lambda b, m: (b, m, 0))
    par = pl.BlockSpec((1, 1, D), lambda b, m: (b, 0, 0))
    return pl.pallas_call(
        _normmod_kernel, grid=(B, L // tm),
        in_specs=[row, pl.BlockSpec((1, D), lambda b, m: (0, 0)), par, par],
        out_specs=row, out_shape=jax.ShapeDtypeStruct((B, L, D), BF16),
        compiler_params=_cparams(2), name="normmod")(x, g.reshape(1, D), sh.reshape(B, 1, D), sc.reshape(B, 1, D))


def _rmsnorm_kernel(x_ref, g_ref, o_ref):
    x = x_ref[...]
    o_ref[...] = x * lax.rsqrt(jnp.mean(x * x, axis=-1, keepdims=True) + EPS) * g_ref[...]


def _rmsnorm(x, g):
    M, D = x.shape
    tm = _tile(256, M)
    row = pl.BlockSpec((tm, D), lambda m: (m, 0))
    return pl.pallas_call(
        _rmsnorm_kernel, grid=(M // tm,),
        in_specs=[row, pl.BlockSpec((1, D), lambda m: (0, 0))],
        out_specs=row, out_shape=jax.ShapeDtypeStruct((M, D), F32),
        compiler_params=_cparams(1), name="final_norm")(x, g.reshape(1, D))


def _mm_kernel(x_ref, w_ref, o_ref, wbf_ref):
    @pl.when(pl.program_id(1) == 0)
    def _():
        wbf_ref[...] = w_ref[...].astype(BF16)
    o_ref[...] = _dot(x_ref[...], wbf_ref[...]).astype(o_ref.dtype)


def _mm(x, w, col0, ncols, tm, tn, out_dtype, name):
    M, K = x.shape
    tm, tn = _tile(tm, M), _tile(tn, ncols)
    assert col0 % tn == 0
    c0 = col0 // tn
    return pl.pallas_call(
        _mm_kernel, grid=(ncols // tn, M // tm),
        in_specs=[pl.BlockSpec((tm, K), lambda n, m: (m, 0)),
                  pl.BlockSpec((K, tn), lambda n, m: (0, c0 + n))],
        out_specs=pl.BlockSpec((tm, tn), lambda n, m: (m, n)),
        out_shape=jax.ShapeDtypeStruct((M, ncols), out_dtype),
        scratch_shapes=[pltpu.VMEM((K, tn), BF16)],
        compiler_params=_cparams(2), name=name)(x, w)


def _mm_res_kernel(x_ref, w_ref, r_ref, g_ref, o_ref, wbf_ref):
    @pl.when(pl.program_id(1) == 0)
    def _():
        wbf_ref[...] = w_ref[...].astype(BF16)
    o_ref[...] = r_ref[...] + g_ref[0] * _dot(x_ref[...], wbf_ref[...])


def _mm_res(x, w, res, gate, rows_per_batch, tm, tn):
    M, K = x.shape
    N = w.shape[1]
    tm, tn = _tile(tm, rows_per_batch), _tile(tn, N)
    mpb = rows_per_batch // tm
    B = gate.shape[0]
    return pl.pallas_call(
        _mm_res_kernel, grid=(N // tn, M // tm),
        in_specs=[pl.BlockSpec((tm, K), lambda n, m: (m, 0)),
                  pl.BlockSpec((K, tn), lambda n, m: (0, n)),
                  pl.BlockSpec((tm, tn), lambda n, m: (m, n)),
                  pl.BlockSpec((1, 1, tn), lambda n, m: (m // mpb, 0, n))],
        out_specs=pl.BlockSpec((tm, tn), lambda n, m: (m, n)),
        out_shape=jax.ShapeDtypeStruct((M, N), F32),
        scratch_shapes=[pltpu.VMEM((K, tn), BF16)],
        compiler_params=_cparams(2), name="out_proj")(x, w, res, gate.reshape(B, 1, N))


def _merge_kernel(yhy_ref, yhg_ref, why_ref, whg_ref, ghy_ref, ghg_ref, o_ref):
    a = _dot(yhy_ref[...], why_ref[...].astype(BF16))
    b = _dot(yhg_ref[...], whg_ref[...].astype(BF16))
    o_ref[...] = (jax.nn.sigmoid(ghy_ref[...].astype(F32)) * a
                  + jax.nn.sigmoid(ghg_ref[...].astype(F32)) * b).astype(o_ref.dtype)


def _merge(yhy, yhg, why, whg, proj, off_gate):
    M, K1 = yhy.shape
    K2 = yhg.shape[1]
    D = why.shape[1]
    tm, tn = _tile(1024, M), _tile(512, D)
    assert off_gate % tn == 0
    g0, g1 = off_gate // tn, (off_gate + D) // tn
    return pl.pallas_call(
        _merge_kernel, grid=(D // tn, M // tm),
        in_specs=[pl.BlockSpec((tm, K1), lambda n, m: (m, 0)),
                  pl.BlockSpec((tm, K2), lambda n, m: (m, 0)),
                  pl.BlockSpec((K1, tn), lambda n, m: (0, n)),
                  pl.BlockSpec((K2, tn), lambda n, m: (0, n)),
                  pl.BlockSpec((tm, tn), lambda n, m: (m, g0 + n)),
                  pl.BlockSpec((tm, tn), lambda n, m: (m, g1 + n))],
        out_specs=pl.BlockSpec((tm, tn), lambda n, m: (m, n)),
        out_shape=jax.ShapeDtypeStruct((M, D), BF16),
        compiler_params=_cparams(2), name="merge")(yhy, yhg, why, whg, proj, proj)


def _sconv_kernel(p_ref, w_ref, b_ref, o_ref):
    x = p_ref[0].astype(F32)
    tl = x.shape[0]
    pos = lax.broadcasted_iota(jnp.int32, x.shape, 0) % GRID_W
    prev = jnp.where(pos != 0, pltpu.roll(x, 1, 0), 0.0)
    nxt = jnp.where(pos != GRID_W - 1, pltpu.roll(x, tl - 1, 0), 0.0)
    w = w_ref[...]
    o_ref[0] = (prev * w[0:1] + x * w[1:2] + nxt * w[2:3] + b_ref[...]).astype(o_ref.dtype)


def _short_conv(proj3, w, b):
    B, L, _ = proj3.shape
    C = w.shape[1]
    tl, tc = _tile(512, L), _tile(1024, C)
    assert tl % GRID_W == 0
    blk = pl.BlockSpec((1, tl, tc), lambda b_, l, c: (b_, l, c))
    return pl.pallas_call(
        _sconv_kernel, grid=(B, L // tl, C // tc),
        in_specs=[blk, pl.BlockSpec((3, tc), lambda b_, l, c: (0, c)),
                  pl.BlockSpec((1, tc), lambda b_, l, c: (0, c))],
        out_specs=blk, out_shape=jax.ShapeDtypeStruct((B, L, C), BF16),
        compiler_params=_cparams(3), name="short_conv")(proj3, w, b.reshape(1, C))


def _dft_tables(L):
    n = 2 * L
    k = jnp.arange(L, dtype=jnp.int32)
    ang = ((k[:, None] * k[None, :]) % n).astype(F32) * (2.0 * math.pi / n)
    alt = jnp.where(k % 2 == 0, 1.0, -1.0).astype(F32)
    cs, sn = jnp.cos(ang), -jnp.sin(ang)
    fwd = jnp.concatenate([cs, sn.at[0].set(alt)], axis=0)
    inv = jnp.concatenate([(2.0 / n) * cs.at[:, 0].set(0.5),
                           (2.0 / n) * sn.at[:, 0].set(0.5 * alt)], axis=1)
    fwd_hi, fwd_lo = _split(fwd)
    return fwd_hi, fwd_lo, inv.astype(BF16)


def _dftf_kernel(u_ref, fc_ref, fs_ref, kre_ref, kim_ref, kd_ref, yre_ref, yim_ref):
    u = u_ref[0]
    ure = _dot(fc_ref[...], u)
    uim = _dot(fs_ref[...], u)
    kim = kim_ref[0]
    yre_ref[0] = (ure * kre_ref[0] - uim * kim).astype(yre_ref.dtype)
    yim_ref[0] = (ure * kim + uim * kd_ref[0]).astype(yim_ref.dtype)


def _dfti_kernel(yre_ref, yim_ref, gc_ref, gs_ref, u_ref, gate_ref, skip_ref, o_ref):
    y = _dot(gc_ref[...], yre_ref[0]) + _dot(gs_ref[...], yim_ref[0])
    u = u_ref[0].astype(F32)
    o_ref[0] = (gate_ref[0].astype(F32) * (y + skip_ref[...] * u)).astype(o_ref.dtype)


def _long_conv_gated(u_arr, u_col0, gate_arr, gate_col0, C, fwd, inv, spectra, order, skip):
    B, L, _ = u_arr.shape
    tm, tn = _tile(512, L), _tile(512, C)
    assert u_col0 % tn == 0 and gate_col0 % tn == 0
    uc, gc = u_col0 // tn, gate_col0 // tn
    nm = L // tm
    grid = (C // tn, nm, B)
    kre, kim, kd = spectra
    kspec = pl.BlockSpec((1, tm, tn), lambda n, m, b: (order, m, n))
    yre, yim = pl.pallas_call(
        _dftf_kernel, grid=grid,
        in_specs=[pl.BlockSpec((1, L, tn), lambda n, m, b: (b, 0, uc + n)),
                  pl.BlockSpec((tm, L), lambda n, m, b: (m, 0)),
                  pl.BlockSpec((tm, L), lambda n, m, b: (nm + m, 0)),
                  kspec, kspec, kspec],
        out_specs=[pl.BlockSpec((1, tm, tn), lambda n, m, b: (b, m, n))] * 2,
        out_shape=[jax.ShapeDtypeStruct((B, L, C), BF16)] * 2,
        compiler_params=_cparams(3), name="dft_fwd")(u_arr, fwd, fwd, kre, kim, kd)
    yspec = pl.BlockSpec((1, L, tn), lambda n, m, b: (b, 0, n))
    return pl.pallas_call(
        _dfti_kernel, grid=grid,
        in_specs=[yspec, yspec,
                  pl.BlockSpec((tm, L), lambda n, m, b: (m, 0)),
                  pl.BlockSpec((tm, L), lambda n, m, b: (m, 1)),
                  pl.BlockSpec((1, tm, tn), lambda n, m, b: (b, m, uc + n)),
                  pl.BlockSpec((1, tm, tn), lambda n, m, b: (b, m, gc + n)),
                  pl.BlockSpec((1, tn), lambda n, m, b: (0, n))],
        out_specs=pl.BlockSpec((1, tm, tn), lambda n, m, b: (b, m, n)),
        out_shape=jax.ShapeDtypeStruct((B, L, C), BF16),
        compiler_params=_cparams(3), name="dft_inv")(yre, yim, inv, inv, u_arr, gate_arr, skip.reshape(1, C))


def _hy_filter_kernel(z_ref, w1_ref, b1_ref, w2_ref, b2_ref, w3_ref, b3_ref, fr_ref, wf_ref, wb_ref,
                      t_ref, ad_ref, hp_ref, hm_ref):
    fr = fr_ref[...]
    h = jnp.sin(fr * (_dot3(z_ref[...], w1_ref[...]) + b1_ref[...]))
    h = jnp.sin(fr * (_dot3(h, w2_ref[...]) + b2_ref[...]))
    h = jnp.sin(fr * (_dot3(h, w3_ref[...]) + b3_ref[...]))
    decay = jnp.exp(-t_ref[...] * ad_ref[...])
    hf = _dot3(h, wf_ref[...]) * decay
    hb = _dot3(h, wb_ref[...]) * decay
    row = lax.broadcasted_iota(jnp.int32, hb.shape, 0)
    hb = jnp.where(row == 0, 0.0, hb)
    hp_ref[0] = hf + hb
    hm_ref[0] = hf - hb


def _hy_filters(L, w1, b1, w2, b2, w3, b3, freq, wout, width):
    emb, ffn = w1.shape
    n_orders = wout.shape[1] // (2 * width)
    t = jnp.linspace(0.0, 1.0, L, dtype=F32)[:, None]
    bands = (emb - 1) // 2
    w = 2.0 * math.pi * jnp.arange(L, dtype=F32) / L
    f = jnp.linspace(1e-4, bands - 1, bands, dtype=F32)
    fw = w[:, None] * f[None, :]
    z = jnp.concatenate([t, jnp.cos(fw), -jnp.sin(fw)], axis=-1)
    deltas = jnp.linspace(math.log(HY_TARGET) / HY_SLOW_DECAY, math.log(HY_TARGET) / HY_FAST_DECAY,
                          width, dtype=F32)
    pe, pf = LANES - emb, LANES - ffn
    z = jnp.pad(z, ((0, 0), (0, pe)))
    w1p = jnp.pad(w1, ((0, pe), (0, pf)))
    w2p = jnp.pad(w2, ((0, pf), (0, pf)))
    w3p = jnp.pad(w3, ((0, pf), (0, pf)))
    woutp = jnp.pad(wout, ((0, pf), (0, 0)))
    vec = lambda a: jnp.pad(a, (0, pf)).reshape(1, LANES)
    tc = _tile(512, width)
    nt = width // tc
    full = lambda shp: pl.BlockSpec(shp, lambda o, n: (0, 0))
    out = pl.BlockSpec((1, L, tc), lambda o, n: (o, 0, n))
    shp = jax.ShapeDtypeStruct((n_orders, L, width), F32)
    return pl.pallas_call(
        _hy_filter_kernel, grid=(n_orders, nt),
        in_specs=[full((L, LANES)), full((LANES, LANES)), full((1, LANES)), full((LANES, LANES)), full((1, LANES)),
                  full((LANES, LANES)), full((1, LANES)), full((1, LANES)),
                  pl.BlockSpec((LANES, tc), lambda o, n: (0, (2 * o) * nt + n)),
                  pl.BlockSpec((LANES, tc), lambda o, n: (0, (2 * o + 1) * nt + n)),
                  full((L, 1)), pl.BlockSpec((1, tc), lambda o, n: (0, n))],
        out_specs=[out, out], out_shape=[shp, shp],
        compiler_params=_cparams(2), name="hy_filter")(
            z, w1p, vec(b1), w2p, vec(b2), w3p, vec(b3), vec(freq), woutp, woutp, t, jnp.abs(deltas).reshape(1, width))


def _spectrum_kernel(fch_ref, fcl_ref, fsh_ref, fsl_ref, hp_ref, hm_ref, kre_ref, kim_ref, kd_ref):
    tm = fch_ref.shape[0]
    hph, hpl = _split(hp_ref[0])
    hmh, hml = _split(hm_ref[0])

    def dot3(ah, al, bh, bl):
        return _dot(ah, bh) + _dot(al, bh) + _dot(ah, bl)

    kre = dot3(fch_ref[...], fcl_ref[...], hph, hpl)
    kim = dot3(fsh_ref[...], fsl_ref[...], hmh, hml)
    nyq = dot3(fsh_ref[0:8, :], fsl_ref[0:8, :], hph, hpl)[0:1, :]
    first = (lax.broadcasted_iota(jnp.int32, kre.shape, 0) + pl.program_id(2) * tm) == 0
    kre_ref[0] = kre
    kim_ref[0] = jnp.where(first, 0.0, kim)
    kd_ref[0] = jnp.where(first, nyq, kre)


def _filter_spectra(hp, hm, fwd_hi, fwd_lo):
    n_orders, L, width = hp.shape
    tm, tn = _tile(512, L), _tile(512, width)
    nm = L // tm
    cs = pl.BlockSpec((tm, L), lambda o, n, m: (m, 0))
    sn = pl.BlockSpec((tm, L), lambda o, n, m: (nm + m, 0))
    hs = pl.BlockSpec((1, L, tn), lambda o, n, m: (o, 0, n))
    out = pl.BlockSpec((1, tm, tn), lambda o, n, m: (o, m, n))
    shp = jax.ShapeDtypeStruct((n_orders, L, width), F32)
    return pl.pallas_call(
        _spectrum_kernel, grid=(n_orders, width // tn, nm),
        in_specs=[cs, cs, sn, sn, hs, hs], out_specs=[out, out, out], out_shape=[shp, shp, shp],
        compiler_params=_cparams(3), name="hy_spectrum")(fwd_hi, fwd_lo, fwd_hi, fwd_lo, hp, hm)


def _hg_gate(z, lb):
    f = lb + (1.0 - lb) * jax.nn.sigmoid(z)
    return jnp.log(f), 1.0 - f


def _chunk_cumsum(g, rev):
    n = g.shape[0]
    pos = lax.broadcasted_iota(jnp.int32, g.shape, 0) % HG_CHUNK
    d = 1
    while d < HG_CHUNK:
        if rev:
            g = g + jnp.where(pos < HG_CHUNK - d, pltpu.roll(g, n - d, 0), 0.0)
        else:
            g = g + jnp.where(pos >= d, pltpu.roll(g, d, 0), 0.0)
        d *= 2
    return g


def _row_at(b, step, rev):
    r = HG_CHUNK - 1 - step if rev else step
    return b[r:r + 1, :]


def _hg_state(k, v, b, st, rev):
    bl = _row_at(b, HG_CHUNK - 1, rev)
    return st * jnp.exp(bl) + _dot_tn(v.astype(BF16), (k * jnp.exp(bl - b)).astype(BF16))


def _hg_chunk(q, k, v, b, st, rev):
    C, SB = HG_CHUNK, HG_SUB
    nsb = C // SB
    row = lax.broadcasted_iota(jnp.int32, q.shape, 0)
    p = (C - 1 - row) if rev else row
    mids = _row_at(b, SB // 2, rev)
    ends = _row_at(b, SB - 1, rev)
    for i in range(1, nsb):
        mids = jnp.where(p >= i * SB, _row_at(b, i * SB + SB // 2, rev), mids)
        ends = jnp.where(p >= i * SB, _row_at(b, i * SB + SB - 1, rev), ends)
    tt = lax.broadcasted_iota(jnp.int32, (C, C), 0)
    ss = lax.broadcasted_iota(jnp.int32, (C, C), 1)
    if rev:
        tt, ss = C - 1 - tt, C - 1 - ss
    ad = _dot_nt((q * jnp.exp(b - mids)).astype(BF16), (k * jnp.exp(mids - b)).astype(BF16))
    a = jnp.where((tt // SB == ss // SB) & (ss <= tt), ad, 0.0)
    ke = k * jnp.exp(ends - b)
    qs, ks = [], []
    for j in range(nsb - 1):
        ej = _row_at(b, j * SB + SB - 1, rev)
        qs.append(jnp.where(p >= (j + 1) * SB, q * jnp.exp(jnp.minimum(b - ej, 0.0)), 0.0).astype(BF16))
        ks.append(jnp.where((p >= j * SB) & (p < (j + 1) * SB), ke, 0.0).astype(BF16))
    a = a + _dot_nt(jnp.concatenate(qs, axis=1), jnp.concatenate(ks, axis=1))
    o = _dot(a.astype(BF16), v.astype(BF16)) + _dot_nt((q * jnp.exp(b)).astype(BF16), st.astype(BF16))
    return o, _hg_state(k, v, b, st, rev)


def _hgrn_kernel(q_ref, zf_ref, zb_ref, i_ref, g_ref, lbf_ref, lbb_ref, gn_ref, sf_ref, sb_ref, o_ref,
                 q_s, kf_s, kb_s, v_s, bf_s, cb_s, of_s, ob_s):
    C = HG_CHUNK
    nc = q_s.shape[0] // C
    nh = q_s.shape[1] // HG_HEAD
    q_s[...] = _silu(q_ref[0].astype(F32)) * (HG_HEAD ** -0.5)
    gf, kf = _hg_gate(zf_ref[0].astype(F32), lbf_ref[0])
    kf_s[...] = kf
    bf_s[...] = _chunk_cumsum(gf, False)
    gb, kb = _hg_gate(zb_ref[0].astype(F32), lbb_ref[0])
    kb_s[...] = kb
    cb_s[...] = _chunk_cumsum(gb, True)
    v_s[...] = i_ref[0].astype(F32)

    def body(c, carry):
        rf = pl.ds(pl.multiple_of(c * C, C), C)
        rb = pl.ds(pl.multiple_of((nc - 1 - c) * C, C), C)
        out = []
        for h in range(nh):
            ln = slice(h * HG_HEAD, (h + 1) * HG_HEAD)
            of, stf = _hg_chunk(q_s[rf, ln], kf_s[rf, ln], v_s[rf, ln], bf_s[rf, ln], carry[2 * h], False)
            ob, stb = _hg_chunk(q_s[rb, ln], kb_s[rb, ln], v_s[rb, ln], cb_s[rb, ln], carry[2 * h + 1], True)
            of_s[rf, ln] = of
            ob_s[rb, ln] = ob
            out += [stf, stb]
        return tuple(out)

    init = []
    for h in range(nh):
        init += [sf_ref[0, h], sb_ref[0, h]]
    lax.fori_loop(0, nc, body, tuple(init))
    for h in range(nh):
        ln = slice(h * HG_HEAD, (h + 1) * HG_HEAD)
        o = of_s[:, ln] + ob_s[:, ln]
        o = o * lax.rsqrt(jnp.mean(o * o, axis=-1, keepdims=True) + EPS)
        o_ref[0, :, ln] = (o * gn_ref[0, :, ln] * _silu(g_ref[0, :, ln].astype(F32))).astype(o_ref.dtype)


def _hgrn(proj3, off_hg, width, lb_f, lb_b, gn, st_f, st_b):
    B, L, _ = proj3.shape
    H = width // HG_HEAD
    nh = _tile(HG_HEADS_PER_STEP, H)
    wb = nh * HG_HEAD
    assert L % HG_CHUNK == 0 and off_hg % wb == 0
    c0, hb = off_hg // wb, H // nh

    def col(i):
        return pl.BlockSpec((1, L, wb), lambda b, h: (b, 0, c0 + i * hb + h))

    par = pl.BlockSpec((1, 1, wb), lambda b, h: (0, 0, h))
    st = pl.BlockSpec((1, nh, HG_HEAD, HG_HEAD), lambda b, h: (b, h, 0, 0))
    return pl.pallas_call(
        _hgrn_kernel, grid=(B, hb),
        in_specs=[col(0), col(1), col(2), col(3), col(4), par, par, par, st, st],
        out_specs=pl.BlockSpec((1, L, wb), lambda b, h: (b, 0, h)),
        out_shape=jax.ShapeDtypeStruct((B, L, width), BF16),
        scratch_shapes=[pltpu.VMEM((L, wb), F32)] * 8,
        compiler_params=_cparams(2), name="hgrn")(
            proj3, proj3, proj3, proj3, proj3,
            lb_f.reshape(1, 1, width), lb_b.reshape(1, 1, width), gn.reshape(1, 1, width), st_f, st_b)


def _hgctx_kernel(zf_ref, zb_ref, i_ref, lbf_ref, lbb_ref, sf_ref, sb_ref):
    C = HG_CHUNK
    nc = zf_ref.shape[1] // C
    gf, kf = _hg_gate(zf_ref[0].astype(F32), lbf_ref[0])
    gb, kb = _hg_gate(zb_ref[0].astype(F32), lbb_ref[0])
    bf = _chunk_cumsum(gf, False)
    cb = _chunk_cumsum(gb, True)
    v = i_ref[0].astype(F32)
    stf = jnp.zeros((HG_HEAD, HG_HEAD), F32)
    stb = jnp.zeros((HG_HEAD, HG_HEAD), F32)
    for c in range(nc):
        sl = slice(c * C, (c + 1) * C)
        stf = _hg_state(kf[sl], v[sl], bf[sl], stf, False)
    for c in reversed(range(nc)):
        sl = slice(c * C, (c + 1) * C)
        stb = _hg_state(kb[sl], v[sl], cb[sl], stb, True)
    sf_ref[0, 0] = stf
    sb_ref[0, 0] = stb


def _hgrn_context_states(pc3, width, lb_f, lb_b):
    B, Lc, _ = pc3.shape
    H = width // HG_HEAD
    assert Lc % HG_CHUNK == 0

    def col(i):
        return pl.BlockSpec((1, Lc, HG_HEAD), lambda b, h: (b, 0, i * H + h))

    par = pl.BlockSpec((1, 1, HG_HEAD), lambda b, h: (h, 0, 0))
    st = pl.BlockSpec((1, 1, HG_HEAD, HG_HEAD), lambda b, h: (b, h, 0, 0))
    shp = jax.ShapeDtypeStruct((B, H, HG_HEAD, HG_HEAD), F32)
    return pl.pallas_call(
        _hgctx_kernel, grid=(B, H),
        in_specs=[col(0), col(1), col(2), par, par],
        out_specs=[st, st], out_shape=[shp, shp],
        compiler_params=_cparams(2), name="hgrn_ctx")(
            pc3, pc3, pc3, lb_f.reshape(H, 1, HG_HEAD), lb_b.reshape(H, 1, HG_HEAD))


def _norm_router_kernel(x_ref, g_ref, sh_ref, sc_ref, wr_ref, xm_ref, aff_ref, *, n_experts):
    xm = _rms_mod(x_ref[0], g_ref[...], sh_ref[0], sc_ref[0])
    xm_ref[0] = xm.astype(xm_ref.dtype)
    logits = _dot3(xm, wr_ref[...])
    lane = lax.broadcasted_iota(jnp.int32, logits.shape, 1)
    logits = jnp.where(lane < n_experts, logits, -1e30)
    e = jnp.exp(logits - jnp.max(logits, axis=-1, keepdims=True))
    aff_ref[0] = e / jnp.sum(e, axis=-1, keepdims=True)


def _norm_router(x, g, sh, sc, w_router):
    B, L, D = x.shape
    E = w_router.shape[1]
    tm = _tile(256, L)
    wr = jnp.pad(w_router, ((0, 0), (0, LANES - E)))
    row = pl.BlockSpec((1, tm, D), lambda b, m: (b, m, 0))
    par = pl.BlockSpec((1, 1, D), lambda b, m: (b, 0, 0))
    return pl.pallas_call(
        functools.partial(_norm_router_kernel, n_experts=E), grid=(B, L // tm),
        in_specs=[row, pl.BlockSpec((1, D), lambda b, m: (0, 0)), par, par,
                  pl.BlockSpec((D, LANES), lambda b, m: (0, 0))],
        out_specs=[row, pl.BlockSpec((1, tm, LANES), lambda b, m: (b, m, 0))],
        out_shape=[jax.ShapeDtypeStruct((B, L, D), BF16), jax.ShapeDtypeStruct((B, L, LANES), F32)],
        compiler_params=_cparams(2), name="norm_router")(
            x, g.reshape(1, D), sh.reshape(B, 1, D), sc.reshape(B, 1, D), wr)


def _route_kernel(a_ref, tri_ref, slot_ref, *, cap):
    bits = lax.bitcast_convert_type(a_ref[...], jnp.int32)

    def count(mask):
        return jnp.sum(jnp.where(mask, 1.0, 0.0), axis=-1, keepdims=True)

    def body(i, thr):
        cand = thr | jnp.left_shift(jnp.int32(1), 30 - i)
        return jnp.where(count(bits >= cand) >= cap, cand, thr)

    thr = lax.fori_loop(0, 31, body, jnp.zeros((bits.shape[0], 1), jnp.int32))
    above, tie = bits > thr, bits == thr
    tri = tri_ref[...]
    ties_before = _dot(jnp.where(tie, 1.0, 0.0).astype(BF16), tri)
    chosen = above | (tie & (ties_before < cap - count(above)))
    before = _dot(jnp.where(chosen, 1.0, 0.0).astype(BF16), tri)
    slot_ref[...] = jnp.where(chosen, before.astype(jnp.int32), -1)


def _route(aff_t, cap):
    B, E, L = aff_t.shape
    idx = jnp.arange(L, dtype=jnp.int32)
    tri = (idx[:, None] < idx[None, :]).astype(BF16)
    slot = pl.pallas_call(
        functools.partial(_route_kernel, cap=cap), grid=(1,),
        in_specs=[pl.BlockSpec((B * E, L), lambda i: (0, 0)), pl.BlockSpec((L, L), lambda i: (0, 0))],
        out_specs=pl.BlockSpec((B * E, L), lambda i: (0, 0)),
        out_shape=jax.ShapeDtypeStruct((B * E, L), jnp.int32),
        compiler_params=_cparams(1), name="route")(aff_t.reshape(B * E, L), tri)
    return slot.reshape(B, E, 1, L)


def _gather_kernel(rank_ref, xm_ref, o_ref):
    rk = rank_ref[0, 0]
    cap = o_ref.shape[2]
    slot = lax.broadcasted_iota(jnp.int32, (cap, rk.shape[1]), 0)
    sel = jnp.where(slot == rk, 1.0, 0.0).astype(BF16)
    o_ref[0, 0] = _dot(sel, xm_ref[0]).astype(o_ref.dtype)


def _gather(rank_row, xm, cap):
    B, E, _, L = rank_row.shape
    D = xm.shape[2]
    td = _tile(1024, D)
    return pl.pallas_call(
        _gather_kernel, grid=(B, D // td, E),
        in_specs=[pl.BlockSpec((1, 1, 1, L), lambda b, d, e: (b, e, 0, 0)),
                  pl.BlockSpec((1, L, td), lambda b, d, e: (b, 0, d))],
        out_specs=pl.BlockSpec((1, 1, cap, td), lambda b, d, e: (e, b, 0, d)),
        out_shape=jax.ShapeDtypeStruct((E, B, cap, D), BF16),
        compiler_params=_cparams(3), name="moe_gather")(rank_row, xm)


def _expert_up_kernel(x_ref, wg_ref, wu_ref, h_ref):
    x = x_ref[0]
    a = _dot(x, wg_ref[0].astype(BF16))
    u = _dot(x, wu_ref[0].astype(BF16))
    h_ref[0] = (_silu(a) * u).astype(h_ref.dtype)


def _expert_down_kernel(h_ref, wd_ref, y_ref):
    y_ref[0] = _dot(h_ref[0], wd_ref[0].astype(BF16)).astype(y_ref.dtype)


def _experts(xg, w_gate, w_up, w_down):
    E, M, D = xg.shape
    FF = w_gate.shape[2]
    tf, td = _tile(256, FF), _tile(512, D)
    h = pl.pallas_call(
        _expert_up_kernel, grid=(E, FF // tf),
        in_specs=[pl.BlockSpec((1, M, D), lambda e, n: (e, 0, 0)),
                  pl.BlockSpec((1, D, tf), lambda e, n: (e, 0, n)),
                  pl.BlockSpec((1, D, tf), lambda e, n: (e, 0, n))],
        out_specs=pl.BlockSpec((1, M, tf), lambda e, n: (e, 0, n)),
        out_shape=jax.ShapeDtypeStruct((E, M, FF), BF16),
        compiler_params=_cparams(2), name="expert_up")(xg, w_gate, w_up)
    return pl.pallas_call(
        _expert_down_kernel, grid=(E, D // td),
        in_specs=[pl.BlockSpec((1, M, FF), lambda e, n: (e, 0, 0)),
                  pl.BlockSpec((1, FF, td), lambda e, n: (e, 0, n))],
        out_specs=pl.BlockSpec((1, M, td), lambda e, n: (e, 0, n)),
        out_shape=jax.ShapeDtypeStruct((E, M, D), BF16),
        compiler_params=_cparams(2), name="expert_down")(h, w_down)


def _combine_kernel(rk_ref, af_ref, y_ref, x_ref, g_ref, o_ref, w_s):
    E, _, cap, td = y_ref.shape

    @pl.when(pl.program_id(2) == 0)
    def _():
        slot = lax.broadcasted_iota(jnp.int32, (w_s.shape[0], cap), 1)
        for e in range(E):
            w_s[:, e * cap:(e + 1) * cap] = jnp.where(rk_ref[0, e] == slot, af_ref[0, e], 0.0).astype(w_s.dtype)

    y = y_ref[:, 0].reshape(E * cap, td)
    o_ref[0] = x_ref[0] + g_ref[0] * _dot(w_s[...], y)


def _combine(rank_col, aff_col, y, x, gate):
    B, E, L, _ = rank_col.shape
    cap, D = y.shape[2], y.shape[3]
    tm, td = _tile(512, L), _tile(512, D)
    colspec = pl.BlockSpec((1, E, tm, 1), lambda b, m, d: (b, 0, m, 0))
    xspec = pl.BlockSpec((1, tm, td), lambda b, m, d: (b, m, d))
    return pl.pallas_call(
        _combine_kernel, grid=(B, L // tm, D // td),
        in_specs=[colspec, colspec,
                  pl.BlockSpec((E, 1, cap, td), lambda b, m, d: (0, b, 0, d)),
                  xspec, pl.BlockSpec((1, 1, td), lambda b, m, d: (b, 0, d))],
        out_specs=xspec, out_shape=jax.ShapeDtypeStruct((B, L, D), F32),
        scratch_shapes=[pltpu.VMEM((tm, E * cap), BF16)],
        compiler_params=_cparams(3), name="moe_combine")(rank_col, aff_col, y, x, gate.reshape(B, 1, D))


def kernel(x, c, ctx, c_ctx, w_ada, b_ada, norm1, norm2, w_in, hy_conv_w, hy_conv_b, hy_w1, hy_b1, hy_w2, hy_b2, hy_w3, hy_b3, hy_freq, hy_wout, hy_skip, hg_lb, hg_norm, w_proj_hy, w_proj_hg, w_out, w_router, w_gate, w_up, w_down, norm_f):
    assert w_ada.shape[0] == 1, "single-layer block"
    B, L, D = x.shape
    Lc = ctx.shape[1]
    hyw, hgw = hy_skip.shape[-1], hg_norm.shape[-1]
    H = hgw // HG_HEAD
    E = w_router.shape[-1]
    off_hg = 3 * hyw
    off_f = off_hg + hgw
    off_gate = off_hg + 5 * hgw
    in_cols = off_gate + 2 * D
    assert w_in.shape[-1] == in_cols and L % GRID_W == 0
    cap = EC_CAPACITY * L // E

    lb_all = jnp.cumsum(jax.nn.softmax(hg_lb.astype(F32), axis=0), axis=0)
    lb_f, lb_b = lb_all[0, 0], lb_all[0, 1]

    rows = -(-(B + 1) // 8) * 8
    cond = jnp.concatenate([c, c_ctx[None], jnp.zeros((rows - B - 1, D), F32)], axis=0)
    ada = _ada(cond, w_ada[0], b_ada[0])
    sh1, sc1, g1, sh2, sc2, g2 = [ada[:B, i * D:(i + 1) * D] for i in range(6)]
    csh1 = jnp.broadcast_to(ada[B:B + 1, 0:D], (B, D))
    csc1 = jnp.broadcast_to(ada[B:B + 1, D:2 * D], (B, D))

    xc_m = _normmod(ctx, norm1[0], csh1, csc1)
    pc = _mm(xc_m.reshape(B * Lc, D), w_in[0], off_f, 3 * hgw, 1024, 512, BF16, "ctx_proj")
    st_f, st_b = _hgrn_context_states(pc.reshape(B, Lc, 3 * hgw), hgw, lb_f, lb_b)

    x_m = _normmod(x, norm1[0], sh1, sc1)
    proj = _mm(x_m.reshape(B * L, D), w_in[0], 0, in_cols, 1024, 512, BF16, "in_proj")
    proj3 = proj.reshape(B, L, in_cols)

    fwd, fwd_lo, inv = _dft_tables(L)
    hp, hm = _hy_filters(L, hy_w1[0], hy_b1[0], hy_w2[0], hy_b2[0], hy_w3[0], hy_b3[0], hy_freq[0],
                         hy_wout[0], hyw)
    spectra = _filter_spectra(hp, hm, fwd, fwd_lo)
    hyp = _short_conv(proj3, hy_conv_w[0], hy_conv_b[0])
    z = _long_conv_gated(hyp, 0, hyp, hyw, hyw, fwd, inv, spectra, 0, hy_skip[0, 0])
    y_hy = _long_conv_gated(z, 0, hyp, 2 * hyw, hyw, fwd, inv, spectra, 1, hy_skip[0, 1])

    y_hg = _hgrn(proj3, off_hg, hgw, lb_f, lb_b, hg_norm[0], st_f, st_b)

    merged = _merge(y_hy.reshape(B * L, hyw), y_hg.reshape(B * L, hgw), w_proj_hy[0], w_proj_hg[0],
                    proj, off_gate)
    x1 = _mm_res(merged, w_out[0], x.reshape(B * L, D), g1, L, 1024, 512).reshape(B, L, D)

    xm2, aff = _norm_router(x1, norm2[0], sh2, sc2, w_router[0])
    aff_t = jnp.swapaxes(aff[:, :, :E], 1, 2)
    slot_row = _route(aff_t, cap)
    xg = _gather(slot_row, xm2, cap)
    y = _experts(xg.reshape(E, B * cap, D), w_gate[0], w_up[0], w_down[0]).reshape(E, B, cap, D)
    x2 = _combine(slot_row.reshape(B, E, L, 1), aff_t.reshape(B, E, L, 1), y, x1, g2)

    return _rmsnorm(x2.reshape(B * L, D), norm_f).reshape(B, L, D)
```

```python
import functools
import math

import jax
import jax.numpy as jnp
from jax import lax
from jax.experimental import pallas as pl
from jax.experimental.pallas import tpu as pltpu

F32 = jnp.float32
BF16 = jnp.bfloat16

EPS = 1e-6
GRID_W = 64
HY_FAST_DECAY = 0.3
HY_SLOW_DECAY = 1.5
HY_TARGET = 1e-2
HG_HEAD = 128
HG_CHUNK = 128
HG_SUB = 32
HG_HEADS_PER_STEP = 2
EC_CAPACITY = 2
LANES = 128
VMEM_LIMIT_BYTES = 56 * 1024 * 1024


def _cparams(n_axes):
    return pltpu.CompilerParams(dimension_semantics=("arbitrary",) * n_axes,
                                vmem_limit_bytes=VMEM_LIMIT_BYTES)


def _tile(pref, dim):
    t = min(pref, dim)
    while dim % t:
        t //= 2
    return t


def _dot(a, b):
    return jnp.dot(a, b, preferred_element_type=F32)


def _dot_nt(a, b):
    return lax.dot_general(a, b, (((1,), (1,)), ((), ())), preferred_element_type=F32)


def _dot_tn(a, b):
    return lax.dot_general(a, b, (((0,), (0,)), ((), ())), preferred_element_type=F32)


def _split(x):
    hi = x.astype(BF16)
    return hi, (x - hi.astype(F32)).astype(BF16)


def _dot3(a, b):
    ah, al = _split(a)
    bh, bl = _split(b)
    return _dot(ah, bh) + _dot(al, bh) + _dot(ah, bl)


def _silu(x):
    return x * jax.nn.sigmoid(x)


def _ada_kernel(c_ref, w_ref, b_ref, o_ref):
    o_ref[...] = _dot3(_silu(c_ref[...]), w_ref[...]) + b_ref[...]


def _ada(cond, w, b):
    R, D = cond.shape
    N = w.shape[1]
    tn = _tile(512, N)
    return pl.pallas_call(
        _ada_kernel, grid=(N // tn,),
        in_specs=[pl.BlockSpec((R, D), lambda n: (0, 0)),
                  pl.BlockSpec((D, tn), lambda n: (0, n)),
                  pl.BlockSpec((1, tn), lambda n: (0, n))],
        out_specs=pl.BlockSpec((R, tn), lambda n: (0, n)),
        out_shape=jax.ShapeDtypeStruct((R, N), F32),
        compiler_params=_cparams(1), name="ada")(cond, w, b.reshape(1, N))


def _rms_mod(x, g, sh, sc):
    y = x * lax.rsqrt(jnp.mean(x * x, axis=-1, keepdims=True) + EPS) * g
    return y * (1.0 + sc) + sh


def _normmod_kernel(x_ref, g_ref, sh_ref, sc_ref, o_ref):
    o_ref[0] = _rms_mod(x_ref[0], g_ref[...], sh_ref[0], sc_ref[0]).astype(o_ref.dtype)


def _normmod(x, g, sh, sc):
    B, L, D = x.shape
    tm = _tile(256, L)
    row = pl.BlockSpec((1, tm, D), lambda b, m: (b, m, 0))
    par = pl.BlockSpec((1, 1, D), lambda b, m: (b, 0, 0))
    return pl.pallas_call(
        _normmod_kernel, grid=(B, L // tm),
        in_specs=[row, pl.BlockSpec((1, D), lambda b, m: (0, 0)), par, par],
        out_specs=row, out_shape=jax.ShapeDtypeStruct((B, L, D), BF16),
        compiler_params=_cparams(2), name="normmod")(x, g.reshape(1, D), sh.reshape(B, 1, D), sc.reshape(B, 1, D))


def _rmsnorm_kernel(x_ref, g_ref, o_ref):
    x = x_ref[...]
    o_ref[...] = x * lax.rsqrt(jnp.mean(x * x, axis=-1, keepdims=True) + EPS) * g_ref[...]


def _rmsnorm(x, g):
    M, D = x.shape
    tm = _tile(256, M)
    row = pl.BlockSpec((tm, D), lambda m: (m, 0))
    return pl.pallas_call(
        _rmsnorm_kernel, grid=(M // tm,),
        in_specs=[row, pl.BlockSpec((1, D), lambda m: (0, 0))],
        out_specs=row, out_shape=jax.ShapeDtypeStruct((M, D), F32),
        compiler_params=_cparams(1), name="final_norm")(x, g.reshape(1, D))


def _mm_kernel(x_ref, w_ref, o_ref, wbf_ref):
    @pl.when(pl.program_id(1) == 0)
    def _():
        wbf_ref[...] = w_ref[...].astype(BF16)
    o_ref[...] = _dot(x_ref[...], wbf_ref[...]).astype(o_ref.dtype)


def _mm(x, w, col0, ncols, tm, tn, out_dtype, name):
    M, K = x.shape
    tm, tn = _tile(tm, M), _tile(tn, ncols)
    assert col0 % tn == 0
    c0 = col0 // tn
    return pl.pallas_call(
        _mm_kernel, grid=(ncols // tn, M // tm),
        in_specs=[pl.BlockSpec((tm, K), lambda n, m: (m, 0)),
                  pl.BlockSpec((K, tn), lambda n, m: (0, c0 + n))],
        out_specs=pl.BlockSpec((tm, tn), lambda n, m: (m, n)),
        out_shape=jax.ShapeDtypeStruct((M, ncols), out_dtype),
        scratch_shapes=[pltpu.VMEM((K, tn), BF16)],
        compiler_params=_cparams(2), name=name)(x, w)


def _mm_res_kernel(x_ref, w_ref, r_ref, g_ref, o_ref, wbf_ref):
    @pl.when(pl.program_id(1) == 0)
    def _():
        wbf_ref[...] = w_ref[...].astype(BF16)
    o_ref[...] = r_ref[...] + g_ref[0] * _dot(x_ref[...], wbf_ref[...])


def _mm_res(x, w, res, gate, rows_per_batch, tm, tn):
    M, K = x.shape
    N = w.shape[1]
    tm, tn = _tile(tm, rows_per_batch), _tile(tn, N)
    mpb = rows_per_batch // tm
    B = gate.shape[0]
    return pl.pallas_call(
        _mm_res_kernel, grid=(N // tn, M // tm),
        in_specs=[pl.BlockSpec((tm, K), lambda n, m: (m, 0)),
                  pl.BlockSpec((K, tn), lambda n, m: (0, n)),
                  pl.BlockSpec((tm, tn), lambda n, m: (m, n)),
                  pl.BlockSpec((1, 1, tn), lambda n, m: (m // mpb, 0, n))],
        out_specs=pl.BlockSpec((tm, tn), lambda n, m: (m, n)),
        out_shape=jax.ShapeDtypeStruct((M, N), F32),
        scratch_shapes=[pltpu.VMEM((K, tn), BF16)],
        compiler_params=_cparams(2), name="out_proj")(x, w, res, gate.reshape(B, 1, N))


def _merge_kernel(yhy_ref, yhg_ref, why_ref, whg_ref, ghy_ref, ghg_ref, o_ref):
    a = _dot(yhy_ref[...], why_ref[...].astype(BF16))
    b = _dot(yhg_ref[...], whg_ref[...].astype(BF16))
    o_ref[...] = (jax.nn.sigmoid(ghy_ref[...].astype(F32)) * a
                  + jax.nn.sigmoid(ghg_ref[...].astype(F32)) * b).astype(o_ref.dtype)


def _merge(yhy, yhg, why, whg, proj, off_gate):
    M, K1 = yhy.shape
    K2 = yhg.shape[1]
    D = why.shape[1]
    tm, tn = _tile(1024, M), _tile(512, D)
    assert off_gate % tn == 0
    g0, g1 = off_gate // tn, (off_gate + D) // tn
    return pl.pallas_call(
        _merge_kernel, grid=(D // tn, M // tm),
        in_specs=[pl.BlockSpec((tm, K1), lambda n, m: (m, 0)),
                  pl.BlockSpec((tm, K2), lambda n, m: (m, 0)),
                  pl.BlockSpec((K1, tn), lambda n, m: (0, n)),
                  pl.BlockSpec((K2, tn), lambda n, m: (0, n)),
                  pl.BlockSpec((tm, tn), lambda n, m: (m, g0 + n)),
                  pl.BlockSpec((tm, tn), lambda n, m: (m, g1 + n))],
        out_specs=pl.BlockSpec((tm, tn), lambda n, m: (m, n)),
        out_shape=jax.ShapeDtypeStruct((M, D), BF16),
        compiler_params=_cparams(2), name="merge")(yhy, yhg, why, whg, proj, proj)


def _sconv_kernel(p_ref, w_ref, b_ref, o_ref, y_s):
    x = p_ref[0].astype(F32)
    tl = x.shape[0]
    pos = lax.broadcasted_iota(jnp.int32, x.shape, 0) % GRID_W
    prev = jnp.where(pos != 0, pltpu.roll(x, 1, 0), 0.0)
    nxt = jnp.where(pos != GRID_W - 1, pltpu.roll(x, tl - 1, 0), 0.0)
    w = w_ref[...]
    y = prev * w[0:1] + x * w[1:2] + nxt * w[2:3] + b_ref[...]
    for g in range(y_s.shape[0]):
        ln = slice(g * LANES, (g + 1) * LANES)
        y_s[g] = y[:, ln]
        o_ref[0, 0, :, ln] = y_s[g, pl.ds(0, tl // 2, stride=2), :].astype(o_ref.dtype)
        o_ref[0, 1, :, ln] = y_s[g, pl.ds(1, tl // 2, stride=2), :].astype(o_ref.dtype)


def _short_conv(proj3, w, b):
    B, L, _ = proj3.shape
    C = w.shape[1]
    tl, tc = _tile(512, L), _tile(1024, C)
    assert tl % GRID_W == 0
    return pl.pallas_call(
        _sconv_kernel, grid=(B, L // tl, C // tc),
        in_specs=[pl.BlockSpec((1, tl, tc), lambda b_, l, c: (b_, l, c)),
                  pl.BlockSpec((3, tc), lambda b_, l, c: (0, c)),
                  pl.BlockSpec((1, tc), lambda b_, l, c: (0, c))],
        out_specs=pl.BlockSpec((1, 2, tl // 2, tc), lambda b_, l, c: (b_, 0, l, c)),
        out_shape=jax.ShapeDtypeStruct((B, 2, L // 2, C), BF16),
        scratch_shapes=[pltpu.VMEM((tc // LANES, tl, LANES), F32)],
        compiler_params=_cparams(3), name="short_conv")(proj3, w, b.reshape(1, C))


def _dft_tables(L):
    lh, n = L // 2, 2 * L
    k = jnp.arange(lh, dtype=jnp.int32)
    alt = jnp.where(k % 2 == 0, 1.0, -1.0).astype(F32)
    ang_e = ((k[:, None] * (2 * k)[None, :]) % n).astype(F32) * (2.0 * math.pi / n)
    ang_o = ((k[:, None] * (2 * k + 1)[None, :]) % n).astype(F32) * (2.0 * math.pi / n)
    ce, co, se, so = jnp.cos(ang_e), jnp.cos(ang_o), -jnp.sin(ang_e), -jnp.sin(ang_o)
    fwd = jnp.stack([ce, co, se.at[0].set(alt), so.at[0].set(-alt)])
    inv = (2.0 / n) * jnp.stack([ce.T.at[:, 0].set(0.5), co.T.at[:, 0].set(0.5),
                                 se.T.at[:, 0].set(alt), so.T.at[:, 0].set(-alt)])
    fwd_hi, fwd_lo = _split(fwd)
    return fwd_hi, fwd_lo, inv.astype(BF16)


def _dftf_kernel(ue_ref, uo_ref, t_ref, krl_ref, krh_ref, kil_ref, kih_ref, ae_ref, ao_ref, be_ref, bo_ref):
    tm = t_ref.shape[1]
    ue, uo = ue_ref[0, 0], uo_ref[0, 0]
    p, q = _dot(t_ref[0], ue), _dot(t_ref[1], uo)
    r, t = _dot(t_ref[2], ue), _dot(t_ref[3], uo)
    first = (lax.broadcasted_iota(jnp.int32, p.shape, 0) + pl.program_id(1) * tm) == 0
    krl, krh, kil, kih = krl_ref[0], krh_ref[0], kil_ref[0], kih_ref[0]
    m_re, m_im, km_re, km_im = r[0:1], t[0:1], kil[0:1], kih[0:1]
    r, t = jnp.where(first, 0.0, r), jnp.where(first, 0.0, t)
    kil, kih = jnp.where(first, 0.0, kil), jnp.where(first, 0.0, kih)
    re_l, re_h, im_l, im_h = p + q, p - q, r + t, t - r
    yr_l, yi_l = re_l * krl - im_l * kil, re_l * kil + im_l * krl
    yr_h, yi_h = re_h * krh - im_h * kih, re_h * kih + im_h * krh
    ae_ref[0] = (yr_l + yr_h).astype(ae_ref.dtype)
    ao_ref[0] = (yr_l - yr_h).astype(ao_ref.dtype)
    be_ref[0] = jnp.where(first, m_re * km_re - m_im * km_im, yi_l - yi_h).astype(be_ref.dtype)
    bo_ref[0] = jnp.where(first, m_re * km_im + m_im * km_re, yi_l + yi_h).astype(bo_ref.dtype)


def _dfti_kernel(ae_ref, ao_ref, be_ref, bo_ref, g_ref, u_ref, gate_ref, skip_ref, o_ref, *y_s):
    skip = skip_ref[...]
    ye = _dot(g_ref[0], ae_ref[0]) + _dot(g_ref[2], be_ref[0])
    yo = _dot(g_ref[1], ao_ref[0]) + _dot(g_ref[3], bo_ref[0])
    oe = gate_ref[0, 0].astype(F32) * (ye + skip * u_ref[0, 0].astype(F32))
    oo = gate_ref[0, 1].astype(F32) * (yo + skip * u_ref[0, 1].astype(F32))
    if y_s:
        tm = oe.shape[0]
        for g in range(y_s[0].shape[0]):
            ln = slice(g * LANES, (g + 1) * LANES)
            y_s[0][g, pl.ds(0, tm, stride=2), :] = oe[:, ln]
            y_s[0][g, pl.ds(1, tm, stride=2), :] = oo[:, ln]
            o_ref[0, :, ln] = y_s[0][g].astype(o_ref.dtype)
    else:
        o_ref[0, 0] = oe.astype(o_ref.dtype)
        o_ref[0, 1] = oo.astype(o_ref.dtype)


def _long_conv_gated(u_arr, u_col0, gate_arr, gate_col0, C, fwd, inv, spectra, order, skip, natural_out):
    B, _, lh, _ = u_arr.shape
    tm, tn = _tile(512, lh), _tile(512, C)
    assert u_col0 % tn == 0 and gate_col0 % tn == 0
    uc, gc = u_col0 // tn, gate_col0 // tn
    grid = (C // tn, lh // tm, B)
    tab = pl.BlockSpec((4, tm, lh), lambda n, m, b: (0, m, 0))
    kspec = pl.BlockSpec((1, tm, tn), lambda n, m, b: (order, m, n))
    ys = pl.pallas_call(
        _dftf_kernel, grid=grid,
        in_specs=[pl.BlockSpec((1, 1, lh, tn), lambda n, m, b: (b, 0, 0, uc + n)),
                  pl.BlockSpec((1, 1, lh, tn), lambda n, m, b: (b, 1, 0, uc + n)),
                  tab, kspec, kspec, kspec, kspec],
        out_specs=[pl.BlockSpec((1, tm, tn), lambda n, m, b: (b, m, n))] * 4,
        out_shape=[jax.ShapeDtypeStruct((B, lh, C), BF16)] * 4,
        compiler_params=_cparams(3), name="dft_fwd")(u_arr, u_arr, fwd, *spectra)
    yspec = pl.BlockSpec((1, lh, tn), lambda n, m, b: (b, 0, n))
    if natural_out:
        out_spec = pl.BlockSpec((1, 2 * tm, tn), lambda n, m, b: (b, m, n))
        out_shape = jax.ShapeDtypeStruct((B, 2 * lh, C), BF16)
        scratch = [pltpu.VMEM((tn // LANES, 2 * tm, LANES), F32)]
    else:
        out_spec = pl.BlockSpec((1, 2, tm, tn), lambda n, m, b: (b, 0, m, n))
        out_shape = jax.ShapeDtypeStruct((B, 2, lh, C), BF16)
        scratch = []
    return pl.pallas_call(
        _dfti_kernel, grid=grid,
        in_specs=[yspec, yspec, yspec, yspec, tab,
                  pl.BlockSpec((1, 2, tm, tn), lambda n, m, b: (b, 0, m, uc + n)),
                  pl.BlockSpec((1, 2, tm, tn), lambda n, m, b: (b, 0, m, gc + n)),
                  pl.BlockSpec((1, tn), lambda n, m, b: (0, n))],
        out_specs=out_spec, out_shape=out_shape, scratch_shapes=scratch,
        compiler_params=_cparams(3), name="dft_inv")(*ys, inv, u_arr, gate_arr, skip.reshape(1, C))


def _hy_filter_kernel(z_ref, w1_ref, b1_ref, w2_ref, b2_ref, w3_ref, b3_ref, fr_ref, wf_ref, wb_ref,
                      t_ref, ad_ref, hp_ref, hm_ref, hh_s, hl_s):
    @pl.when((pl.program_id(0) == 0) & (pl.program_id(1) == 0))
    def _():
        fr = fr_ref[...]
        h = jnp.sin(fr * (_dot3(z_ref[...], w1_ref[...]) + b1_ref[...]))
        h = jnp.sin(fr * (_dot3(h, w2_ref[...]) + b2_ref[...]))
        h = jnp.sin(fr * (_dot3(h, w3_ref[...]) + b3_ref[...]))
        hh_s[...], hl_s[...] = _split(h)

    def dot3h(w):
        wh, wl = _split(w)
        return _dot(hh_s[...], wh) + _dot(hl_s[...], wh) + _dot(hh_s[...], wl)

    decay = jnp.exp(-t_ref[...] * ad_ref[...])
    hf = dot3h(wf_ref[...]) * decay
    hb = dot3h(wb_ref[...]) * decay
    row = lax.broadcasted_iota(jnp.int32, hb.shape, 0)
    hb = jnp.where(row == 0, 0.0, hb)
    hp_ref[0] = hf + hb
    hm_ref[0] = hf - hb


def _hy_filters(L, w1, b1, w2, b2, w3, b3, freq, wout, width):
    emb, ffn = w1.shape
    n_orders = wout.shape[1] // (2 * width)
    t = jnp.linspace(0.0, 1.0, L, dtype=F32)[:, None]
    bands = (emb - 1) // 2
    w = 2.0 * math.pi * jnp.arange(L, dtype=F32) / L
    f = jnp.linspace(1e-4, bands - 1, bands, dtype=F32)
    fw = w[:, None] * f[None, :]
    z = jnp.concatenate([t, jnp.cos(fw), -jnp.sin(fw)], axis=-1)
    z = jnp.concatenate([z[0::2], z[1::2]], axis=0)
    t = jnp.concatenate([t[0::2], t[1::2]], axis=0)
    deltas = jnp.linspace(math.log(HY_TARGET) / HY_SLOW_DECAY, math.log(HY_TARGET) / HY_FAST_DECAY,
                          width, dtype=F32)
    pe, pf = LANES - emb, LANES - ffn
    z = jnp.pad(z, ((0, 0), (0, pe)))
    w1p = jnp.pad(w1, ((0, pe), (0, pf)))
    w2p = jnp.pad(w2, ((0, pf), (0, pf)))
    w3p = jnp.pad(w3, ((0, pf), (0, pf)))
    woutp = jnp.pad(wout, ((0, pf), (0, 0)))
    vec = lambda a: jnp.pad(a, (0, pf)).reshape(1, LANES)
    tc = _tile(512, width)
    nt = width // tc
    full = lambda shp: pl.BlockSpec(shp, lambda o, n: (0, 0))
    out = pl.BlockSpec((1, L, tc), lambda o, n: (o, 0, n))
    shp = jax.ShapeDtypeStruct((n_orders, L, width), F32)
    return pl.pallas_call(
        _hy_filter_kernel, grid=(n_orders, nt),
        in_specs=[full((L, LANES)), full((LANES, LANES)), full((1, LANES)), full((LANES, LANES)), full((1, LANES)),
                  full((LANES, LANES)), full((1, LANES)), full((1, LANES)),
                  pl.BlockSpec((LANES, tc), lambda o, n: (0, (2 * o) * nt + n)),
                  pl.BlockSpec((LANES, tc), lambda o, n: (0, (2 * o + 1) * nt + n)),
                  full((L, 1)), pl.BlockSpec((1, tc), lambda o, n: (0, n))],
        out_specs=[out, out], out_shape=[shp, shp],
        scratch_shapes=[pltpu.VMEM((L, LANES), BF16)] * 2,
        compiler_params=_cparams(2), name="hy_filter")(
            z, w1p, vec(b1), w2p, vec(b2), w3p, vec(b3), vec(freq), woutp, woutp, t, jnp.abs(deltas).reshape(1, width))


def _spectrum_kernel(th_ref, tl_ref, hp_ref, hm_ref, krl_ref, krh_ref, kil_ref, kih_ref):
    tm = th_ref.shape[1]

    def dot3(i, x, rows=slice(None)):
        xh, xl = x
        return _dot(th_ref[i, rows], xh) + _dot(tl_ref[i, rows], xh) + _dot(th_ref[i, rows], xl)

    pe, po = _split(hp_ref[0, 0]), _split(hp_ref[0, 1])
    me, mo = _split(hm_ref[0, 0]), _split(hm_ref[0, 1])
    p, q, r, t = dot3(0, pe), dot3(1, po), dot3(2, me), dot3(3, mo)
    mid_re = dot3(2, pe, slice(0, 8))[0:1]
    first = (lax.broadcasted_iota(jnp.int32, p.shape, 0) + pl.program_id(2) * tm) == 0
    krl_ref[0] = p + q
    krh_ref[0] = p - q
    kil_ref[0] = jnp.where(first, mid_re, r + t)
    kih_ref[0] = jnp.where(first, t[0:1], t - r)


def _filter_spectra(hp, hm, fwd_hi, fwd_lo):
    n_orders, _, lh, width = hp.shape
    tm, tn = _tile(512, lh), _tile(512, width)
    tab = pl.BlockSpec((4, tm, lh), lambda o, n, m: (0, m, 0))
    hs = pl.BlockSpec((1, 2, lh, tn), lambda o, n, m: (o, 0, 0, n))
    out = pl.BlockSpec((1, tm, tn), lambda o, n, m: (o, m, n))
    shp = jax.ShapeDtypeStruct((n_orders, lh, width), F32)
    return pl.pallas_call(
        _spectrum_kernel, grid=(n_orders, width // tn, lh // tm),
        in_specs=[tab, tab, hs, hs], out_specs=[out] * 4, out_shape=[shp] * 4,
        compiler_params=_cparams(3), name="hy_spectrum")(fwd_hi, fwd_lo, hp, hm)


def _hg_gate(z, lb):
    f = lb + (1.0 - lb) * jax.nn.sigmoid(z)
    return jnp.log(f), 1.0 - f


def _chunk_cumsum(g, rev):
    n = g.shape[0]
    pos = lax.broadcasted_iota(jnp.int32, g.shape, 0) % HG_CHUNK
    d = 1
    while d < HG_CHUNK:
        if rev:
            g = g + jnp.where(pos < HG_CHUNK - d, pltpu.roll(g, n - d, 0), 0.0)
        else:
            g = g + jnp.where(pos >= d, pltpu.roll(g, d, 0), 0.0)
        d *= 2
    return g


def _row_at(b, step, rev):
    r = HG_CHUNK - 1 - step if rev else step
    return b[r:r + 1, :]


def _hg_kv(k, v, b, rev):
    bl = _row_at(b, HG_CHUNK - 1, rev)
    return _dot_tn(v.astype(BF16), (k * jnp.exp(bl - b)).astype(BF16))


def _hg_state(k, v, b, st, rev):
    return st * jnp.exp(_row_at(b, HG_CHUNK - 1, rev)) + _hg_kv(k, v, b, rev)


def _hg_intra(q, k, v, b, rev):
    C, SB = HG_CHUNK, HG_SUB
    nsb = C // SB
    row = lax.broadcasted_iota(jnp.int32, q.shape, 0)
    p = (C - 1 - row) if rev else row
    mids = _row_at(b, SB // 2, rev)
    ends = _row_at(b, SB - 1, rev)
    for i in range(1, nsb):
        mids = jnp.where(p >= i * SB, _row_at(b, i * SB + SB // 2, rev), mids)
        ends = jnp.where(p >= i * SB, _row_at(b, i * SB + SB - 1, rev), ends)
    tt = lax.broadcasted_iota(jnp.int32, (C, C), 0)
    ss = lax.broadcasted_iota(jnp.int32, (C, C), 1)
    if rev:
        tt, ss = C - 1 - tt, C - 1 - ss
    ad = _dot_nt((q * jnp.exp(b - mids)).astype(BF16), (k * jnp.exp(mids - b)).astype(BF16))
    a = jnp.where((tt // SB == ss // SB) & (ss <= tt), ad, 0.0)
    ke = k * jnp.exp(ends - b)
    qs, ks = [], []
    for j in range(nsb - 1):
        ej = _row_at(b, j * SB + SB - 1, rev)
        qs.append(jnp.where(p >= (j + 1) * SB, q * jnp.exp(jnp.minimum(b - ej, 0.0)), 0.0).astype(BF16))
        ks.append(jnp.where((p >= j * SB) & (p < (j + 1) * SB), ke, 0.0).astype(BF16))
    a = a + _dot_nt(jnp.concatenate(qs, axis=1), jnp.concatenate(ks, axis=1))
    return _dot(a.astype(BF16), v.astype(BF16))


def _hgrn_kernel(q_ref, zf_ref, zb_ref, i_ref, g_ref, lbf_ref, lbb_ref, gn_ref, sf_ref, sb_ref, o_ref,
                 q_s, kf_s, kb_s, v_s, bf_s, cb_s, of_s, ob_s):
    C = HG_CHUNK
    nc = q_s.shape[0] // C
    nh = q_s.shape[1] // HG_HEAD
    q_s[...] = _silu(q_ref[0].astype(F32)) * (HG_HEAD ** -0.5)
    gf, kf = _hg_gate(zf_ref[0].astype(F32), lbf_ref[0])
    kf_s[...] = kf
    bf_s[...] = _chunk_cumsum(gf, False)
    gb, kb = _hg_gate(zb_ref[0].astype(F32), lbb_ref[0])
    kb_s[...] = kb
    cb_s[...] = _chunk_cumsum(gb, True)
    v_s[...] = i_ref[0].astype(F32)
    dirs = ((kf_s, bf_s, of_s, False), (kb_s, cb_s, ob_s, True))

    def body(c, carry):
        out = []
        for h in range(nh):
            ln = slice(h * HG_HEAD, (h + 1) * HG_HEAD)
            for d, (k_s, b_s, o_s, rev) in enumerate(dirs):
                cc = nc - 1 - c if rev else c
                rows = pl.ds(pl.multiple_of(cc * C, C), C)
                st = carry[2 * h + d]
                q, k, v, b = q_s[rows, ln], k_s[rows, ln], v_s[rows, ln], b_s[rows, ln]
                o_s[rows, ln] = (_hg_intra(q, k, v, b, rev)
                                 + _dot_nt((q * jnp.exp(b)).astype(BF16), st.astype(BF16)))
                out.append(_hg_state(k, v, b, st, rev))
        return tuple(out)

    init = []
    for h in range(nh):
        init += [sf_ref[0, h], sb_ref[0, h]]
    lax.fori_loop(0, nc, body, tuple(init))
    for h in range(nh):
        ln = slice(h * HG_HEAD, (h + 1) * HG_HEAD)
        o = of_s[:, ln] + ob_s[:, ln]
        o = o * lax.rsqrt(jnp.mean(o * o, axis=-1, keepdims=True) + EPS)
        o_ref[0, :, ln] = (o * gn_ref[0, :, ln] * _silu(g_ref[0, :, ln].astype(F32))).astype(o_ref.dtype)


def _hgrn(proj3, off_hg, width, lb_f, lb_b, gn, st_f, st_b):
    B, L, _ = proj3.shape
    H = width // HG_HEAD
    nh = _tile(HG_HEADS_PER_STEP, H)
    wb = nh * HG_HEAD
    assert L % HG_CHUNK == 0 and off_hg % wb == 0
    c0, hb = off_hg // wb, H // nh

    def col(i):
        return pl.BlockSpec((1, L, wb), lambda b, h: (b, 0, c0 + i * hb + h))

    par = pl.BlockSpec((1, 1, wb), lambda b, h: (0, 0, h))
    st = pl.BlockSpec((1, nh, HG_HEAD, HG_HEAD), lambda b, h: (b, h, 0, 0))
    return pl.pallas_call(
        _hgrn_kernel, grid=(B, hb),
        in_specs=[col(0), col(1), col(2), col(3), col(4), par, par, par, st, st],
        out_specs=pl.BlockSpec((1, L, wb), lambda b, h: (b, 0, h)),
        out_shape=jax.ShapeDtypeStruct((B, L, width), BF16),
        scratch_shapes=[pltpu.VMEM((L, wb), F32)] * 8,
        compiler_params=_cparams(2), name="hgrn")(
            proj3, proj3, proj3, proj3, proj3,
            lb_f.reshape(1, 1, width), lb_b.reshape(1, 1, width), gn.reshape(1, 1, width), st_f, st_b)


def _hgctx_kernel(zf_ref, zb_ref, i_ref, lbf_ref, lbb_ref, sf_ref, sb_ref):
    C = HG_CHUNK
    nc = zf_ref.shape[1] // C
    gf, kf = _hg_gate(zf_ref[0].astype(F32), lbf_ref[0])
    gb, kb = _hg_gate(zb_ref[0].astype(F32), lbb_ref[0])
    bf = _chunk_cumsum(gf, False)
    cb = _chunk_cumsum(gb, True)
    v = i_ref[0].astype(F32)
    stf = jnp.zeros((HG_HEAD, HG_HEAD), F32)
    stb = jnp.zeros((HG_HEAD, HG_HEAD), F32)
    for c in range(nc):
        sl = slice(c * C, (c + 1) * C)
        stf = _hg_state(kf[sl], v[sl], bf[sl], stf, False)
    for c in reversed(range(nc)):
        sl = slice(c * C, (c + 1) * C)
        stb = _hg_state(kb[sl], v[sl], cb[sl], stb, True)
    sf_ref[0, 0] = stf
    sb_ref[0, 0] = stb


def _hgrn_context_states(pc3, width, lb_f, lb_b):
    B, Lc, _ = pc3.shape
    H = width // HG_HEAD
    assert Lc % HG_CHUNK == 0

    def col(i):
        return pl.BlockSpec((1, Lc, HG_HEAD), lambda b, h: (b, 0, i * H + h))

    par = pl.BlockSpec((1, 1, HG_HEAD), lambda b, h: (h, 0, 0))
    st = pl.BlockSpec((1, 1, HG_HEAD, HG_HEAD), lambda b, h: (b, h, 0, 0))
    shp = jax.ShapeDtypeStruct((B, H, HG_HEAD, HG_HEAD), F32)
    return pl.pallas_call(
        _hgctx_kernel, grid=(B, H),
        in_specs=[col(0), col(1), col(2), par, par],
        out_specs=[st, st], out_shape=[shp, shp],
        compiler_params=_cparams(2), name="hgrn_ctx")(
            pc3, pc3, pc3, lb_f.reshape(H, 1, HG_HEAD), lb_b.reshape(H, 1, HG_HEAD))


def _norm_router_kernel(x_ref, g_ref, sh_ref, sc_ref, wr_ref, xm_ref, aff_ref, *, n_experts):
    xm = _rms_mod(x_ref[0], g_ref[...], sh_ref[0], sc_ref[0])
    xm_ref[0] = xm.astype(xm_ref.dtype)
    logits = _dot3(xm, wr_ref[...])
    lane = lax.broadcasted_iota(jnp.int32, logits.shape, 1)
    logits = jnp.where(lane < n_experts, logits, -1e30)
    e = jnp.exp(logits - jnp.max(logits, axis=-1, keepdims=True))
    aff_ref[0] = e / jnp.sum(e, axis=-1, keepdims=True)


def _norm_router(x, g, sh, sc, w_router):
    B, L, D = x.shape
    E = w_router.shape[1]
    tm = _tile(256, L)
    wr = jnp.pad(w_router, ((0, 0), (0, LANES - E)))
    row = pl.BlockSpec((1, tm, D), lambda b, m: (b, m, 0))
    par = pl.BlockSpec((1, 1, D), lambda b, m: (b, 0, 0))
    return pl.pallas_call(
        functools.partial(_norm_router_kernel, n_experts=E), grid=(B, L // tm),
        in_specs=[row, pl.BlockSpec((1, D), lambda b, m: (0, 0)), par, par,
                  pl.BlockSpec((D, LANES), lambda b, m: (0, 0))],
        out_specs=[row, pl.BlockSpec((1, tm, LANES), lambda b, m: (b, m, 0))],
        out_shape=[jax.ShapeDtypeStruct((B, L, D), BF16), jax.ShapeDtypeStruct((B, L, LANES), F32)],
        compiler_params=_cparams(2), name="norm_router")(
            x, g.reshape(1, D), sh.reshape(B, 1, D), sc.reshape(B, 1, D), wr)


def _route_kernel(a_ref, tri_ref, slot_ref, *, cap):
    bits = lax.bitcast_convert_type(a_ref[...], jnp.int32)

    def count(mask):
        return jnp.sum(jnp.where(mask, 1.0, 0.0), axis=-1, keepdims=True)

    def body(i, thr):
        cand = thr | jnp.left_shift(jnp.int32(1), 30 - i)
        return jnp.where(count(bits >= cand) >= cap, cand, thr)

    thr = lax.fori_loop(0, 31, body, jnp.zeros((bits.shape[0], 1), jnp.int32))
    above, tie = bits > thr, bits == thr
    tri = tri_ref[...]
    ties_before = _dot(jnp.where(tie, 1.0, 0.0).astype(BF16), tri)
    chosen = above | (tie & (ties_before < cap - count(above)))
    before = _dot(jnp.where(chosen, 1.0, 0.0).astype(BF16), tri)
    slot_ref[...] = jnp.where(chosen, before.astype(jnp.int32), -1)


def _route(aff_t, cap):
    B, E, L = aff_t.shape
    idx = jnp.arange(L, dtype=jnp.int32)
    tri = (idx[:, None] < idx[None, :]).astype(BF16)
    slot = pl.pallas_call(
        functools.partial(_route_kernel, cap=cap), grid=(1,),
        in_specs=[pl.BlockSpec((B * E, L), lambda i: (0, 0)), pl.BlockSpec((L, L), lambda i: (0, 0))],
        out_specs=pl.BlockSpec((B * E, L), lambda i: (0, 0)),
        out_shape=jax.ShapeDtypeStruct((B * E, L), jnp.int32),
        compiler_params=_cparams(1), name="route")(aff_t.reshape(B * E, L), tri)
    return slot.reshape(B, E, 1, L)


def _gather_kernel(rank_ref, xm_ref, o_ref):
    rk = rank_ref[0, 0]
    cap = o_ref.shape[2]
    slot = lax.broadcasted_iota(jnp.int32, (cap, rk.shape[1]), 0)
    sel = jnp.where(slot == rk, 1.0, 0.0).astype(BF16)
    o_ref[0, 0] = _dot(sel, xm_ref[0]).astype(o_ref.dtype)


def _gather(rank_row, xm, cap):
    B, E, _, L = rank_row.shape
    D = xm.shape[2]
    td = _tile(1024, D)
    return pl.pallas_call(
        _gather_kernel, grid=(B, D // td, E),
        in_specs=[pl.BlockSpec((1, 1, 1, L), lambda b, d, e: (b, e, 0, 0)),
                  pl.BlockSpec((1, L, td), lambda b, d, e: (b, 0, d))],
        out_specs=pl.BlockSpec((1, 1, cap, td), lambda b, d, e: (e, b, 0, d)),
        out_shape=jax.ShapeDtypeStruct((E, B, cap, D), BF16),
        compiler_params=_cparams(3), name="moe_gather")(rank_row, xm)


def _expert_up_kernel(x_ref, wg_ref, wu_ref, h_ref):
    x = x_ref[0]
    a = _dot(x, wg_ref[0].astype(BF16))
    u = _dot(x, wu_ref[0].astype(BF16))
    h_ref[0] = (_silu(a) * u).astype(h_ref.dtype)


def _expert_down_kernel(h_ref, wd_ref, y_ref):
    y_ref[0] = _dot(h_ref[0], wd_ref[0].astype(BF16)).astype(y_ref.dtype)


def _experts(xg, w_gate, w_up, w_down):
    E, M, D = xg.shape
    FF = w_gate.shape[2]
    tf, td = _tile(256, FF), _tile(512, D)
    h = pl.pallas_call(
        _expert_up_kernel, grid=(E, FF // tf),
        in_specs=[pl.BlockSpec((1, M, D), lambda e, n: (e, 0, 0)),
                  pl.BlockSpec((1, D, tf), lambda e, n: (e, 0, n)),
                  pl.BlockSpec((1, D, tf), lambda e, n: (e, 0, n))],
        out_specs=pl.BlockSpec((1, M, tf), lambda e, n: (e, 0, n)),
        out_shape=jax.ShapeDtypeStruct((E, M, FF), BF16),
        compiler_params=_cparams(2), name="expert_up")(xg, w_gate, w_up)
    return pl.pallas_call(
        _expert_down_kernel, grid=(E, D // td),
        in_specs=[pl.BlockSpec((1, M, FF), lambda e, n: (e, 0, 0)),
                  pl.BlockSpec((1, FF, td), lambda e, n: (e, 0, n))],
        out_specs=pl.BlockSpec((1, M, td), lambda e, n: (e, 0, n)),
        out_shape=jax.ShapeDtypeStruct((E, M, D), BF16),
        compiler_params=_cparams(2), name="expert_down")(h, w_down)


def _combine_kernel(rk_ref, af_ref, y_ref, x_ref, g_ref, o_ref, w_s):
    E, _, cap, td = y_ref.shape

    @pl.when(pl.program_id(2) == 0)
    def _():
        slot = lax.broadcasted_iota(jnp.int32, (w_s.shape[0], cap), 1)
        for e in range(E):
            w_s[:, e * cap:(e + 1) * cap] = jnp.where(rk_ref[0, e] == slot, af_ref[0, e], 0.0).astype(w_s.dtype)

    y = y_ref[:, 0].reshape(E * cap, td)
    o_ref[0] = x_ref[0] + g_ref[0] * _dot(w_s[...], y)


def _combine(rank_col, aff_col, y, x, gate):
    B, E, L, _ = rank_col.shape
    cap, D = y.shape[2], y.shape[3]
    tm, td = _tile(512, L), _tile(512, D)
    colspec = pl.BlockSpec((1, E, tm, 1), lambda b, m, d: (b, 0, m, 0))
    xspec = pl.BlockSpec((1, tm, td), lambda b, m, d: (b, m, d))
    return pl.pallas_call(
        _combine_kernel, grid=(B, L // tm, D // td),
        in_specs=[colspec, colspec,
                  pl.BlockSpec((E, 1, cap, td), lambda b, m, d: (0, b, 0, d)),
                  xspec, pl.BlockSpec((1, 1, td), lambda b, m, d: (b, 0, d))],
        out_specs=xspec, out_shape=jax.ShapeDtypeStruct((B, L, D), F32),
        scratch_shapes=[pltpu.VMEM((tm, E * cap), BF16)],
        compiler_params=_cparams(3), name="moe_combine")(rank_col, aff_col, y, x, gate.reshape(B, 1, D))


def kernel(x, c, ctx, c_ctx, w_ada, b_ada, norm1, norm2, w_in, hy_conv_w, hy_conv_b, hy_w1, hy_b1, hy_w2, hy_b2, hy_w3, hy_b3, hy_freq, hy_wout, hy_skip, hg_lb, hg_norm, w_proj_hy, w_proj_hg, w_out, w_router, w_gate, w_up, w_down, norm_f):
    assert w_ada.shape[0] == 1, "single-layer block"
    B, L, D = x.shape
    Lc = ctx.shape[1]
    hyw, hgw = hy_skip.shape[-1], hg_norm.shape[-1]
    H = hgw // HG_HEAD
    E = w_router.shape[-1]
    off_hg = 3 * hyw
    off_f = off_hg + hgw
    off_gate = off_hg + 5 * hgw
    in_cols = off_gate + 2 * D
    assert w_in.shape[-1] == in_cols and L % GRID_W == 0
    cap = EC_CAPACITY * L // E

    lb_all = jnp.cumsum(jax.nn.softmax(hg_lb.astype(F32), axis=0), axis=0)
    lb_f, lb_b = lb_all[0, 0], lb_all[0, 1]

    rows = -(-(B + 1) // 8) * 8
    cond = jnp.concatenate([c, c_ctx[None], jnp.zeros((rows - B - 1, D), F32)], axis=0)
    ada = _ada(cond, w_ada[0], b_ada[0])
    sh1, sc1, g1, sh2, sc2, g2 = [ada[:B, i * D:(i + 1) * D] for i in range(6)]
    csh1 = jnp.broadcast_to(ada[B:B + 1, 0:D], (B, D))
    csc1 = jnp.broadcast_to(ada[B:B + 1, D:2 * D], (B, D))

    xc_m = _normmod(ctx, norm1[0], csh1, csc1)
    pc = _mm(xc_m.reshape(B * Lc, D), w_in[0], off_f, 3 * hgw, 1024, 512, BF16, "ctx_proj")
    st_f, st_b = _hgrn_context_states(pc.reshape(B, Lc, 3 * hgw), hgw, lb_f, lb_b)

    x_m = _normmod(x, norm1[0], sh1, sc1)
    proj = _mm(x_m.reshape(B * L, D), w_in[0], 0, in_cols, 1024, 512, BF16, "in_proj")
    proj3 = proj.reshape(B, L, in_cols)

    fwd, fwd_lo, inv = _dft_tables(L)
    hp, hm = _hy_filters(L, hy_w1[0], hy_b1[0], hy_w2[0], hy_b2[0], hy_w3[0], hy_b3[0], hy_freq[0],
                         hy_wout[0], hyw)
    n_orders = hp.shape[0]
    spectra = _filter_spectra(hp.reshape(n_orders, 2, L // 2, hyw), hm.reshape(n_orders, 2, L // 2, hyw),
                              fwd, fwd_lo)
    hyp = _short_conv(proj3, hy_conv_w[0], hy_conv_b[0])
    z = _long_conv_gated(hyp, 0, hyp, hyw, hyw, fwd, inv, spectra, 0, hy_skip[0, 0], False)
    y_hy = _long_conv_gated(z, 0, hyp, 2 * hyw, hyw, fwd, inv, spectra, 1, hy_skip[0, 1], True)

    y_hg = _hgrn(proj3, off_hg, hgw, lb_f, lb_b, hg_norm[0], st_f, st_b)

    merged = _merge(y_hy.reshape(B * L, hyw), y_hg.reshape(B * L, hgw), w_proj_hy[0], w_proj_hg[0],
                    proj, off_gate)
    x1 = _mm_res(merged, w_out[0], x.reshape(B * L, D), g1, L, 1024, 512).reshape(B, L, D)

    xm2, aff = _norm_router(x1, norm2[0], sh2, sc2, w_router[0])
    aff_t = jnp.swapaxes(aff[:, :, :E], 1, 2)
    slot_row = _route(aff_t, cap)
    xg = _gather(slot_row, xm2, cap)
    y = _experts(xg.reshape(E, B * cap, D), w_gate[0], w_up[0], w_down[0]).reshape(E, B, cap, D)
    x2 = _combine(slot_row.reshape(B, E, L, 1), aff_t.reshape(B, E, L, 1), y, x1, g2)

    return _rmsnorm(x2.reshape(B * L, D), norm_f).reshape(B, L, D)
```

```python
import functools
import math

import jax
import jax.numpy as jnp
from jax import lax
from jax.experimental import pallas as pl
from jax.experimental.pallas import tpu as pltpu

F32 = jnp.float32
BF16 = jnp.bfloat16

EPS = 1e-6
GRID_W = 64
HY_FAST_DECAY = 0.3
HY_SLOW_DECAY = 1.5
HY_TARGET = 1e-2
HG_HEAD = 128
HG_CHUNK = 128
HG_SUB = 32
HG_HEADS_PER_STEP = 4
EC_CAPACITY = 2
LANES = 128
VMEM_LIMIT_BYTES = 56 * 1024 * 1024


def _cparams(n_axes):
    return pltpu.CompilerParams(dimension_semantics=("arbitrary",) * n_axes,
                                vmem_limit_bytes=VMEM_LIMIT_BYTES)


def _tile(pref, dim):
    t = min(pref, dim)
    while dim % t:
        t //= 2
    return t


def _dot(a, b):
    return jnp.dot(a, b, preferred_element_type=F32)


def _dot_nt(a, b):
    return lax.dot_general(a, b, (((1,), (1,)), ((), ())), preferred_element_type=F32)


def _dot_tn(a, b):
    return lax.dot_general(a, b, (((0,), (0,)), ((), ())), preferred_element_type=F32)


def _split(x):
    hi = x.astype(BF16)
    return hi, (x - hi.astype(F32)).astype(BF16)


def _dot3(a, b):
    ah, al = _split(a)
    bh, bl = _split(b)
    return _dot(ah, bh) + _dot(al, bh) + _dot(ah, bl)


def _silu(x):
    return x * jax.nn.sigmoid(x)


def _ada_kernel(c_ref, w_ref, b_ref, o_ref):
    o_ref[...] = _dot3(_silu(c_ref[...]), w_ref[...]) + b_ref[...]


def _ada(cond, w, b):
    R, D = cond.shape
    N = w.shape[1]
    tn = _tile(512, N)
    return pl.pallas_call(
        _ada_kernel, grid=(N // tn,),
        in_specs=[pl.BlockSpec((R, D), lambda n: (0, 0)),
                  pl.BlockSpec((D, tn), lambda n: (0, n)),
                  pl.BlockSpec((1, tn), lambda n: (0, n))],
        out_specs=pl.BlockSpec((R, tn), lambda n: (0, n)),
        out_shape=jax.ShapeDtypeStruct((R, N), F32),
        compiler_params=_cparams(1), name="ada")(cond, w, b.reshape(1, N))


def _rms_mod(x, g, sh, sc):
    y = x * lax.rsqrt(jnp.mean(x * x, axis=-1, keepdims=True) + EPS) * g
    return y * (1.0 + sc) + sh


def _normmod_kernel(x_ref, g_ref, sh_ref, sc_ref, o_ref):
    o_ref[0] = _rms_mod(x_ref[0], g_ref[...], sh_ref[0], sc_ref[0]).astype(o_ref.dtype)


def _normmod(x, g, sh, sc):
    B, L, D = x.shape
    tm = _tile(256, L)
    row = pl.BlockSpec((1, tm, D), lambda b, m: (b, m, 0))
    par = pl.BlockSpec((1, 1, D), lambda b, m: (b, 0, 0))
    return pl.pallas_call(
        _normmod_kernel, grid=(B, L // tm),
        in_specs=[row, pl.BlockSpec((1, D), lambda b, m: (0, 0)), par, par],
        out_specs=row, out_shape=jax.ShapeDtypeStruct((B, L, D), BF16),
        compiler_params=_cparams(2), name="normmod")(x, g.reshape(1, D), sh.reshape(B, 1, D), sc.reshape(B, 1, D))


def _rmsnorm_kernel(x_ref, g_ref, o_ref):
    x = x_ref[...]
    o_ref[...] = x * lax.rsqrt(jnp.mean(x * x, axis=-1, keepdims=True) + EPS) * g_ref[...]


def _rmsnorm(x, g):
    M, D = x.shape
    tm = _tile(256, M)
    row = pl.BlockSpec((tm, D), lambda m: (m, 0))
    return pl.pallas_call(
        _rmsnorm_kernel, grid=(M // tm,),
        in_specs=[row, pl.BlockSpec((1, D), lambda m: (0, 0))],
        out_specs=row, out_shape=jax.ShapeDtypeStruct((M, D), F32),
        compiler_params=_cparams(1), name="final_norm")(x, g.reshape(1, D))


def _mm_kernel(x_ref, w_ref, o_ref, wbf_ref):
    @pl.when(pl.program_id(1) == 0)
    def _():
        wbf_ref[...] = w_ref[...].astype(BF16)
    o_ref[...] = _dot(x_ref[...], wbf_ref[...]).astype(o_ref.dtype)


def _mm(x, w, col0, ncols, tm, tn, out_dtype, name):
    M, K = x.shape
    tm, tn = _tile(tm, M), _tile(tn, ncols)
    assert col0 % tn == 0
    c0 = col0 // tn
    return pl.pallas_call(
        _mm_kernel, grid=(ncols // tn, M // tm),
        in_specs=[pl.BlockSpec((tm, K), lambda n, m: (m, 0)),
                  pl.BlockSpec((K, tn), lambda n, m: (0, c0 + n))],
        out_specs=pl.BlockSpec((tm, tn), lambda n, m: (m, n)),
        out_shape=jax.ShapeDtypeStruct((M, ncols), out_dtype),
        scratch_shapes=[pltpu.VMEM((K, tn), BF16)],
        compiler_params=_cparams(2), name=name)(x, w)


def _mm_res_kernel(x_ref, w_ref, r_ref, g_ref, o_ref, wbf_ref):
    @pl.when(pl.program_id(1) == 0)
    def _():
        wbf_ref[...] = w_ref[...].astype(BF16)
    o_ref[...] = r_ref[...] + g_ref[0] * _dot(x_ref[...], wbf_ref[...])


def _mm_res(x, w, res, gate, rows_per_batch, tm, tn):
    M, K = x.shape
    N = w.shape[1]
    tm, tn = _tile(tm, rows_per_batch), _tile(tn, N)
    mpb = rows_per_batch // tm
    B = gate.shape[0]
    return pl.pallas_call(
        _mm_res_kernel, grid=(N // tn, M // tm),
        in_specs=[pl.BlockSpec((tm, K), lambda n, m: (m, 0)),
                  pl.BlockSpec((K, tn), lambda n, m: (0, n)),
                  pl.BlockSpec((tm, tn), lambda n, m: (m, n)),
                  pl.BlockSpec((1, 1, tn), lambda n, m: (m // mpb, 0, n))],
        out_specs=pl.BlockSpec((tm, tn), lambda n, m: (m, n)),
        out_shape=jax.ShapeDtypeStruct((M, N), F32),
        scratch_shapes=[pltpu.VMEM((K, tn), BF16)],
        compiler_params=_cparams(2), name="out_proj")(x, w, res, gate.reshape(B, 1, N))


def _merge_kernel(yhy_ref, yhg_ref, why_ref, whg_ref, ghy_ref, ghg_ref, o_ref):
    a = _dot(yhy_ref[...], why_ref[...].astype(BF16))
    b = _dot(yhg_ref[...], whg_ref[...].astype(BF16))
    o_ref[...] = (jax.nn.sigmoid(ghy_ref[...].astype(F32)) * a
                  + jax.nn.sigmoid(ghg_ref[...].astype(F32)) * b).astype(o_ref.dtype)


def _merge(yhy, yhg, why, whg, proj, off_gate):
    M, K1 = yhy.shape
    K2 = yhg.shape[1]
    D = why.shape[1]
    tm, tn = _tile(1024, M), _tile(512, D)
    assert off_gate % tn == 0
    g0, g1 = off_gate // tn, (off_gate + D) // tn
    return pl.pallas_call(
        _merge_kernel, grid=(D // tn, M // tm),
        in_specs=[pl.BlockSpec((tm, K1), lambda n, m: (m, 0)),
                  pl.BlockSpec((tm, K2), lambda n, m: (m, 0)),
                  pl.BlockSpec((K1, tn), lambda n, m: (0, n)),
                  pl.BlockSpec((K2, tn), lambda n, m: (0, n)),
                  pl.BlockSpec((tm, tn), lambda n, m: (m, g0 + n)),
                  pl.BlockSpec((tm, tn), lambda n, m: (m, g1 + n))],
        out_specs=pl.BlockSpec((tm, tn), lambda n, m: (m, n)),
        out_shape=jax.ShapeDtypeStruct((M, D), BF16),
        compiler_params=_cparams(2), name="merge")(yhy, yhg, why, whg, proj, proj)


def _sconv_kernel(p_ref, w_ref, b_ref, o_ref, y_s):
    x = p_ref[0].astype(F32)
    tl = x.shape[0]
    pos = lax.broadcasted_iota(jnp.int32, x.shape, 0) % GRID_W
    prev = jnp.where(pos != 0, pltpu.roll(x, 1, 0), 0.0)
    nxt = jnp.where(pos != GRID_W - 1, pltpu.roll(x, tl - 1, 0), 0.0)
    w = w_ref[...]
    y = prev * w[0:1] + x * w[1:2] + nxt * w[2:3] + b_ref[...]
    for g in range(y_s.shape[0]):
        ln = slice(g * LANES, (g + 1) * LANES)
        y_s[g] = y[:, ln]
        o_ref[0, 0, :, ln] = y_s[g, pl.ds(0, tl // 2, stride=2), :].astype(o_ref.dtype)
        o_ref[0, 1, :, ln] = y_s[g, pl.ds(1, tl // 2, stride=2), :].astype(o_ref.dtype)


def _short_conv(proj3, w, b):
    B, L, _ = proj3.shape
    C = w.shape[1]
    tl, tc = _tile(512, L), _tile(1024, C)
    assert tl % GRID_W == 0
    return pl.pallas_call(
        _sconv_kernel, grid=(B, L // tl, C // tc),
        in_specs=[pl.BlockSpec((1, tl, tc), lambda b_, l, c: (b_, l, c)),
                  pl.BlockSpec((3, tc), lambda b_, l, c: (0, c)),
                  pl.BlockSpec((1, tc), lambda b_, l, c: (0, c))],
        out_specs=pl.BlockSpec((1, 2, tl // 2, tc), lambda b_, l, c: (b_, 0, l, c)),
        out_shape=jax.ShapeDtypeStruct((B, 2, L // 2, C), BF16),
        scratch_shapes=[pltpu.VMEM((tc // LANES, tl, LANES), F32)],
        compiler_params=_cparams(3), name="short_conv")(proj3, w, b.reshape(1, C))


def _dft_tables(L):
    lh, n = L // 2, 2 * L
    k = jnp.arange(lh, dtype=jnp.int32)
    alt = jnp.where(k % 2 == 0, 1.0, -1.0).astype(F32)
    ang_e = ((k[:, None] * (2 * k)[None, :]) % n).astype(F32) * (2.0 * math.pi / n)
    ang_o = ((k[:, None] * (2 * k + 1)[None, :]) % n).astype(F32) * (2.0 * math.pi / n)
    ce, co, se, so = jnp.cos(ang_e), jnp.cos(ang_o), -jnp.sin(ang_e), -jnp.sin(ang_o)
    fwd = jnp.stack([ce, co, se.at[0].set(alt), so.at[0].set(-alt)])
    inv = (2.0 / n) * jnp.stack([ce.T.at[:, 0].set(0.5), co.T.at[:, 0].set(0.5),
                                 se.T.at[:, 0].set(alt), so.T.at[:, 0].set(-alt)])
    fwd_hi, fwd_lo = _split(fwd)
    return fwd_hi, fwd_lo, inv.astype(BF16)


def _dftf_kernel(ue_ref, uo_ref, t_ref, krl_ref, krh_ref, kil_ref, kih_ref, ae_ref, ao_ref, be_ref, bo_ref):
    tm = t_ref.shape[1]
    ue, uo = ue_ref[0, 0], uo_ref[0, 0]
    p, q = _dot(t_ref[0], ue), _dot(t_ref[1], uo)
    r, t = _dot(t_ref[2], ue), _dot(t_ref[3], uo)
    first = (lax.broadcasted_iota(jnp.int32, p.shape, 0) + pl.program_id(1) * tm) == 0
    krl, krh, kil, kih = krl_ref[0], krh_ref[0], kil_ref[0], kih_ref[0]
    m_re, m_im, km_re, km_im = r[0:1], t[0:1], kil[0:1], kih[0:1]
    r, t = jnp.where(first, 0.0, r), jnp.where(first, 0.0, t)
    kil, kih = jnp.where(first, 0.0, kil), jnp.where(first, 0.0, kih)
    re_l, re_h, im_l, im_h = p + q, p - q, r + t, t - r
    yr_l, yi_l = re_l * krl - im_l * kil, re_l * kil + im_l * krl
    yr_h, yi_h = re_h * krh - im_h * kih, re_h * kih + im_h * krh
    ae_ref[0] = (yr_l + yr_h).astype(ae_ref.dtype)
    ao_ref[0] = (yr_l - yr_h).astype(ao_ref.dtype)
    be_ref[0] = jnp.where(first, m_re * km_re - m_im * km_im, yi_l - yi_h).astype(be_ref.dtype)
    bo_ref[0] = jnp.where(first, m_re * km_im + m_im * km_re, yi_l + yi_h).astype(bo_ref.dtype)


def _dfti_kernel(ae_ref, ao_ref, be_ref, bo_ref, g_ref, u_ref, gate_ref, skip_ref, o_ref, *y_s):
    skip = skip_ref[...]
    ye = _dot(g_ref[0], ae_ref[0]) + _dot(g_ref[2], be_ref[0])
    yo = _dot(g_ref[1], ao_ref[0]) + _dot(g_ref[3], bo_ref[0])
    oe = gate_ref[0, 0].astype(F32) * (ye + skip * u_ref[0, 0].astype(F32))
    oo = gate_ref[0, 1].astype(F32) * (yo + skip * u_ref[0, 1].astype(F32))
    if y_s:
        tm = oe.shape[0]
        for g in range(y_s[0].shape[0]):
            ln = slice(g * LANES, (g + 1) * LANES)
            y_s[0][g, pl.ds(0, tm, stride=2), :] = oe[:, ln]
            y_s[0][g, pl.ds(1, tm, stride=2), :] = oo[:, ln]
            o_ref[0, :, ln] = y_s[0][g].astype(o_ref.dtype)
    else:
        o_ref[0, 0] = oe.astype(o_ref.dtype)
        o_ref[0, 1] = oo.astype(o_ref.dtype)


def _long_conv_gated(u_arr, u_col0, gate_arr, gate_col0, C, fwd, inv, spectra, order, skip, natural_out):
    B, _, lh, _ = u_arr.shape
    tm, tn = _tile(512, lh), _tile(512, C)
    assert u_col0 % tn == 0 and gate_col0 % tn == 0
    uc, gc = u_col0 // tn, gate_col0 // tn
    grid = (C // tn, lh // tm, B)
    tab = pl.BlockSpec((4, tm, lh), lambda n, m, b: (0, m, 0))
    kspec = pl.BlockSpec((1, tm, tn), lambda n, m, b: (order, m, n))
    ys = pl.pallas_call(
        _dftf_kernel, grid=grid,
        in_specs=[pl.BlockSpec((1, 1, lh, tn), lambda n, m, b: (b, 0, 0, uc + n)),
                  pl.BlockSpec((1, 1, lh, tn), lambda n, m, b: (b, 1, 0, uc + n)),
                  tab, kspec, kspec, kspec, kspec],
        out_specs=[pl.BlockSpec((1, tm, tn), lambda n, m, b: (b, m, n))] * 4,
        out_shape=[jax.ShapeDtypeStruct((B, lh, C), BF16)] * 4,
        compiler_params=_cparams(3), name="dft_fwd")(u_arr, u_arr, fwd, *spectra)
    yspec = pl.BlockSpec((1, lh, tn), lambda n, m, b: (b, 0, n))
    if natural_out:
        out_spec = pl.BlockSpec((1, 2 * tm, tn), lambda n, m, b: (b, m, n))
        out_shape = jax.ShapeDtypeStruct((B, 2 * lh, C), BF16)
        scratch = [pltpu.VMEM((tn // LANES, 2 * tm, LANES), F32)]
    else:
        out_spec = pl.BlockSpec((1, 2, tm, tn), lambda n, m, b: (b, 0, m, n))
        out_shape = jax.ShapeDtypeStruct((B, 2, lh, C), BF16)
        scratch = []
    return pl.pallas_call(
        _dfti_kernel, grid=grid,
        in_specs=[yspec, yspec, yspec, yspec, tab,
                  pl.BlockSpec((1, 2, tm, tn), lambda n, m, b: (b, 0, m, uc + n)),
                  pl.BlockSpec((1, 2, tm, tn), lambda n, m, b: (b, 0, m, gc + n)),
                  pl.BlockSpec((1, tn), lambda n, m, b: (0, n))],
        out_specs=out_spec, out_shape=out_shape, scratch_shapes=scratch,
        compiler_params=_cparams(3), name="dft_inv")(*ys, inv, u_arr, gate_arr, skip.reshape(1, C))


def _hy_filter_kernel(z_ref, w1_ref, b1_ref, w2_ref, b2_ref, w3_ref, b3_ref, fr_ref, wf_ref, wb_ref,
                      t_ref, ad_ref, hp_ref, hm_ref, hh_s, hl_s):
    @pl.when((pl.program_id(0) == 0) & (pl.program_id(1) == 0))
    def _():
        fr = fr_ref[...]
        h = jnp.sin(fr * (_dot3(z_ref[...], w1_ref[...]) + b1_ref[...]))
        h = jnp.sin(fr * (_dot3(h, w2_ref[...]) + b2_ref[...]))
        h = jnp.sin(fr * (_dot3(h, w3_ref[...]) + b3_ref[...]))
        hh_s[...], hl_s[...] = _split(h)

    def dot3h(w):
        wh, wl = _split(w)
        return _dot(hh_s[...], wh) + _dot(hl_s[...], wh) + _dot(hh_s[...], wl)

    decay = jnp.exp(-t_ref[...] * ad_ref[...])
    hf = dot3h(wf_ref[...]) * decay
    hb = dot3h(wb_ref[...]) * decay
    row = lax.broadcasted_iota(jnp.int32, hb.shape, 0)
    hb = jnp.where(row == 0, 0.0, hb)
    hp_ref[0] = hf + hb
    hm_ref[0] = hf - hb


def _hy_filters(L, w1, b1, w2, b2, w3, b3, freq, wout, width):
    emb, ffn = w1.shape
    n_orders = wout.shape[1] // (2 * width)
    t = jnp.linspace(0.0, 1.0, L, dtype=F32)[:, None]
    bands = (emb - 1) // 2
    w = 2.0 * math.pi * jnp.arange(L, dtype=F32) / L
    f = jnp.linspace(1e-4, bands - 1, bands, dtype=F32)
    fw = w[:, None] * f[None, :]
    z = jnp.concatenate([t, jnp.cos(fw), -jnp.sin(fw)], axis=-1)
    z = jnp.concatenate([z[0::2], z[1::2]], axis=0)
    t = jnp.concatenate([t[0::2], t[1::2]], axis=0)
    deltas = jnp.linspace(math.log(HY_TARGET) / HY_SLOW_DECAY, math.log(HY_TARGET) / HY_FAST_DECAY,
                          width, dtype=F32)
    pe, pf = LANES - emb, LANES - ffn
    z = jnp.pad(z, ((0, 0), (0, pe)))
    w1p = jnp.pad(w1, ((0, pe), (0, pf)))
    w2p = jnp.pad(w2, ((0, pf), (0, pf)))
    w3p = jnp.pad(w3, ((0, pf), (0, pf)))
    woutp = jnp.pad(wout, ((0, pf), (0, 0)))
    vec = lambda a: jnp.pad(a, (0, pf)).reshape(1, LANES)
    tc = _tile(512, width)
    nt = width // tc
    full = lambda shp: pl.BlockSpec(shp, lambda o, n: (0, 0))
    out = pl.BlockSpec((1, L, tc), lambda o, n: (o, 0, n))
    shp = jax.ShapeDtypeStruct((n_orders, L, width), F32)
    return pl.pallas_call(
        _hy_filter_kernel, grid=(n_orders, nt),
        in_specs=[full((L, LANES)), full((LANES, LANES)), full((1, LANES)), full((LANES, LANES)), full((1, LANES)),
                  full((LANES, LANES)), full((1, LANES)), full((1, LANES)),
                  pl.BlockSpec((LANES, tc), lambda o, n: (0, (2 * o) * nt + n)),
                  pl.BlockSpec((LANES, tc), lambda o, n: (0, (2 * o + 1) * nt + n)),
                  full((L, 1)), pl.BlockSpec((1, tc), lambda o, n: (0, n))],
        out_specs=[out, out], out_shape=[shp, shp],
        scratch_shapes=[pltpu.VMEM((L, LANES), BF16)] * 2,
        compiler_params=_cparams(2), name="hy_filter")(
            z, w1p, vec(b1), w2p, vec(b2), w3p, vec(b3), vec(freq), woutp, woutp, t, jnp.abs(deltas).reshape(1, width))


def _spectrum_kernel(th_ref, tl_ref, hp_ref, hm_ref, krl_ref, krh_ref, kil_ref, kih_ref):
    tm = th_ref.shape[1]

    def dot3(i, x, rows=slice(None)):
        xh, xl = x
        return _dot(th_ref[i, rows], xh) + _dot(tl_ref[i, rows], xh) + _dot(th_ref[i, rows], xl)

    pe, po = _split(hp_ref[0, 0]), _split(hp_ref[0, 1])
    me, mo = _split(hm_ref[0, 0]), _split(hm_ref[0, 1])
    p, q, r, t = dot3(0, pe), dot3(1, po), dot3(2, me), dot3(3, mo)
    mid_re = dot3(2, pe, slice(0, 8))[0:1]
    first = (lax.broadcasted_iota(jnp.int32, p.shape, 0) + pl.program_id(2) * tm) == 0
    krl_ref[0] = p + q
    krh_ref[0] = p - q
    kil_ref[0] = jnp.where(first, mid_re, r + t)
    kih_ref[0] = jnp.where(first, t[0:1], t - r)


def _filter_spectra(hp, hm, fwd_hi, fwd_lo):
    n_orders, _, lh, width = hp.shape
    tm, tn = _tile(512, lh), _tile(512, width)
    tab = pl.BlockSpec((4, tm, lh), lambda o, n, m: (0, m, 0))
    hs = pl.BlockSpec((1, 2, lh, tn), lambda o, n, m: (o, 0, 0, n))
    out = pl.BlockSpec((1, tm, tn), lambda o, n, m: (o, m, n))
    shp = jax.ShapeDtypeStruct((n_orders, lh, width), F32)
    return pl.pallas_call(
        _spectrum_kernel, grid=(n_orders, width // tn, lh // tm),
        in_specs=[tab, tab, hs, hs], out_specs=[out] * 4, out_shape=[shp] * 4,
        compiler_params=_cparams(3), name="hy_spectrum")(fwd_hi, fwd_lo, hp, hm)


def _hg_gate(z, lb):
    f = lb + (1.0 - lb) * jax.nn.sigmoid(z)
    return jnp.log(f), 1.0 - f


def _chunk_cumsum(g, rev):
    n = g.shape[0]
    pos = lax.broadcasted_iota(jnp.int32, g.shape, 0) % HG_CHUNK
    d = 1
    while d < HG_CHUNK:
        if rev:
            g = g + jnp.where(pos < HG_CHUNK - d, pltpu.roll(g, n - d, 0), 0.0)
        else:
            g = g + jnp.where(pos >= d, pltpu.roll(g, d, 0), 0.0)
        d *= 2
    return g


def _row_at(b, step, rev):
    r = HG_CHUNK - 1 - step if rev else step
    return b[r:r + 1, :]


def _hg_kv(k, v, b, rev):
    bl = _row_at(b, HG_CHUNK - 1, rev)
    return _dot_tn(v.astype(BF16), (k * jnp.exp(bl - b)).astype(BF16))


def _hg_state(k, v, b, st, rev):
    return st * jnp.exp(_row_at(b, HG_CHUNK - 1, rev)) + _hg_kv(k, v, b, rev)


def _hg_intra(q, k, v, b, rev):
    C, SB = HG_CHUNK, HG_SUB
    nsb = C // SB
    row = lax.broadcasted_iota(jnp.int32, q.shape, 0)
    p = (C - 1 - row) if rev else row
    mids = _row_at(b, SB // 2, rev)
    ends = _row_at(b, SB - 1, rev)
    for i in range(1, nsb):
        mids = jnp.where(p >= i * SB, _row_at(b, i * SB + SB // 2, rev), mids)
        ends = jnp.where(p >= i * SB, _row_at(b, i * SB + SB - 1, rev), ends)
    tt = lax.broadcasted_iota(jnp.int32, (C, C), 0)
    ss = lax.broadcasted_iota(jnp.int32, (C, C), 1)
    if rev:
        tt, ss = C - 1 - tt, C - 1 - ss
    ad = _dot_nt((q * jnp.exp(b - mids)).astype(BF16), (k * jnp.exp(mids - b)).astype(BF16))
    a = jnp.where((tt // SB == ss // SB) & (ss <= tt), ad, 0.0)
    ke = k * jnp.exp(ends - b)
    qs, ks = [], []
    for j in range(nsb - 1):
        ej = _row_at(b, j * SB + SB - 1, rev)
        qs.append(jnp.where(p >= (j + 1) * SB, q * jnp.exp(jnp.minimum(b - ej, 0.0)), 0.0).astype(BF16))
        ks.append(jnp.where((p >= j * SB) & (p < (j + 1) * SB), ke, 0.0).astype(BF16))
    a = a + _dot_nt(jnp.concatenate(qs, axis=1), jnp.concatenate(ks, axis=1))
    return _dot(a.astype(BF16), v.astype(BF16))


def _hgrn_kernel(q_ref, zf_ref, zb_ref, i_ref, g_ref, lbf_ref, lbb_ref, gn_ref, sf_ref, sb_ref, o_ref,
                 kf_s, kb_s, bf_s, cb_s, of_s, ob_s):
    C = HG_CHUNK
    nc = kf_s.shape[0] // C
    nh = kf_s.shape[1] // HG_HEAD
    gf, kf = _hg_gate(zf_ref[0].astype(F32), lbf_ref[0])
    kf_s[...] = kf
    bf_s[...] = _chunk_cumsum(gf, False)
    gb, kb = _hg_gate(zb_ref[0].astype(F32), lbb_ref[0])
    kb_s[...] = kb
    cb_s[...] = _chunk_cumsum(gb, True)
    dirs = ((kf_s, bf_s, of_s, False), (kb_s, cb_s, ob_s, True))

    def body(c, carry):
        out = []
        for h in range(nh):
            ln = slice(h * HG_HEAD, (h + 1) * HG_HEAD)
            for d, (k_s, b_s, o_s, rev) in enumerate(dirs):
                cc = nc - 1 - c if rev else c
                rows = pl.ds(pl.multiple_of(cc * C, C), C)
                st = carry[2 * h + d]
                q = _silu(q_ref[0, rows, ln].astype(F32)) * (HG_HEAD ** -0.5)
                v = i_ref[0, rows, ln].astype(F32)
                k, b = k_s[rows, ln], b_s[rows, ln]
                o_s[rows, ln] = (_hg_intra(q, k, v, b, rev)
                                 + _dot_nt((q * jnp.exp(b)).astype(BF16), st.astype(BF16)))
                out.append(_hg_state(k, v, b, st, rev))
        return tuple(out)

    init = []
    for h in range(nh):
        init += [sf_ref[0, h], sb_ref[0, h]]
    lax.fori_loop(0, nc, body, tuple(init))
    for h in range(nh):
        ln = slice(h * HG_HEAD, (h + 1) * HG_HEAD)
        o = of_s[:, ln] + ob_s[:, ln]
        o = o * lax.rsqrt(jnp.mean(o * o, axis=-1, keepdims=True) + EPS)
        o_ref[0, :, ln] = (o * gn_ref[0, :, ln] * _silu(g_ref[0, :, ln].astype(F32))).astype(o_ref.dtype)


def _hgrn(proj3, off_hg, width, lb_f, lb_b, gn, st_f, st_b):
    B, L, _ = proj3.shape
    H = width // HG_HEAD
    nh = _tile(HG_HEADS_PER_STEP, H)
    wb = nh * HG_HEAD
    assert L % HG_CHUNK == 0 and off_hg % wb == 0
    c0, hb = off_hg // wb, H // nh

    def col(i):
        return pl.BlockSpec((1, L, wb), lambda b, h: (b, 0, c0 + i * hb + h))

    par = pl.BlockSpec((1, 1, wb), lambda b, h: (0, 0, h))
    st = pl.BlockSpec((1, nh, HG_HEAD, HG_HEAD), lambda b, h: (b, h, 0, 0))
    return pl.pallas_call(
        _hgrn_kernel, grid=(B, hb),
        in_specs=[col(0), col(1), col(2), col(3), col(4), par, par, par, st, st],
        out_specs=pl.BlockSpec((1, L, wb), lambda b, h: (b, 0, h)),
        out_shape=jax.ShapeDtypeStruct((B, L, width), BF16),
        scratch_shapes=[pltpu.VMEM((L, wb), F32)] * 6,
        compiler_params=_cparams(2), name="hgrn")(
            proj3, proj3, proj3, proj3, proj3,
            lb_f.reshape(1, 1, width), lb_b.reshape(1, 1, width), gn.reshape(1, 1, width), st_f, st_b)


def _hgctx_kernel(zf_ref, zb_ref, i_ref, lbf_ref, lbb_ref, sf_ref, sb_ref):
    C = HG_CHUNK
    nc = zf_ref.shape[1] // C
    gf, kf = _hg_gate(zf_ref[0].astype(F32), lbf_ref[0])
    gb, kb = _hg_gate(zb_ref[0].astype(F32), lbb_ref[0])
    bf = _chunk_cumsum(gf, False)
    cb = _chunk_cumsum(gb, True)
    v = i_ref[0].astype(F32)
    stf = jnp.zeros((HG_HEAD, HG_HEAD), F32)
    stb = jnp.zeros((HG_HEAD, HG_HEAD), F32)
    for c in range(nc):
        sl = slice(c * C, (c + 1) * C)
        stf = _hg_state(kf[sl], v[sl], bf[sl], stf, False)
    for c in reversed(range(nc)):
        sl = slice(c * C, (c + 1) * C)
        stb = _hg_state(kb[sl], v[sl], cb[sl], stb, True)
    sf_ref[0, 0] = stf
    sb_ref[0, 0] = stb


def _hgrn_context_states(pc3, width, lb_f, lb_b):
    B, Lc, _ = pc3.shape
    H = width // HG_HEAD
    assert Lc % HG_CHUNK == 0

    def col(i):
        return pl.BlockSpec((1, Lc, HG_HEAD), lambda b, h: (b, 0, i * H + h))

    par = pl.BlockSpec((1, 1, HG_HEAD), lambda b, h: (h, 0, 0))
    st = pl.BlockSpec((1, 1, HG_HEAD, HG_HEAD), lambda b, h: (b, h, 0, 0))
    shp = jax.ShapeDtypeStruct((B, H, HG_HEAD, HG_HEAD), F32)
    return pl.pallas_call(
        _hgctx_kernel, grid=(B, H),
        in_specs=[col(0), col(1), col(2), par, par],
        out_specs=[st, st], out_shape=[shp, shp],
        compiler_params=_cparams(2), name="hgrn_ctx")(
            pc3, pc3, pc3, lb_f.reshape(H, 1, HG_HEAD), lb_b.reshape(H, 1, HG_HEAD))


def _norm_router_kernel(x_ref, g_ref, sh_ref, sc_ref, wr_ref, xm_ref, aff_ref, *, n_experts):
    xm = _rms_mod(x_ref[0], g_ref[...], sh_ref[0], sc_ref[0])
    xm_ref[0] = xm.astype(xm_ref.dtype)
    logits = _dot3(xm, wr_ref[...])
    lane = lax.broadcasted_iota(jnp.int32, logits.shape, 1)
    logits = jnp.where(lane < n_experts, logits, -1e30)
    e = jnp.exp(logits - jnp.max(logits, axis=-1, keepdims=True))
    aff_ref[0] = e / jnp.sum(e, axis=-1, keepdims=True)


def _norm_router(x, g, sh, sc, w_router):
    B, L, D = x.shape
    E = w_router.shape[1]
    tm = _tile(256, L)
    wr = jnp.pad(w_router, ((0, 0), (0, LANES - E)))
    row = pl.BlockSpec((1, tm, D), lambda b, m: (b, m, 0))
    par = pl.BlockSpec((1, 1, D), lambda b, m: (b, 0, 0))
    return pl.pallas_call(
        functools.partial(_norm_router_kernel, n_experts=E), grid=(B, L // tm),
        in_specs=[row, pl.BlockSpec((1, D), lambda b, m: (0, 0)), par, par,
                  pl.BlockSpec((D, LANES), lambda b, m: (0, 0))],
        out_specs=[row, pl.BlockSpec((1, tm, LANES), lambda b, m: (b, m, 0))],
        out_shape=[jax.ShapeDtypeStruct((B, L, D), BF16), jax.ShapeDtypeStruct((B, L, LANES), F32)],
        compiler_params=_cparams(2), name="norm_router")(
            x, g.reshape(1, D), sh.reshape(B, 1, D), sc.reshape(B, 1, D), wr)


def _route_kernel(a_ref, tri_ref, slot_ref, *, cap):
    bits = lax.bitcast_convert_type(a_ref[...], jnp.int32)

    def count(mask):
        return jnp.sum(jnp.where(mask, 1.0, 0.0), axis=-1, keepdims=True)

    def body(i, thr):
        cand = thr | jnp.left_shift(jnp.int32(1), 30 - i)
        return jnp.where(count(bits >= cand) >= cap, cand, thr)

    thr = lax.fori_loop(0, 31, body, jnp.zeros((bits.shape[0], 1), jnp.int32))
    above, tie = bits > thr, bits == thr
    tri = tri_ref[...]
    ties_before = _dot(jnp.where(tie, 1.0, 0.0).astype(BF16), tri)
    chosen = above | (tie & (ties_before < cap - count(above)))
    before = _dot(jnp.where(chosen, 1.0, 0.0).astype(BF16), tri)
    slot_ref[...] = jnp.where(chosen, before.astype(jnp.int32), -1)


def _route(aff_t, cap):
    B, E, L = aff_t.shape
    idx = jnp.arange(L, dtype=jnp.int32)
    tri = (idx[:, None] < idx[None, :]).astype(BF16)
    slot = pl.pallas_call(
        functools.partial(_route_kernel, cap=cap), grid=(1,),
        in_specs=[pl.BlockSpec((B * E, L), lambda i: (0, 0)), pl.BlockSpec((L, L), lambda i: (0, 0))],
        out_specs=pl.BlockSpec((B * E, L), lambda i: (0, 0)),
        out_shape=jax.ShapeDtypeStruct((B * E, L), jnp.int32),
        compiler_params=_cparams(1), name="route")(aff_t.reshape(B * E, L), tri)
    return slot.reshape(B, E, 1, L)


def _gather_kernel(slot_ref, xm_ref, o_ref):
    ne, _, cap, td = o_ref.shape
    L = slot_ref.shape[3]
    want = lax.broadcasted_iota(jnp.int32, (cap, L), 0)
    sel = jnp.concatenate([jnp.where(want == slot_ref[0, e], 1.0, 0.0).astype(BF16) for e in range(ne)], axis=0)
    o_ref[:, 0] = _dot(sel, xm_ref[0]).astype(o_ref.dtype).reshape(ne, cap, td)


def _gather(slot_row, xm, cap):
    B, E, _, L = slot_row.shape
    D = xm.shape[2]
    td, ne = _tile(1024, D), _tile(4, E)
    return pl.pallas_call(
        _gather_kernel, grid=(B, D // td, E // ne),
        in_specs=[pl.BlockSpec((1, ne, 1, L), lambda b, d, e: (b, e, 0, 0)),
                  pl.BlockSpec((1, L, td), lambda b, d, e: (b, 0, d))],
        out_specs=pl.BlockSpec((ne, 1, cap, td), lambda b, d, e: (e, b, 0, d)),
        out_shape=jax.ShapeDtypeStruct((E, B, cap, D), BF16),
        compiler_params=_cparams(3), name="moe_gather")(slot_row, xm)


def _expert_up_kernel(x_ref, wg_ref, wu_ref, h_ref):
    x = x_ref[0]
    a = _dot(x, wg_ref[0].astype(BF16))
    u = _dot(x, wu_ref[0].astype(BF16))
    h_ref[0] = (_silu(a) * u).astype(h_ref.dtype)


def _expert_down_kernel(h_ref, wd_ref, y_ref):
    y_ref[0] = _dot(h_ref[0], wd_ref[0].astype(BF16)).astype(y_ref.dtype)


def _experts(xg, w_gate, w_up, w_down):
    E, M, D = xg.shape
    FF = w_gate.shape[2]
    tf, td = _tile(256, FF), _tile(1024, D)
    h = pl.pallas_call(
        _expert_up_kernel, grid=(E, FF // tf),
        in_specs=[pl.BlockSpec((1, M, D), lambda e, n: (e, 0, 0)),
                  pl.BlockSpec((1, D, tf), lambda e, n: (e, 0, n)),
                  pl.BlockSpec((1, D, tf), lambda e, n: (e, 0, n))],
        out_specs=pl.BlockSpec((1, M, tf), lambda e, n: (e, 0, n)),
        out_shape=jax.ShapeDtypeStruct((E, M, FF), BF16),
        compiler_params=_cparams(2), name="expert_up")(xg, w_gate, w_up)
    return pl.pallas_call(
        _expert_down_kernel, grid=(E, D // td),
        in_specs=[pl.BlockSpec((1, M, FF), lambda e, n: (e, 0, 0)),
                  pl.BlockSpec((1, FF, td), lambda e, n: (e, 0, n))],
        out_specs=pl.BlockSpec((1, M, td), lambda e, n: (e, 0, n)),
        out_shape=jax.ShapeDtypeStruct((E, M, D), BF16),
        compiler_params=_cparams(2), name="expert_down")(h, w_down)


def _combine_kernel(sl_ref, af_ref, y_ref, x_ref, g_ref, o_ref, w_s):
    E, _, cap, td = y_ref.shape

    @pl.when(pl.program_id(2) == 0)
    def _():
        sl, af = sl_ref[0], af_ref[0]
        want = lax.broadcasted_iota(jnp.int32, (w_s.shape[0], cap), 1)
        for e in range(E):
            w_s[:, e * cap:(e + 1) * cap] = jnp.where(sl[:, e:e + 1] == want, af[:, e:e + 1], 0.0).astype(w_s.dtype)

    y = y_ref[:, 0].reshape(E * cap, td)
    o_ref[0] = x_ref[0] + g_ref[0] * _dot(w_s[...], y)


def _combine(slot_tok, aff, y, x, gate):
    B, L, _ = slot_tok.shape
    E, cap, D = y.shape[0], y.shape[2], y.shape[3]
    tm, td = _tile(512, L), _tile(1024, D)
    lanespec = pl.BlockSpec((1, tm, LANES), lambda b, m, d: (b, m, 0))
    xspec = pl.BlockSpec((1, tm, td), lambda b, m, d: (b, m, d))
    return pl.pallas_call(
        _combine_kernel, grid=(B, L // tm, D // td),
        in_specs=[lanespec, lanespec,
                  pl.BlockSpec((E, 1, cap, td), lambda b, m, d: (0, b, 0, d)),
                  xspec, pl.BlockSpec((1, 1, td), lambda b, m, d: (b, 0, d))],
        out_specs=xspec, out_shape=jax.ShapeDtypeStruct((B, L, D), F32),
        scratch_shapes=[pltpu.VMEM((tm, E * cap), BF16)],
        compiler_params=_cparams(3), name="moe_combine")(slot_tok, aff, y, x, gate.reshape(B, 1, D))


def kernel(x, c, ctx, c_ctx, w_ada, b_ada, norm1, norm2, w_in, hy_conv_w, hy_conv_b, hy_w1, hy_b1, hy_w2, hy_b2, hy_w3, hy_b3, hy_freq, hy_wout, hy_skip, hg_lb, hg_norm, w_proj_hy, w_proj_hg, w_out, w_router, w_gate, w_up, w_down, norm_f):
    assert w_ada.shape[0] == 1, "single-layer block"
    B, L, D = x.shape
    Lc = ctx.shape[1]
    hyw, hgw = hy_skip.shape[-1], hg_norm.shape[-1]
    H = hgw // HG_HEAD
    E = w_router.shape[-1]
    off_hg = 3 * hyw
    off_f = off_hg + hgw
    off_gate = off_hg + 5 * hgw
    in_cols = off_gate + 2 * D
    assert w_in.shape[-1] == in_cols and L % GRID_W == 0
    cap = EC_CAPACITY * L // E

    lb_all = jnp.cumsum(jax.nn.softmax(hg_lb.astype(F32), axis=0), axis=0)
    lb_f, lb_b = lb_all[0, 0], lb_all[0, 1]

    rows = -(-(B + 1) // 8) * 8
    cond = jnp.concatenate([c, c_ctx[None], jnp.zeros((rows - B - 1, D), F32)], axis=0)
    ada = _ada(cond, w_ada[0], b_ada[0])
    sh1, sc1, g1, sh2, sc2, g2 = [ada[:B, i * D:(i + 1) * D] for i in range(6)]
    csh1 = jnp.broadcast_to(ada[B:B + 1, 0:D], (B, D))
    csc1 = jnp.broadcast_to(ada[B:B + 1, D:2 * D], (B, D))

    xc_m = _normmod(ctx, norm1[0], csh1, csc1)
    pc = _mm(xc_m.reshape(B * Lc, D), w_in[0], off_f, 3 * hgw, 1024, 512, BF16, "ctx_proj")
    st_f, st_b = _hgrn_context_states(pc.reshape(B, Lc, 3 * hgw), hgw, lb_f, lb_b)

    x_m = _normmod(x, norm1[0], sh1, sc1)
    proj = _mm(x_m.reshape(B * L, D), w_in[0], 0, in_cols, 1024, 512, BF16, "in_proj")
    proj3 = proj.reshape(B, L, in_cols)

    fwd, fwd_lo, inv = _dft_tables(L)
    hp, hm = _hy_filters(L, hy_w1[0], hy_b1[0], hy_w2[0], hy_b2[0], hy_w3[0], hy_b3[0], hy_freq[0],
                         hy_wout[0], hyw)
    n_orders = hp.shape[0]
    spectra = _filter_spectra(hp.reshape(n_orders, 2, L // 2, hyw), hm.reshape(n_orders, 2, L // 2, hyw),
                              fwd, fwd_lo)
    hyp = _short_conv(proj3, hy_conv_w[0], hy_conv_b[0])
    z = _long_conv_gated(hyp, 0, hyp, hyw, hyw, fwd, inv, spectra, 0, hy_skip[0, 0], False)
    y_hy = _long_conv_gated(z, 0, hyp, 2 * hyw, hyw, fwd, inv, spectra, 1, hy_skip[0, 1], True)

    y_hg = _hgrn(proj3, off_hg, hgw, lb_f, lb_b, hg_norm[0], st_f, st_b)

    merged = _merge(y_hy.reshape(B * L, hyw), y_hg.reshape(B * L, hgw), w_proj_hy[0], w_proj_hg[0],
                    proj, off_gate)
    x1 = _mm_res(merged, w_out[0], x.reshape(B * L, D), g1, L, 1024, 512).reshape(B, L, D)

    xm2, aff = _norm_router(x1, norm2[0], sh2, sc2, w_router[0])
    aff_t = jnp.swapaxes(aff[:, :, :E], 1, 2)
    slot_row = _route(aff_t, cap)
    xg = _gather(slot_row, xm2, cap)
    y = _experts(xg.reshape(E, B * cap, D), w_gate[0], w_up[0], w_down[0]).reshape(E, B, cap, D)
    slot_tok = jnp.pad(jnp.swapaxes(slot_row.reshape(B, E, L), 1, 2), ((0, 0), (0, 0), (0, LANES - E)),
                       constant_values=-1)
    x2 = _combine(slot_tok, aff, y, x1, g2)

    return _rmsnorm(x2.reshape(B * L, D), norm_f).reshape(B, L, D)
```

```python
import functools
import math

import jax
import jax.numpy as jnp
from jax import lax
from jax.experimental import pallas as pl
from jax.experimental.pallas import tpu as pltpu

F32 = jnp.float32
BF16 = jnp.bfloat16

EPS = 1e-6
GRID_W = 64
HY_FAST_DECAY = 0.3
HY_SLOW_DECAY = 1.5
HY_TARGET = 1e-2
HG_HEAD = 128
HG_CHUNK = 128
HG_SUB = 32
HG_HEADS_PER_STEP = 4
EC_CAPACITY = 2
LANES = 128
VMEM_LIMIT_BYTES = 56 * 1024 * 1024


def _cparams(n_axes):
    return pltpu.CompilerParams(dimension_semantics=("arbitrary",) * n_axes,
                                vmem_limit_bytes=VMEM_LIMIT_BYTES)


def _tile(pref, dim):
    t = min(pref, dim)
    while dim % t:
        t //= 2
    return t


def _dot(a, b):
    return jnp.dot(a, b, preferred_element_type=F32)


def _dot_nt(a, b):
    return lax.dot_general(a, b, (((1,), (1,)), ((), ())), preferred_element_type=F32)


def _dot_tn(a, b):
    return lax.dot_general(a, b, (((0,), (0,)), ((), ())), preferred_element_type=F32)


def _split(x):
    hi = x.astype(BF16)
    return hi, (x - hi.astype(F32)).astype(BF16)


def _dot3(a, b):
    ah, al = _split(a)
    bh, bl = _split(b)
    return _dot(ah, bh) + _dot(al, bh) + _dot(ah, bl)


def _silu(x):
    return x * jax.nn.sigmoid(x)


def _ada_kernel(c_ref, w_ref, b_ref, o_ref):
    o_ref[...] = _dot3(_silu(c_ref[...]), w_ref[...]) + b_ref[...]


def _ada(cond, w, b):
    R, D = cond.shape
    N = w.shape[1]
    tn = _tile(512, N)
    return pl.pallas_call(
        _ada_kernel, grid=(N // tn,),
        in_specs=[pl.BlockSpec((R, D), lambda n: (0, 0)),
                  pl.BlockSpec((D, tn), lambda n: (0, n)),
                  pl.BlockSpec((1, tn), lambda n: (0, n))],
        out_specs=pl.BlockSpec((R, tn), lambda n: (0, n)),
        out_shape=jax.ShapeDtypeStruct((R, N), F32),
        compiler_params=_cparams(1), name="ada")(cond, w, b.reshape(1, N))


def _rms_mod(x, g, sh, sc):
    y = x * lax.rsqrt(jnp.mean(x * x, axis=-1, keepdims=True) + EPS) * g
    return y * (1.0 + sc) + sh


def _normmod_kernel(x_ref, g_ref, sh_ref, sc_ref, o_ref):
    o_ref[0] = _rms_mod(x_ref[0], g_ref[...], sh_ref[0], sc_ref[0]).astype(o_ref.dtype)


def _normmod(x, g, sh, sc):
    B, L, D = x.shape
    tm = _tile(256, L)
    row = pl.BlockSpec((1, tm, D), lambda b, m: (b, m, 0))
    par = pl.BlockSpec((1, 1, D), lambda b, m: (b, 0, 0))
    return pl.pallas_call(
        _normmod_kernel, grid=(B, L // tm),
        in_specs=[row, pl.BlockSpec((1, D), lambda b, m: (0, 0)), par, par],
        out_specs=row, out_shape=jax.ShapeDtypeStruct((B, L, D), BF16),
        compiler_params=_cparams(2), name="normmod")(x, g.reshape(1, D), sh.reshape(B, 1, D), sc.reshape(B, 1, D))


def _rmsnorm_kernel(x_ref, g_ref, o_ref):
    x = x_ref[...]
    o_ref[...] = x * lax.rsqrt(jnp.mean(x * x, axis=-1, keepdims=True) + EPS) * g_ref[...]


def _rmsnorm(x, g):
    M, D = x.shape
    tm = _tile(256, M)
    row = pl.BlockSpec((tm, D), lambda m: (m, 0))
    return pl.pallas_call(
        _rmsnorm_kernel, grid=(M // tm,),
        in_specs=[row, pl.BlockSpec((1, D), lambda m: (0, 0))],
        out_specs=row, out_shape=jax.ShapeDtypeStruct((M, D), F32),
        compiler_params=_cparams(1), name="final_norm")(x, g.reshape(1, D))


def _mm_kernel(x_ref, w_ref, o_ref, wbf_ref):
    @pl.when(pl.program_id(1) == 0)
    def _():
        wbf_ref[...] = w_ref[...].astype(BF16)
    o_ref[...] = _dot(x_ref[...], wbf_ref[...]).astype(o_ref.dtype)


def _mm(x, w, col0, ncols, tm, tn, out_dtype, name):
    M, K = x.shape
    tm, tn = _tile(tm, M), _tile(tn, ncols)
    assert col0 % tn == 0
    c0 = col0 // tn
    return pl.pallas_call(
        _mm_kernel, grid=(ncols // tn, M // tm),
        in_specs=[pl.BlockSpec((tm, K), lambda n, m: (m, 0)),
                  pl.BlockSpec((K, tn), lambda n, m: (0, c0 + n))],
        out_specs=pl.BlockSpec((tm, tn), lambda n, m: (m, n)),
        out_shape=jax.ShapeDtypeStruct((M, ncols), out_dtype),
        scratch_shapes=[pltpu.VMEM((K, tn), BF16)],
        compiler_params=_cparams(2), name=name)(x, w)


def _mm_res_kernel(x_ref, w_ref, r_ref, g_ref, o_ref, wbf_ref):
    @pl.when(pl.program_id(1) == 0)
    def _():
        wbf_ref[...] = w_ref[...].astype(BF16)
    o_ref[...] = r_ref[...] + g_ref[0] * _dot(x_ref[...], wbf_ref[...])


def _mm_res(x, w, res, gate, rows_per_batch, tm, tn):
    M, K = x.shape
    N = w.shape[1]
    tm, tn = _tile(tm, rows_per_batch), _tile(tn, N)
    mpb = rows_per_batch // tm
    B = gate.shape[0]
    return pl.pallas_call(
        _mm_res_kernel, grid=(N // tn, M // tm),
        in_specs=[pl.BlockSpec((tm, K), lambda n, m: (m, 0)),
                  pl.BlockSpec((K, tn), lambda n, m: (0, n)),
                  pl.BlockSpec((tm, tn), lambda n, m: (m, n)),
                  pl.BlockSpec((1, 1, tn), lambda n, m: (m // mpb, 0, n))],
        out_specs=pl.BlockSpec((tm, tn), lambda n, m: (m, n)),
        out_shape=jax.ShapeDtypeStruct((M, N), F32),
        scratch_shapes=[pltpu.VMEM((K, tn), BF16)],
        compiler_params=_cparams(2), name="out_proj")(x, w, res, gate.reshape(B, 1, N))


def _merge_kernel(yhy_ref, yhg_ref, why_ref, whg_ref, ghy_ref, ghg_ref, o_ref, why_s, whg_s):
    @pl.when(pl.program_id(1) == 0)
    def _():
        why_s[...] = why_ref[...].astype(BF16)
        whg_s[...] = whg_ref[...].astype(BF16)
    a = _dot(yhy_ref[...], why_s[...])
    b = _dot(yhg_ref[...], whg_s[...])
    o_ref[...] = (jax.nn.sigmoid(ghy_ref[...].astype(F32)) * a
                  + jax.nn.sigmoid(ghg_ref[...].astype(F32)) * b).astype(o_ref.dtype)


def _merge(yhy, yhg, why, whg, proj, off_gate):
    M, K1 = yhy.shape
    K2 = yhg.shape[1]
    D = why.shape[1]
    tm, tn = _tile(1024, M), _tile(512, D)
    assert off_gate % tn == 0
    g0, g1 = off_gate // tn, (off_gate + D) // tn
    return pl.pallas_call(
        _merge_kernel, grid=(D // tn, M // tm),
        in_specs=[pl.BlockSpec((tm, K1), lambda n, m: (m, 0)),
                  pl.BlockSpec((tm, K2), lambda n, m: (m, 0)),
                  pl.BlockSpec((K1, tn), lambda n, m: (0, n)),
                  pl.BlockSpec((K2, tn), lambda n, m: (0, n)),
                  pl.BlockSpec((tm, tn), lambda n, m: (m, g0 + n)),
                  pl.BlockSpec((tm, tn), lambda n, m: (m, g1 + n))],
        out_specs=pl.BlockSpec((tm, tn), lambda n, m: (m, n)),
        out_shape=jax.ShapeDtypeStruct((M, D), BF16),
        scratch_shapes=[pltpu.VMEM((K1, tn), BF16), pltpu.VMEM((K2, tn), BF16)],
        compiler_params=_cparams(2), name="merge")(yhy, yhg, why, whg, proj, proj)


def _mm_sconv_kernel(x_ref, w_ref, cw_ref, cb_ref, o_ref, wbf_ref, y_s):
    @pl.when(pl.program_id(1) == 0)
    def _():
        wbf_ref[...] = w_ref[...].astype(BF16)
    x = _dot(x_ref[...], wbf_ref[...])
    tl = x.shape[0]
    pos = lax.broadcasted_iota(jnp.int32, x.shape, 0) % GRID_W
    prev = jnp.where(pos != 0, pltpu.roll(x, 1, 0), 0.0)
    nxt = jnp.where(pos != GRID_W - 1, pltpu.roll(x, tl - 1, 0), 0.0)
    cw = cw_ref[...]
    y = prev * cw[0:1] + x * cw[1:2] + nxt * cw[2:3] + cb_ref[...]
    for g in range(y_s.shape[0]):
        ln = slice(g * LANES, (g + 1) * LANES)
        y_s[g] = y[:, ln]
        o_ref[0, 0, :, ln] = y_s[g, pl.ds(0, tl // 2, stride=2), :].astype(o_ref.dtype)
        o_ref[0, 1, :, ln] = y_s[g, pl.ds(1, tl // 2, stride=2), :].astype(o_ref.dtype)


def _mm_short_conv(x, w, ncols, conv_w, conv_b, B, L):
    M, K = x.shape
    tm, tn = _tile(1024, L), _tile(512, ncols)
    assert tm % GRID_W == 0 and tn % LANES == 0
    mpb = L // tm
    return pl.pallas_call(
        _mm_sconv_kernel, grid=(ncols // tn, M // tm),
        in_specs=[pl.BlockSpec((tm, K), lambda n, m: (m, 0)),
                  pl.BlockSpec((K, tn), lambda n, m: (0, n)),
                  pl.BlockSpec((3, tn), lambda n, m: (0, n)),
                  pl.BlockSpec((1, tn), lambda n, m: (0, n))],
        out_specs=pl.BlockSpec((1, 2, tm // 2, tn), lambda n, m: (m // mpb, 0, m % mpb, n)),
        out_shape=jax.ShapeDtypeStruct((B, 2, L // 2, ncols), BF16),
        scratch_shapes=[pltpu.VMEM((K, tn), BF16), pltpu.VMEM((tn // LANES, tm, LANES), F32)],
        compiler_params=_cparams(2), name="in_proj_hyena")(x, w, conv_w, conv_b.reshape(1, ncols))


def _dft_tables(L):
    lh, n = L // 2, 2 * L
    k = jnp.arange(lh, dtype=jnp.int32)
    alt = jnp.where(k % 2 == 0, 1.0, -1.0).astype(F32)
    ang_e = ((k[:, None] * (2 * k)[None, :]) % n).astype(F32) * (2.0 * math.pi / n)
    ang_o = ((k[:, None] * (2 * k + 1)[None, :]) % n).astype(F32) * (2.0 * math.pi / n)
    ce, co, se, so = jnp.cos(ang_e), jnp.cos(ang_o), -jnp.sin(ang_e), -jnp.sin(ang_o)
    fwd = jnp.stack([ce, co, se.at[0].set(alt), so.at[0].set(-alt)])
    inv = (2.0 / n) * jnp.stack([ce.T.at[:, 0].set(0.5), co.T.at[:, 0].set(0.5),
                                 se.T.at[:, 0].set(alt), so.T.at[:, 0].set(-alt)])
    fwd_hi, fwd_lo = _split(fwd)
    return fwd_hi, fwd_lo, inv.astype(BF16)


def _dftf_kernel(ue_ref, uo_ref, t_ref, krl_ref, krh_ref, kil_ref, kih_ref, ae_ref, ao_ref, be_ref, bo_ref):
    tm = t_ref.shape[1]
    ue, uo = ue_ref[0, 0], uo_ref[0, 0]
    p, q = _dot(t_ref[0], ue), _dot(t_ref[1], uo)
    r, t = _dot(t_ref[2], ue), _dot(t_ref[3], uo)
    first = (lax.broadcasted_iota(jnp.int32, p.shape, 0) + pl.program_id(1) * tm) == 0
    krl, krh, kil, kih = krl_ref[0], krh_ref[0], kil_ref[0], kih_ref[0]
    m_re, m_im, km_re, km_im = r[0:1], t[0:1], kil[0:1], kih[0:1]
    r, t = jnp.where(first, 0.0, r), jnp.where(first, 0.0, t)
    kil, kih = jnp.where(first, 0.0, kil), jnp.where(first, 0.0, kih)
    re_l, re_h, im_l, im_h = p + q, p - q, r + t, t - r
    yr_l, yi_l = re_l * krl - im_l * kil, re_l * kil + im_l * krl
    yr_h, yi_h = re_h * krh - im_h * kih, re_h * kih + im_h * krh
    ae_ref[0] = (yr_l + yr_h).astype(ae_ref.dtype)
    ao_ref[0] = (yr_l - yr_h).astype(ao_ref.dtype)
    be_ref[0] = jnp.where(first, m_re * km_re - m_im * km_im, yi_l - yi_h).astype(be_ref.dtype)
    bo_ref[0] = jnp.where(first, m_re * km_im + m_im * km_re, yi_l + yi_h).astype(bo_ref.dtype)


def _dfti_kernel(ae_ref, ao_ref, be_ref, bo_ref, g_ref, u_ref, gate_ref, skip_ref, o_ref, *y_s):
    skip = skip_ref[...]
    ye = _dot(g_ref[0], ae_ref[0]) + _dot(g_ref[2], be_ref[0])
    yo = _dot(g_ref[1], ao_ref[0]) + _dot(g_ref[3], bo_ref[0])
    oe = gate_ref[0, 0].astype(F32) * (ye + skip * u_ref[0, 0].astype(F32))
    oo = gate_ref[0, 1].astype(F32) * (yo + skip * u_ref[0, 1].astype(F32))
    if y_s:
        tm = oe.shape[0]
        for g in range(y_s[0].shape[0]):
            ln = slice(g * LANES, (g + 1) * LANES)
            y_s[0][g, pl.ds(0, tm, stride=2), :] = oe[:, ln]
            y_s[0][g, pl.ds(1, tm, stride=2), :] = oo[:, ln]
            o_ref[0, :, ln] = y_s[0][g].astype(o_ref.dtype)
    else:
        o_ref[0, 0] = oe.astype(o_ref.dtype)
        o_ref[0, 1] = oo.astype(o_ref.dtype)


def _long_conv_gated(u_arr, u_col0, gate_arr, gate_col0, C, fwd, inv, spectra, order, skip, natural_out):
    B, _, lh, _ = u_arr.shape
    tm, tn = _tile(512, lh), _tile(512, C)
    assert u_col0 % tn == 0 and gate_col0 % tn == 0
    uc, gc = u_col0 // tn, gate_col0 // tn
    grid = (C // tn, lh // tm, B)
    tab = pl.BlockSpec((4, tm, lh), lambda n, m, b: (0, m, 0))
    kspec = pl.BlockSpec((1, tm, tn), lambda n, m, b: (order, m, n))
    ys = pl.pallas_call(
        _dftf_kernel, grid=grid,
        in_specs=[pl.BlockSpec((1, 1, lh, tn), lambda n, m, b: (b, 0, 0, uc + n)),
                  pl.BlockSpec((1, 1, lh, tn), lambda n, m, b: (b, 1, 0, uc + n)),
                  tab, kspec, kspec, kspec, kspec],
        out_specs=[pl.BlockSpec((1, tm, tn), lambda n, m, b: (b, m, n))] * 4,
        out_shape=[jax.ShapeDtypeStruct((B, lh, C), BF16)] * 4,
        compiler_params=_cparams(3), name="dft_fwd")(u_arr, u_arr, fwd, *spectra)
    yspec = pl.BlockSpec((1, lh, tn), lambda n, m, b: (b, 0, n))
    if natural_out:
        out_spec = pl.BlockSpec((1, 2 * tm, tn), lambda n, m, b: (b, m, n))
        out_shape = jax.ShapeDtypeStruct((B, 2 * lh, C), BF16)
        scratch = [pltpu.VMEM((tn // LANES, 2 * tm, LANES), F32)]
    else:
        out_spec = pl.BlockSpec((1, 2, tm, tn), lambda n, m, b: (b, 0, m, n))
        out_shape = jax.ShapeDtypeStruct((B, 2, lh, C), BF16)
        scratch = []
    return pl.pallas_call(
        _dfti_kernel, grid=grid,
        in_specs=[yspec, yspec, yspec, yspec, tab,
                  pl.BlockSpec((1, 2, tm, tn), lambda n, m, b: (b, 0, m, uc + n)),
                  pl.BlockSpec((1, 2, tm, tn), lambda n, m, b: (b, 0, m, gc + n)),
                  pl.BlockSpec((1, tn), lambda n, m, b: (0, n))],
        out_specs=out_spec, out_shape=out_shape, scratch_shapes=scratch,
        compiler_params=_cparams(3), name="dft_inv")(*ys, inv, u_arr, gate_arr, skip.reshape(1, C))


def _hy_filter_kernel(z_ref, w1_ref, b1_ref, w2_ref, b2_ref, w3_ref, b3_ref, fr_ref, wf_ref, wb_ref,
                      t_ref, ad_ref, hp_ref, hm_ref, hh_s, hl_s):
    @pl.when((pl.program_id(0) == 0) & (pl.program_id(1) == 0))
    def _():
        fr = fr_ref[...]
        h = jnp.sin(fr * (_dot3(z_ref[...], w1_ref[...]) + b1_ref[...]))
        h = jnp.sin(fr * (_dot3(h, w2_ref[...]) + b2_ref[...]))
        h = jnp.sin(fr * (_dot3(h, w3_ref[...]) + b3_ref[...]))
        hh_s[...], hl_s[...] = _split(h)

    def dot3h(w):
        wh, wl = _split(w)
        return _dot(hh_s[...], wh) + _dot(hl_s[...], wh) + _dot(hh_s[...], wl)

    decay = jnp.exp(-t_ref[...] * ad_ref[...])
    hf = dot3h(wf_ref[...]) * decay
    hb = dot3h(wb_ref[...]) * decay
    row = lax.broadcasted_iota(jnp.int32, hb.shape, 0)
    hb = jnp.where(row == 0, 0.0, hb)
    hp_ref[0] = hf + hb
    hm_ref[0] = hf - hb


def _hy_filters(L, w1, b1, w2, b2, w3, b3, freq, wout, width):
    emb, ffn = w1.shape
    n_orders = wout.shape[1] // (2 * width)
    t = jnp.linspace(0.0, 1.0, L, dtype=F32)[:, None]
    bands = (emb - 1) // 2
    w = 2.0 * math.pi * jnp.arange(L, dtype=F32) / L
    f = jnp.linspace(1e-4, bands - 1, bands, dtype=F32)
    fw = w[:, None] * f[None, :]
    z = jnp.concatenate([t, jnp.cos(fw), -jnp.sin(fw)], axis=-1)
    z = jnp.concatenate([z[0::2], z[1::2]], axis=0)
    t = jnp.concatenate([t[0::2], t[1::2]], axis=0)
    deltas = jnp.linspace(math.log(HY_TARGET) / HY_SLOW_DECAY, math.log(HY_TARGET) / HY_FAST_DECAY,
                          width, dtype=F32)
    pe, pf = LANES - emb, LANES - ffn
    z = jnp.pad(z, ((0, 0), (0, pe)))
    w1p = jnp.pad(w1, ((0, pe), (0, pf)))
    w2p = jnp.pad(w2, ((0, pf), (0, pf)))
    w3p = jnp.pad(w3, ((0, pf), (0, pf)))
    woutp = jnp.pad(wout, ((0, pf), (0, 0)))
    vec = lambda a: jnp.pad(a, (0, pf)).reshape(1, LANES)
    tc = _tile(512, width)
    nt = width // tc
    full = lambda shp: pl.BlockSpec(shp, lambda o, n: (0, 0))
    out = pl.BlockSpec((1, L, tc), lambda o, n: (o, 0, n))
    shp = jax.ShapeDtypeStruct((n_orders, L, width), F32)
    return pl.pallas_call(
        _hy_filter_kernel, grid=(n_orders, nt),
        in_specs=[full((L, LANES)), full((LANES, LANES)), full((1, LANES)), full((LANES, LANES)), full((1, LANES)),
                  full((LANES, LANES)), full((1, LANES)), full((1, LANES)),
                  pl.BlockSpec((LANES, tc), lambda o, n: (0, (2 * o) * nt + n)),
                  pl.BlockSpec((LANES, tc), lambda o, n: (0, (2 * o + 1) * nt + n)),
                  full((L, 1)), pl.BlockSpec((1, tc), lambda o, n: (0, n))],
        out_specs=[out, out], out_shape=[shp, shp],
        scratch_shapes=[pltpu.VMEM((L, LANES), BF16)] * 2,
        compiler_params=_cparams(2), name="hy_filter")(
            z, w1p, vec(b1), w2p, vec(b2), w3p, vec(b3), vec(freq), woutp, woutp, t, jnp.abs(deltas).reshape(1, width))


def _spectrum_kernel(th_ref, tl_ref, hp_ref, hm_ref, krl_ref, krh_ref, kil_ref, kih_ref):
    tm = th_ref.shape[1]

    def dot3(i, x, rows=slice(None)):
        xh, xl = x
        return _dot(th_ref[i, rows], xh) + _dot(tl_ref[i, rows], xh) + _dot(th_ref[i, rows], xl)

    pe, po = _split(hp_ref[0, 0]), _split(hp_ref[0, 1])
    me, mo = _split(hm_ref[0, 0]), _split(hm_ref[0, 1])
    p, q, r, t = dot3(0, pe), dot3(1, po), dot3(2, me), dot3(3, mo)
    mid_re = dot3(2, pe, slice(0, 8))[0:1]
    first = (lax.broadcasted_iota(jnp.int32, p.shape, 0) + pl.program_id(2) * tm) == 0
    krl_ref[0] = p + q
    krh_ref[0] = p - q
    kil_ref[0] = jnp.where(first, mid_re, r + t)
    kih_ref[0] = jnp.where(first, t[0:1], t - r)


def _filter_spectra(hp, hm, fwd_hi, fwd_lo):
    n_orders, _, lh, width = hp.shape
    tm, tn = _tile(512, lh), _tile(512, width)
    tab = pl.BlockSpec((4, tm, lh), lambda o, n, m: (0, m, 0))
    hs = pl.BlockSpec((1, 2, lh, tn), lambda o, n, m: (o, 0, 0, n))
    out = pl.BlockSpec((1, tm, tn), lambda o, n, m: (o, m, n))
    shp = jax.ShapeDtypeStruct((n_orders, lh, width), F32)
    return pl.pallas_call(
        _spectrum_kernel, grid=(n_orders, width // tn, lh // tm),
        in_specs=[tab, tab, hs, hs], out_specs=[out] * 4, out_shape=[shp] * 4,
        compiler_params=_cparams(3), name="hy_spectrum")(fwd_hi, fwd_lo, hp, hm)


def _hg_gate(z, lb):
    f = lb + (1.0 - lb) * jax.nn.sigmoid(z)
    return jnp.log(f), 1.0 - f


def _chunk_cumsum(g, rev):
    n = g.shape[0]
    pos = lax.broadcasted_iota(jnp.int32, g.shape, 0) % HG_CHUNK
    d = 1
    while d < HG_CHUNK:
        if rev:
            g = g + jnp.where(pos < HG_CHUNK - d, pltpu.roll(g, n - d, 0), 0.0)
        else:
            g = g + jnp.where(pos >= d, pltpu.roll(g, d, 0), 0.0)
        d *= 2
    return g


def _row_at(b, step, rev):
    r = HG_CHUNK - 1 - step if rev else step
    return b[r:r + 1, :]


def _hg_kv(k, v, b, rev):
    bl = _row_at(b, HG_CHUNK - 1, rev)
    return _dot_tn(v.astype(BF16), (k * jnp.exp(bl - b)).astype(BF16))


def _hg_state(k, v, b, st, rev):
    return st * jnp.exp(_row_at(b, HG_CHUNK - 1, rev)) + _hg_kv(k, v, b, rev)


def _hg_intra(q, k, v, b, rev):
    C, SB = HG_CHUNK, HG_SUB
    nsb = C // SB
    row = lax.broadcasted_iota(jnp.int32, q.shape, 0)
    p = (C - 1 - row) if rev else row
    mids = _row_at(b, SB // 2, rev)
    ends = _row_at(b, SB - 1, rev)
    for i in range(1, nsb):
        mids = jnp.where(p >= i * SB, _row_at(b, i * SB + SB // 2, rev), mids)
        ends = jnp.where(p >= i * SB, _row_at(b, i * SB + SB - 1, rev), ends)
    tt = lax.broadcasted_iota(jnp.int32, (C, C), 0)
    ss = lax.broadcasted_iota(jnp.int32, (C, C), 1)
    if rev:
        tt, ss = C - 1 - tt, C - 1 - ss
    ad = _dot_nt((q * jnp.exp(b - mids)).astype(BF16), (k * jnp.exp(mids - b)).astype(BF16))
    a = jnp.where((tt // SB == ss // SB) & (ss <= tt), ad, 0.0)
    ke = k * jnp.exp(ends - b)
    qs, ks = [], []
    for j in range(nsb - 1):
        ej = _row_at(b, j * SB + SB - 1, rev)
        qs.append(jnp.where(p >= (j + 1) * SB, q * jnp.exp(jnp.minimum(b - ej, 0.0)), 0.0).astype(BF16))
        ks.append(jnp.where((p >= j * SB) & (p < (j + 1) * SB), ke, 0.0).astype(BF16))
    a = a + _dot_nt(jnp.concatenate(qs, axis=1), jnp.concatenate(ks, axis=1))
    return _dot(a.astype(BF16), v.astype(BF16))


def _hgrn_kernel(q_ref, zf_ref, zb_ref, i_ref, g_ref, lbf_ref, lbb_ref, gn_ref, sf_ref, sb_ref, o_ref,
                 kf_s, kb_s, bf_s, cb_s, of_s, ob_s):
    C = HG_CHUNK
    nc = kf_s.shape[0] // C
    nh = kf_s.shape[1] // HG_HEAD
    gf, kf = _hg_gate(zf_ref[0].astype(F32), lbf_ref[0])
    kf_s[...] = kf
    bf_s[...] = _chunk_cumsum(gf, False)
    gb, kb = _hg_gate(zb_ref[0].astype(F32), lbb_ref[0])
    kb_s[...] = kb
    cb_s[...] = _chunk_cumsum(gb, True)
    dirs = ((kf_s, bf_s, of_s, False), (kb_s, cb_s, ob_s, True))

    def body(c, carry):
        out = []
        for h in range(nh):
            ln = slice(h * HG_HEAD, (h + 1) * HG_HEAD)
            for d, (k_s, b_s, o_s, rev) in enumerate(dirs):
                cc = nc - 1 - c if rev else c
                rows = pl.ds(pl.multiple_of(cc * C, C), C)
                st = carry[2 * h + d]
                q = _silu(q_ref[0, rows, ln].astype(F32)) * (HG_HEAD ** -0.5)
                v = i_ref[0, rows, ln].astype(F32)
                k, b = k_s[rows, ln], b_s[rows, ln]
                o_s[rows, ln] = (_hg_intra(q, k, v, b, rev)
                                 + _dot_nt((q * jnp.exp(b)).astype(BF16), st.astype(BF16)))
                out.append(_hg_state(k, v, b, st, rev))
        return tuple(out)

    init = []
    for h in range(nh):
        init += [sf_ref[0, h], sb_ref[0, h]]
    lax.fori_loop(0, nc, body, tuple(init))
    for h in range(nh):
        ln = slice(h * HG_HEAD, (h + 1) * HG_HEAD)
        o = of_s[:, ln] + ob_s[:, ln]
        o = o * lax.rsqrt(jnp.mean(o * o, axis=-1, keepdims=True) + EPS)
        o_ref[0, :, ln] = (o * gn_ref[0, :, ln] * _silu(g_ref[0, :, ln].astype(F32))).astype(o_ref.dtype)


def _hgrn(proj3, off_hg, width, lb_f, lb_b, gn, st_f, st_b):
    B, L, _ = proj3.shape
    H = width // HG_HEAD
    nh = _tile(HG_HEADS_PER_STEP, H)
    wb = nh * HG_HEAD
    assert L % HG_CHUNK == 0 and off_hg % wb == 0
    c0, hb = off_hg // wb, H // nh

    def col(i):
        return pl.BlockSpec((1, L, wb), lambda b, h: (b, 0, c0 + i * hb + h))

    par = pl.BlockSpec((1, 1, wb), lambda b, h: (0, 0, h))
    st = pl.BlockSpec((1, nh, HG_HEAD, HG_HEAD), lambda b, h: (b, h, 0, 0))
    return pl.pallas_call(
        _hgrn_kernel, grid=(B, hb),
        in_specs=[col(0), col(1), col(2), col(3), col(4), par, par, par, st, st],
        out_specs=pl.BlockSpec((1, L, wb), lambda b, h: (b, 0, h)),
        out_shape=jax.ShapeDtypeStruct((B, L, width), BF16),
        scratch_shapes=[pltpu.VMEM((L, wb), F32)] * 6,
        compiler_params=_cparams(2), name="hgrn")(
            proj3, proj3, proj3, proj3, proj3,
            lb_f.reshape(1, 1, width), lb_b.reshape(1, 1, width), gn.reshape(1, 1, width), st_f, st_b)


def _hgctx_kernel(zf_ref, zb_ref, i_ref, lbf_ref, lbb_ref, sf_ref, sb_ref):
    C = HG_CHUNK
    nc = zf_ref.shape[1] // C
    gf, kf = _hg_gate(zf_ref[0].astype(F32), lbf_ref[0])
    gb, kb = _hg_gate(zb_ref[0].astype(F32), lbb_ref[0])
    bf = _chunk_cumsum(gf, False)
    cb = _chunk_cumsum(gb, True)
    v = i_ref[0].astype(F32)
    stf = jnp.zeros((HG_HEAD, HG_HEAD), F32)
    stb = jnp.zeros((HG_HEAD, HG_HEAD), F32)
    for c in range(nc):
        sl = slice(c * C, (c + 1) * C)
        stf = _hg_state(kf[sl], v[sl], bf[sl], stf, False)
    for c in reversed(range(nc)):
        sl = slice(c * C, (c + 1) * C)
        stb = _hg_state(kb[sl], v[sl], cb[sl], stb, True)
    sf_ref[0, 0] = stf
    sb_ref[0, 0] = stb


def _hgrn_context_states(pc3, width, lb_f, lb_b):
    B, Lc, _ = pc3.shape
    H = width // HG_HEAD
    assert Lc % HG_CHUNK == 0

    def col(i):
        return pl.BlockSpec((1, Lc, HG_HEAD), lambda b, h: (b, 0, i * H + h))

    par = pl.BlockSpec((1, 1, HG_HEAD), lambda b, h: (h, 0, 0))
    st = pl.BlockSpec((1, 1, HG_HEAD, HG_HEAD), lambda b, h: (b, h, 0, 0))
    shp = jax.ShapeDtypeStruct((B, H, HG_HEAD, HG_HEAD), F32)
    return pl.pallas_call(
        _hgctx_kernel, grid=(B, H),
        in_specs=[col(0), col(1), col(2), par, par],
        out_specs=[st, st], out_shape=[shp, shp],
        compiler_params=_cparams(2), name="hgrn_ctx")(
            pc3, pc3, pc3, lb_f.reshape(H, 1, HG_HEAD), lb_b.reshape(H, 1, HG_HEAD))


def _norm_router_kernel(x_ref, g_ref, sh_ref, sc_ref, wr_ref, xm_ref, aff_ref, *, n_experts):
    xm = _rms_mod(x_ref[0], g_ref[...], sh_ref[0], sc_ref[0])
    xm_ref[0] = xm.astype(xm_ref.dtype)
    logits = _dot3(xm, wr_ref[...])
    lane = lax.broadcasted_iota(jnp.int32, logits.shape, 1)
    logits = jnp.where(lane < n_experts, logits, -1e30)
    e = jnp.exp(logits - jnp.max(logits, axis=-1, keepdims=True))
    aff_ref[0] = e / jnp.sum(e, axis=-1, keepdims=True)


def _norm_router(x, g, sh, sc, w_router):
    B, L, D = x.shape
    E = w_router.shape[1]
    tm = _tile(256, L)
    wr = jnp.pad(w_router, ((0, 0), (0, LANES - E)))
    row = pl.BlockSpec((1, tm, D), lambda b, m: (b, m, 0))
    par = pl.BlockSpec((1, 1, D), lambda b, m: (b, 0, 0))
    return pl.pallas_call(
        functools.partial(_norm_router_kernel, n_experts=E), grid=(B, L // tm),
        in_specs=[row, pl.BlockSpec((1, D), lambda b, m: (0, 0)), par, par,
                  pl.BlockSpec((D, LANES), lambda b, m: (0, 0))],
        out_specs=[row, pl.BlockSpec((1, tm, LANES), lambda b, m: (b, m, 0))],
        out_shape=[jax.ShapeDtypeStruct((B, L, D), BF16), jax.ShapeDtypeStruct((B, L, LANES), F32)],
        compiler_params=_cparams(2), name="norm_router")(
            x, g.reshape(1, D), sh.reshape(B, 1, D), sc.reshape(B, 1, D), wr)


def _route_kernel(a_ref, tri_ref, slot_ref, *, cap):
    bits = lax.bitcast_convert_type(a_ref[...], jnp.int32)

    def count(mask):
        return jnp.sum(jnp.where(mask, 1.0, 0.0), axis=-1, keepdims=True)

    def body(i, thr):
        cand = thr | jnp.left_shift(jnp.int32(1), 30 - i)
        return jnp.where(count(bits >= cand) >= cap, cand, thr)

    thr = lax.fori_loop(0, 31, body, jnp.zeros((bits.shape[0], 1), jnp.int32))
    above, tie = bits > thr, bits == thr
    tri = tri_ref[...]
    ties_before = _dot(jnp.where(tie, 1.0, 0.0).astype(BF16), tri)
    chosen = above | (tie & (ties_before < cap - count(above)))
    before = _dot(jnp.where(chosen, 1.0, 0.0).astype(BF16), tri)
    slot_ref[...] = jnp.where(chosen, before.astype(jnp.int32), -1)


def _route(aff_t, cap):
    B, E, L = aff_t.shape
    idx = jnp.arange(L, dtype=jnp.int32)
    tri = (idx[:, None] < idx[None, :]).astype(BF16)
    slot = pl.pallas_call(
        functools.partial(_route_kernel, cap=cap), grid=(1,),
        in_specs=[pl.BlockSpec((B * E, L), lambda i: (0, 0)), pl.BlockSpec((L, L), lambda i: (0, 0))],
        out_specs=pl.BlockSpec((B * E, L), lambda i: (0, 0)),
        out_shape=jax.ShapeDtypeStruct((B * E, L), jnp.int32),
        compiler_params=_cparams(1), name="route")(aff_t.reshape(B * E, L), tri)
    return slot.reshape(B, E, 1, L)


def _gather_kernel(slot_ref, xm_ref, o_ref):
    ne, _, cap, td = o_ref.shape
    L = slot_ref.shape[3]
    want = lax.broadcasted_iota(jnp.int32, (cap, L), 0)
    sel = jnp.concatenate([jnp.where(want == slot_ref[0, e], 1.0, 0.0).astype(BF16) for e in range(ne)], axis=0)
    o_ref[:, 0] = _dot(sel, xm_ref[0]).astype(o_ref.dtype).reshape(ne, cap, td)


def _gather(slot_row, xm, cap):
    B, E, _, L = slot_row.shape
    D = xm.shape[2]
    td, ne = _tile(1024, D), _tile(4, E)
    return pl.pallas_call(
        _gather_kernel, grid=(B, D // td, E // ne),
        in_specs=[pl.BlockSpec((1, ne, 1, L), lambda b, d, e: (b, e, 0, 0)),
                  pl.BlockSpec((1, L, td), lambda b, d, e: (b, 0, d))],
        out_specs=pl.BlockSpec((ne, 1, cap, td), lambda b, d, e: (e, b, 0, d)),
        out_shape=jax.ShapeDtypeStruct((E, B, cap, D), BF16),
        compiler_params=_cparams(3), name="moe_gather")(slot_row, xm)


def _expert_up_kernel(x_ref, wg_ref, wu_ref, h_ref):
    x = x_ref[0]
    a = _dot(x, wg_ref[0].astype(BF16))
    u = _dot(x, wu_ref[0].astype(BF16))
    h_ref[0] = (_silu(a) * u).astype(h_ref.dtype)


def _expert_down_kernel(h_ref, wd_ref, y_ref):
    y_ref[0] = _dot(h_ref[0], wd_ref[0].astype(BF16)).astype(y_ref.dtype)


def _experts(xg, w_gate, w_up, w_down):
    E, M, D = xg.shape
    FF = w_gate.shape[2]
    tf, td = _tile(256, FF), _tile(1024, D)
    h = pl.pallas_call(
        _expert_up_kernel, grid=(E, FF // tf),
        in_specs=[pl.BlockSpec((1, M, D), lambda e, n: (e, 0, 0)),
                  pl.BlockSpec((1, D, tf), lambda e, n: (e, 0, n)),
                  pl.BlockSpec((1, D, tf), lambda e, n: (e, 0, n))],
        out_specs=pl.BlockSpec((1, M, tf), lambda e, n: (e, 0, n)),
        out_shape=jax.ShapeDtypeStruct((E, M, FF), BF16),
        compiler_params=_cparams(2), name="expert_up")(xg, w_gate, w_up)
    return pl.pallas_call(
        _expert_down_kernel, grid=(E, D // td),
        in_specs=[pl.BlockSpec((1, M, FF), lambda e, n: (e, 0, 0)),
                  pl.BlockSpec((1, FF, td), lambda e, n: (e, 0, n))],
        out_specs=pl.BlockSpec((1, M, td), lambda e, n: (e, 0, n)),
        out_shape=jax.ShapeDtypeStruct((E, M, D), BF16),
        compiler_params=_cparams(2), name="expert_down")(h, w_down)


def _combine_kernel(sl_ref, af_ref, y_ref, x_ref, g_ref, o_ref, w_s):
    E, _, cap, td = y_ref.shape

    @pl.when(pl.program_id(2) == 0)
    def _():
        sl, af = sl_ref[0], af_ref[0]
        want = lax.broadcasted_iota(jnp.int32, (w_s.shape[0], cap), 1)
        for e in range(E):
            w_s[:, e * cap:(e + 1) * cap] = jnp.where(sl[:, e:e + 1] == want, af[:, e:e + 1], 0.0).astype(w_s.dtype)

    y = y_ref[:, 0].reshape(E * cap, td)
    o_ref[0] = x_ref[0] + g_ref[0] * _dot(w_s[...], y)


def _combine(slot_tok, aff, y, x, gate):
    B, L, _ = slot_tok.shape
    E, cap, D = y.shape[0], y.shape[2], y.shape[3]
    tm, td = _tile(512, L), _tile(1024, D)
    lanespec = pl.BlockSpec((1, tm, LANES), lambda b, m, d: (b, m, 0))
    xspec = pl.BlockSpec((1, tm, td), lambda b, m, d: (b, m, d))
    return pl.pallas_call(
        _combine_kernel, grid=(B, L // tm, D // td),
        in_specs=[lanespec, lanespec,
                  pl.BlockSpec((E, 1, cap, td), lambda b, m, d: (0, b, 0, d)),
                  xspec, pl.BlockSpec((1, 1, td), lambda b, m, d: (b, 0, d))],
        out_specs=xspec, out_shape=jax.ShapeDtypeStruct((B, L, D), F32),
        scratch_shapes=[pltpu.VMEM((tm, E * cap), BF16)],
        compiler_params=_cparams(3), name="moe_combine")(slot_tok, aff, y, x, gate.reshape(B, 1, D))


def kernel(x, c, ctx, c_ctx, w_ada, b_ada, norm1, norm2, w_in, hy_conv_w, hy_conv_b, hy_w1, hy_b1, hy_w2, hy_b2, hy_w3, hy_b3, hy_freq, hy_wout, hy_skip, hg_lb, hg_norm, w_proj_hy, w_proj_hg, w_out, w_router, w_gate, w_up, w_down, norm_f):
    assert w_ada.shape[0] == 1, "single-layer block"
    B, L, D = x.shape
    Lc = ctx.shape[1]
    hyw, hgw = hy_skip.shape[-1], hg_norm.shape[-1]
    H = hgw // HG_HEAD
    E = w_router.shape[-1]
    off_hg = 3 * hyw
    off_f = off_hg + hgw
    off_gate = off_hg + 5 * hgw
    in_cols = off_gate + 2 * D
    assert w_in.shape[-1] == in_cols and L % GRID_W == 0
    cap = EC_CAPACITY * L // E

    lb_all = jnp.cumsum(jax.nn.softmax(hg_lb.astype(F32), axis=0), axis=0)
    lb_f, lb_b = lb_all[0, 0], lb_all[0, 1]

    rows = -(-(B + 1) // 8) * 8
    cond = jnp.concatenate([c, c_ctx[None], jnp.zeros((rows - B - 1, D), F32)], axis=0)
    ada = _ada(cond, w_ada[0], b_ada[0])
    sh1, sc1, g1, sh2, sc2, g2 = [ada[:B, i * D:(i + 1) * D] for i in range(6)]
    csh1 = jnp.broadcast_to(ada[B:B + 1, 0:D], (B, D))
    csc1 = jnp.broadcast_to(ada[B:B + 1, D:2 * D], (B, D))

    xc_m = _normmod(ctx, norm1[0], csh1, csc1)
    pc = _mm(xc_m.reshape(B * Lc, D), w_in[0], off_f, 3 * hgw, 1024, 512, BF16, "ctx_proj")
    st_f, st_b = _hgrn_context_states(pc.reshape(B, Lc, 3 * hgw), hgw, lb_f, lb_b)

    x_m = _normmod(x, norm1[0], sh1, sc1).reshape(B * L, D)
    hyp = _mm_short_conv(x_m, w_in[0], off_hg, hy_conv_w[0], hy_conv_b[0], B, L)
    proj = _mm(x_m, w_in[0], off_hg, in_cols - off_hg, 1024, 512, BF16, "in_proj")

    fwd, fwd_lo, inv = _dft_tables(L)
    hp, hm = _hy_filters(L, hy_w1[0], hy_b1[0], hy_w2[0], hy_b2[0], hy_w3[0], hy_b3[0], hy_freq[0],
                         hy_wout[0], hyw)
    n_orders = hp.shape[0]
    spectra = _filter_spectra(hp.reshape(n_orders, 2, L // 2, hyw), hm.reshape(n_orders, 2, L // 2, hyw),
                              fwd, fwd_lo)
    z = _long_conv_gated(hyp, 0, hyp, hyw, hyw, fwd, inv, spectra, 0, hy_skip[0, 0], False)
    y_hy = _long_conv_gated(z, 0, hyp, 2 * hyw, hyw, fwd, inv, spectra, 1, hy_skip[0, 1], True)

    y_hg = _hgrn(proj.reshape(B, L, in_cols - off_hg), 0, hgw, lb_f, lb_b, hg_norm[0], st_f, st_b)

    merged = _merge(y_hy.reshape(B * L, hyw), y_hg.reshape(B * L, hgw), w_proj_hy[0], w_proj_hg[0],
                    proj, off_gate - off_hg)
    x1 = _mm_res(merged, w_out[0], x.reshape(B * L, D), g1, L, 1024, 512).reshape(B, L, D)

    xm2, aff = _norm_router(x1, norm2[0], sh2, sc2, w_router[0])
    aff_t = jnp.swapaxes(aff[:, :, :E], 1, 2)
    slot_row = _route(aff_t, cap)
    xg = _gather(slot_row, xm2, cap)
    y = _experts(xg.reshape(E, B * cap, D), w_gate[0], w_up[0], w_down[0]).reshape(E, B, cap, D)
    slot_tok = jnp.pad(jnp.swapaxes(slot_row.reshape(B, E, L), 1, 2), ((0, 0), (0, 0), (0, LANES - E)),
                       constant_values=-1)
    x2 = _combine(slot_tok, aff, y, x1, g2)

    return _rmsnorm(x2.reshape(B * L, D), norm_f).reshape(B, L, D)
```

```python
import functools
import math

import jax
import jax.numpy as jnp
from jax import lax
from jax.experimental import pallas as pl
from jax.experimental.pallas import tpu as pltpu

F32 = jnp.float32
BF16 = jnp.bfloat16

EPS = 1e-6
GRID_W = 64
HY_FAST_DECAY = 0.3
HY_SLOW_DECAY = 1.5
HY_TARGET = 1e-2
HG_HEAD = 128
HG_CHUNK = 128
HG_SUB = 32
HG_HEADS_PER_STEP = 4
EC_CAPACITY = 2
LANES = 128
VMEM_LIMIT_BYTES = 56 * 1024 * 1024


def _cparams(n_axes):
    return pltpu.CompilerParams(dimension_semantics=("arbitrary",) * n_axes,
                                vmem_limit_bytes=VMEM_LIMIT_BYTES)


def _tile(pref, dim):
    t = min(pref, dim)
    while dim % t:
        t //= 2
    return t


def _dot(a, b):
    return jnp.dot(a, b, preferred_element_type=F32)


def _dot_nt(a, b):
    return lax.dot_general(a, b, (((1,), (1,)), ((), ())), preferred_element_type=F32)


def _dot_tn(a, b):
    return lax.dot_general(a, b, (((0,), (0,)), ((), ())), preferred_element_type=F32)


def _split(x):
    hi = x.astype(BF16)
    return hi, (x - hi.astype(F32)).astype(BF16)


def _dot3(a, b):
    ah, al = _split(a)
    bh, bl = _split(b)
    return _dot(ah, bh) + _dot(al, bh) + _dot(ah, bl)


def _silu(x):
    return x * jax.nn.sigmoid(x)


def _ada_kernel(c_ref, w_ref, b_ref, o_ref):
    o_ref[...] = _dot3(_silu(c_ref[...]), w_ref[...]) + b_ref[...]


def _ada(cond, w, b):
    R, D = cond.shape
    N = w.shape[1]
    tn = _tile(512, N)
    return pl.pallas_call(
        _ada_kernel, grid=(N // tn,),
        in_specs=[pl.BlockSpec((R, D), lambda n: (0, 0)),
                  pl.BlockSpec((D, tn), lambda n: (0, n)),
                  pl.BlockSpec((1, tn), lambda n: (0, n))],
        out_specs=pl.BlockSpec((R, tn), lambda n: (0, n)),
        out_shape=jax.ShapeDtypeStruct((R, N), F32),
        compiler_params=_cparams(1), name="ada")(cond, w, b.reshape(1, N))


def _rms_mod(x, g, sh, sc):
    y = x * lax.rsqrt(jnp.mean(x * x, axis=-1, keepdims=True) + EPS) * g
    return y * (1.0 + sc) + sh


def _normmod_kernel(x_ref, g_ref, sh_ref, sc_ref, o_ref):
    o_ref[0] = _rms_mod(x_ref[0], g_ref[...], sh_ref[0], sc_ref[0]).astype(o_ref.dtype)


def _normmod(x, g, sh, sc):
    B, L, D = x.shape
    tm = _tile(256, L)
    row = pl.BlockSpec((1, tm, D), lambda b, m: (b, m, 0))
    par = pl.BlockSpec((1, 1, D), lambda b, m: (b, 0, 0))
    return pl.pallas_call(
        _normmod_kernel, grid=(B, L // tm),
        in_specs=[row, pl.BlockSpec((1, D), lambda b, m: (0, 0)), par, par],
        out_specs=row, out_shape=jax.ShapeDtypeStruct((B, L, D), BF16),
        compiler_params=_cparams(2), name="normmod")(x, g.reshape(1, D), sh.reshape(B, 1, D), sc.reshape(B, 1, D))


def _rmsnorm_kernel(x_ref, g_ref, o_ref):
    x = x_ref[...]
    o_ref[...] = x * lax.rsqrt(jnp.mean(x * x, axis=-1, keepdims=True) + EPS) * g_ref[...]


def _rmsnorm(x, g):
    M, D = x.shape
    tm = _tile(256, M)
    row = pl.BlockSpec((tm, D), lambda m: (m, 0))
    return pl.pallas_call(
        _rmsnorm_kernel, grid=(M // tm,),
        in_specs=[row, pl.BlockSpec((1, D), lambda m: (0, 0))],
        out_specs=row, out_shape=jax.ShapeDtypeStruct((M, D), F32),
        compiler_params=_cparams(1), name="final_norm")(x, g.reshape(1, D))


def _mm_kernel(x_ref, w_ref, o_ref, wbf_ref):
    @pl.when(pl.program_id(1) == 0)
    def _():
        wbf_ref[...] = w_ref[...].astype(BF16)
    o_ref[...] = _dot(x_ref[...], wbf_ref[...]).astype(o_ref.dtype)


def _mm(x, w, col0, ncols, tm, tn, out_dtype, name):
    M, K = x.shape
    tm, tn = _tile(tm, M), _tile(tn, ncols)
    assert col0 % tn == 0
    c0 = col0 // tn
    return pl.pallas_call(
        _mm_kernel, grid=(ncols // tn, M // tm),
        in_specs=[pl.BlockSpec((tm, K), lambda n, m: (m, 0)),
                  pl.BlockSpec((K, tn), lambda n, m: (0, c0 + n))],
        out_specs=pl.BlockSpec((tm, tn), lambda n, m: (m, n)),
        out_shape=jax.ShapeDtypeStruct((M, ncols), out_dtype),
        scratch_shapes=[pltpu.VMEM((K, tn), BF16)],
        compiler_params=_cparams(2), name=name)(x, w)


def _mm_res_kernel(x_ref, w_ref, r_ref, g_ref, o_ref, wbf_ref):
    @pl.when(pl.program_id(1) == 0)
    def _():
        wbf_ref[...] = w_ref[...].astype(BF16)
    o_ref[...] = r_ref[...] + g_ref[0] * _dot(x_ref[...], wbf_ref[...])


def _mm_res(x, w, res, gate, rows_per_batch, tm, tn):
    M, K = x.shape
    N = w.shape[1]
    tm, tn = _tile(tm, rows_per_batch), _tile(tn, N)
    mpb = rows_per_batch // tm
    B = gate.shape[0]
    return pl.pallas_call(
        _mm_res_kernel, grid=(N // tn, M // tm),
        in_specs=[pl.BlockSpec((tm, K), lambda n, m: (m, 0)),
                  pl.BlockSpec((K, tn), lambda n, m: (0, n)),
                  pl.BlockSpec((tm, tn), lambda n, m: (m, n)),
                  pl.BlockSpec((1, 1, tn), lambda n, m: (m // mpb, 0, n))],
        out_specs=pl.BlockSpec((tm, tn), lambda n, m: (m, n)),
        out_shape=jax.ShapeDtypeStruct((M, N), F32),
        scratch_shapes=[pltpu.VMEM((K, tn), BF16)],
        compiler_params=_cparams(2), name="out_proj")(x, w, res, gate.reshape(B, 1, N))


def _merge_kernel(yhy_ref, yhg_ref, why_ref, whg_ref, ghy_ref, ghg_ref, o_ref, why_s, whg_s):
    @pl.when(pl.program_id(1) == 0)
    def _():
        why_s[...] = why_ref[...].astype(BF16)
        whg_s[...] = whg_ref[...].astype(BF16)
    ts = _tile(256, o_ref.shape[0])
    blocks = [slice(r * ts, (r + 1) * ts) for r in range(o_ref.shape[0] // ts)]
    prods = [(_dot(yhy_ref[rows, :], why_s[...]), _dot(yhg_ref[rows, :], whg_s[...])) for rows in blocks]
    for rows, (a, b) in zip(blocks, prods):
        o_ref[rows, :] = (jax.nn.sigmoid(ghy_ref[rows, :].astype(F32)) * a
                          + jax.nn.sigmoid(ghg_ref[rows, :].astype(F32)) * b).astype(o_ref.dtype)


def _merge(yhy, yhg, why, whg, proj, off_gate):
    M, K1 = yhy.shape
    K2 = yhg.shape[1]
    D = why.shape[1]
    tm, tn = _tile(1024, M), _tile(512, D)
    assert off_gate % tn == 0
    g0, g1 = off_gate // tn, (off_gate + D) // tn
    return pl.pallas_call(
        _merge_kernel, grid=(D // tn, M // tm),
        in_specs=[pl.BlockSpec((tm, K1), lambda n, m: (m, 0)),
                  pl.BlockSpec((tm, K2), lambda n, m: (m, 0)),
                  pl.BlockSpec((K1, tn), lambda n, m: (0, n)),
                  pl.BlockSpec((K2, tn), lambda n, m: (0, n)),
                  pl.BlockSpec((tm, tn), lambda n, m: (m, g0 + n)),
                  pl.BlockSpec((tm, tn), lambda n, m: (m, g1 + n))],
        out_specs=pl.BlockSpec((tm, tn), lambda n, m: (m, n)),
        out_shape=jax.ShapeDtypeStruct((M, D), BF16),
        scratch_shapes=[pltpu.VMEM((K1, tn), BF16), pltpu.VMEM((K2, tn), BF16)],
        compiler_params=_cparams(2), name="merge")(yhy, yhg, why, whg, proj, proj)


def _mm_sconv_kernel(x_ref, w_ref, cw_ref, cb_ref, o_ref, wbf_ref, y_s):
    @pl.when(pl.program_id(1) == 0)
    def _():
        wbf_ref[...] = w_ref[...].astype(BF16)
    cw = cw_ref[...]
    tl = y_s.shape[2]
    xs = [_dot(x_ref[r * tl:(r + 1) * tl, :], wbf_ref[...]) for r in range(x_ref.shape[0] // tl)]
    for r, x in enumerate(xs):
        pos = lax.broadcasted_iota(jnp.int32, x.shape, 0) % GRID_W
        prev = jnp.where(pos != 0, pltpu.roll(x, 1, 0), 0.0)
        nxt = jnp.where(pos != GRID_W - 1, pltpu.roll(x, tl - 1, 0), 0.0)
        y = prev * cw[0:1] + x * cw[1:2] + nxt * cw[2:3] + cb_ref[...]
        half = slice(r * (tl // 2), (r + 1) * (tl // 2))
        for g in range(y_s.shape[1]):
            ln = slice(g * LANES, (g + 1) * LANES)
            y_s[r, g] = y[:, ln]
            o_ref[0, 0, half, ln] = y_s[r, g, pl.ds(0, tl // 2, stride=2), :].astype(o_ref.dtype)
            o_ref[0, 1, half, ln] = y_s[r, g, pl.ds(1, tl // 2, stride=2), :].astype(o_ref.dtype)


def _mm_short_conv(x, w, ncols, conv_w, conv_b, B, L):
    M, K = x.shape
    tm, tn = _tile(1024, L), _tile(512, ncols)
    ts = _tile(256, tm)
    assert ts % GRID_W == 0 and tn % LANES == 0
    mpb = L // tm
    return pl.pallas_call(
        _mm_sconv_kernel, grid=(ncols // tn, M // tm),
        in_specs=[pl.BlockSpec((tm, K), lambda n, m: (m, 0)),
                  pl.BlockSpec((K, tn), lambda n, m: (0, n)),
                  pl.BlockSpec((3, tn), lambda n, m: (0, n)),
                  pl.BlockSpec((1, tn), lambda n, m: (0, n))],
        out_specs=pl.BlockSpec((1, 2, tm // 2, tn), lambda n, m: (m // mpb, 0, m % mpb, n)),
        out_shape=jax.ShapeDtypeStruct((B, 2, L // 2, ncols), BF16),
        scratch_shapes=[pltpu.VMEM((K, tn), BF16),
                        pltpu.VMEM((tm // ts, tn // LANES, ts, LANES), F32)],
        compiler_params=_cparams(2), name="in_proj_hyena")(x, w, conv_w, conv_b.reshape(1, ncols))


def _dft_tables(L):
    lh, n = L // 2, 2 * L
    k = jnp.arange(lh, dtype=jnp.int32)
    alt = jnp.where(k % 2 == 0, 1.0, -1.0).astype(F32)
    ang_e = ((k[:, None] * (2 * k)[None, :]) % n).astype(F32) * (2.0 * math.pi / n)
    ang_o = ((k[:, None] * (2 * k + 1)[None, :]) % n).astype(F32) * (2.0 * math.pi / n)
    ce, co, se, so = jnp.cos(ang_e), jnp.cos(ang_o), -jnp.sin(ang_e), -jnp.sin(ang_o)
    fwd = jnp.stack([ce, co, se.at[0].set(alt), so.at[0].set(-alt)])
    inv = (2.0 / n) * jnp.stack([ce.T.at[:, 0].set(0.5), co.T.at[:, 0].set(0.5),
                                 se.T.at[:, 0].set(alt), so.T.at[:, 0].set(-alt)])
    fwd_hi, fwd_lo = _split(fwd)
    return fwd_hi, fwd_lo, inv.astype(BF16)


def _dftf_kernel(ue_ref, uo_ref, t_ref, krl_ref, krh_ref, kil_ref, kih_ref, ae_ref, ao_ref, be_ref, bo_ref):
    tm = t_ref.shape[1]
    ue, uo = ue_ref[0, 0], uo_ref[0, 0]
    p, q = _dot(t_ref[0], ue), _dot(t_ref[1], uo)
    r, t = _dot(t_ref[2], ue), _dot(t_ref[3], uo)
    first = (lax.broadcasted_iota(jnp.int32, p.shape, 0) + pl.program_id(1) * tm) == 0
    krl, krh, kil, kih = krl_ref[0], krh_ref[0], kil_ref[0], kih_ref[0]
    m_re, m_im, km_re, km_im = r[0:1], t[0:1], kil[0:1], kih[0:1]
    r, t = jnp.where(first, 0.0, r), jnp.where(first, 0.0, t)
    kil, kih = jnp.where(first, 0.0, kil), jnp.where(first, 0.0, kih)
    re_l, re_h, im_l, im_h = p + q, p - q, r + t, t - r
    yr_l, yi_l = re_l * krl - im_l * kil, re_l * kil + im_l * krl
    yr_h, yi_h = re_h * krh - im_h * kih, re_h * kih + im_h * krh
    ae_ref[0] = (yr_l + yr_h).astype(ae_ref.dtype)
    ao_ref[0] = (yr_l - yr_h).astype(ao_ref.dtype)
    be_ref[0] = jnp.where(first, m_re * km_re - m_im * km_im, yi_l - yi_h).astype(be_ref.dtype)
    bo_ref[0] = jnp.where(first, m_re * km_im + m_im * km_re, yi_l + yi_h).astype(bo_ref.dtype)


def _dfti_kernel(ae_ref, ao_ref, be_ref, bo_ref, g_ref, u_ref, gate_ref, skip_ref, o_ref, *y_s):
    skip = skip_ref[...]
    ye = _dot(g_ref[0], ae_ref[0]) + _dot(g_ref[2], be_ref[0])
    yo = _dot(g_ref[1], ao_ref[0]) + _dot(g_ref[3], bo_ref[0])
    oe = gate_ref[0, 0].astype(F32) * (ye + skip * u_ref[0, 0].astype(F32))
    oo = gate_ref[0, 1].astype(F32) * (yo + skip * u_ref[0, 1].astype(F32))
    if y_s:
        tm = oe.shape[0]
        for g in range(y_s[0].shape[0]):
            ln = slice(g * LANES, (g + 1) * LANES)
            y_s[0][g, pl.ds(0, tm, stride=2), :] = oe[:, ln]
            y_s[0][g, pl.ds(1, tm, stride=2), :] = oo[:, ln]
            o_ref[0, :, ln] = y_s[0][g].astype(o_ref.dtype)
    else:
        o_ref[0, 0] = oe.astype(o_ref.dtype)
        o_ref[0, 1] = oo.astype(o_ref.dtype)


def _long_conv_gated(u_arr, u_col0, gate_arr, gate_col0, C, fwd, inv, spectra, order, skip, natural_out):
    B, _, lh, _ = u_arr.shape
    tm, tn = _tile(512, lh), _tile(512, C)
    assert u_col0 % tn == 0 and gate_col0 % tn == 0
    uc, gc = u_col0 // tn, gate_col0 // tn
    grid = (C // tn, lh // tm, B)
    tab = pl.BlockSpec((4, tm, lh), lambda n, m, b: (0, m, 0))
    kspec = pl.BlockSpec((1, tm, tn), lambda n, m, b: (order, m, n))
    ys = pl.pallas_call(
        _dftf_kernel, grid=grid,
        in_specs=[pl.BlockSpec((1, 1, lh, tn), lambda n, m, b: (b, 0, 0, uc + n)),
                  pl.BlockSpec((1, 1, lh, tn), lambda n, m, b: (b, 1, 0, uc + n)),
                  tab, kspec, kspec, kspec, kspec],
        out_specs=[pl.BlockSpec((1, tm, tn), lambda n, m, b: (b, m, n))] * 4,
        out_shape=[jax.ShapeDtypeStruct((B, lh, C), BF16)] * 4,
        compiler_params=_cparams(3), name="dft_fwd")(u_arr, u_arr, fwd, *spectra)
    yspec = pl.BlockSpec((1, lh, tn), lambda n, m, b: (b, 0, n))
    if natural_out:
        out_spec = pl.BlockSpec((1, 2 * tm, tn), lambda n, m, b: (b, m, n))
        out_shape = jax.ShapeDtypeStruct((B, 2 * lh, C), BF16)
        scratch = [pltpu.VMEM((tn // LANES, 2 * tm, LANES), F32)]
    else:
        out_spec = pl.BlockSpec((1, 2, tm, tn), lambda n, m, b: (b, 0, m, n))
        out_shape = jax.ShapeDtypeStruct((B, 2, lh, C), BF16)
        scratch = []
    return pl.pallas_call(
        _dfti_kernel, grid=grid,
        in_specs=[yspec, yspec, yspec, yspec, tab,
                  pl.BlockSpec((1, 2, tm, tn), lambda n, m, b: (b, 0, m, uc + n)),
                  pl.BlockSpec((1, 2, tm, tn), lambda n, m, b: (b, 0, m, gc + n)),
                  pl.BlockSpec((1, tn), lambda n, m, b: (0, n))],
        out_specs=out_spec, out_shape=out_shape, scratch_shapes=scratch,
        compiler_params=_cparams(3), name="dft_inv")(*ys, inv, u_arr, gate_arr, skip.reshape(1, C))


def _hy_filter_kernel(z_ref, w1_ref, b1_ref, w2_ref, b2_ref, w3_ref, b3_ref, fr_ref, wf_ref, wb_ref,
                      t_ref, ad_ref, hp_ref, hm_ref, hh_s, hl_s):
    @pl.when((pl.program_id(0) == 0) & (pl.program_id(1) == 0))
    def _():
        fr = fr_ref[...]
        h = jnp.sin(fr * (_dot3(z_ref[...], w1_ref[...]) + b1_ref[...]))
        h = jnp.sin(fr * (_dot3(h, w2_ref[...]) + b2_ref[...]))
        h = jnp.sin(fr * (_dot3(h, w3_ref[...]) + b3_ref[...]))
        hh_s[...], hl_s[...] = _split(h)

    def dot3h(w):
        wh, wl = _split(w)
        return _dot(hh_s[...], wh) + _dot(hl_s[...], wh) + _dot(hh_s[...], wl)

    decay = jnp.exp(-t_ref[...] * ad_ref[...])
    hf = dot3h(wf_ref[...]) * decay
    hb = dot3h(wb_ref[...]) * decay
    row = lax.broadcasted_iota(jnp.int32, hb.shape, 0)
    hb = jnp.where(row == 0, 0.0, hb)
    hp_ref[0] = hf + hb
    hm_ref[0] = hf - hb


def _hy_filters(L, w1, b1, w2, b2, w3, b3, freq, wout, width):
    emb, ffn = w1.shape
    n_orders = wout.shape[1] // (2 * width)
    t = jnp.linspace(0.0, 1.0, L, dtype=F32)[:, None]
    bands = (emb - 1) // 2
    w = 2.0 * math.pi * jnp.arange(L, dtype=F32) / L
    f = jnp.linspace(1e-4, bands - 1, bands, dtype=F32)
    fw = w[:, None] * f[None, :]
    z = jnp.concatenate([t, jnp.cos(fw), -jnp.sin(fw)], axis=-1)
    z = jnp.concatenate([z[0::2], z[1::2]], axis=0)
    t = jnp.concatenate([t[0::2], t[1::2]], axis=0)
    deltas = jnp.linspace(math.log(HY_TARGET) / HY_SLOW_DECAY, math.log(HY_TARGET) / HY_FAST_DECAY,
                          width, dtype=F32)
    pe, pf = LANES - emb, LANES - ffn
    z = jnp.pad(z, ((0, 0), (0, pe)))
    w1p = jnp.pad(w1, ((0, pe), (0, pf)))
    w2p = jnp.pad(w2, ((0, pf), (0, pf)))
    w3p = jnp.pad(w3, ((0, pf), (0, pf)))
    woutp = jnp.pad(wout, ((0, pf), (0, 0)))
    vec = lambda a: jnp.pad(a, (0, pf)).reshape(1, LANES)
    tc = _tile(512, width)
    nt = width // tc
    full = lambda shp: pl.BlockSpec(shp, lambda o, n: (0, 0))
    out = pl.BlockSpec((1, L, tc), lambda o, n: (o, 0, n))
    shp = jax.ShapeDtypeStruct((n_orders, L, width), F32)
    return pl.pallas_call(
        _hy_filter_kernel, grid=(n_orders, nt),
        in_specs=[full((L, LANES)), full((LANES, LANES)), full((1, LANES)), full((LANES, LANES)), full((1, LANES)),
                  full((LANES, LANES)), full((1, LANES)), full((1, LANES)),
                  pl.BlockSpec((LANES, tc), lambda o, n: (0, (2 * o) * nt + n)),
                  pl.BlockSpec((LANES, tc), lambda o, n: (0, (2 * o + 1) * nt + n)),
                  full((L, 1)), pl.BlockSpec((1, tc), lambda o, n: (0, n))],
        out_specs=[out, out], out_shape=[shp, shp],
        scratch_shapes=[pltpu.VMEM((L, LANES), BF16)] * 2,
        compiler_params=_cparams(2), name="hy_filter")(
            z, w1p, vec(b1), w2p, vec(b2), w3p, vec(b3), vec(freq), woutp, woutp, t, jnp.abs(deltas).reshape(1, width))


def _spectrum_kernel(th_ref, tl_ref, hp_ref, hm_ref, krl_ref, krh_ref, kil_ref, kih_ref):
    tm = th_ref.shape[1]

    def dot3(i, x, rows=slice(None)):
        xh, xl = x
        return _dot(th_ref[i, rows], xh) + _dot(tl_ref[i, rows], xh) + _dot(th_ref[i, rows], xl)

    pe, po = _split(hp_ref[0, 0]), _split(hp_ref[0, 1])
    me, mo = _split(hm_ref[0, 0]), _split(hm_ref[0, 1])
    p, q, r, t = dot3(0, pe), dot3(1, po), dot3(2, me), dot3(3, mo)
    mid_re = dot3(2, pe, slice(0, 8))[0:1]
    first = (lax.broadcasted_iota(jnp.int32, p.shape, 0) + pl.program_id(2) * tm) == 0
    krl_ref[0] = p + q
    krh_ref[0] = p - q
    kil_ref[0] = jnp.where(first, mid_re, r + t)
    kih_ref[0] = jnp.where(first, t[0:1], t - r)


def _filter_spectra(hp, hm, fwd_hi, fwd_lo):
    n_orders, _, lh, width = hp.shape
    tm, tn = _tile(512, lh), _tile(512, width)
    tab = pl.BlockSpec((4, tm, lh), lambda o, n, m: (0, m, 0))
    hs = pl.BlockSpec((1, 2, lh, tn), lambda o, n, m: (o, 0, 0, n))
    out = pl.BlockSpec((1, tm, tn), lambda o, n, m: (o, m, n))
    shp = jax.ShapeDtypeStruct((n_orders, lh, width), F32)
    return pl.pallas_call(
        _spectrum_kernel, grid=(n_orders, width // tn, lh // tm),
        in_specs=[tab, tab, hs, hs], out_specs=[out] * 4, out_shape=[shp] * 4,
        compiler_params=_cparams(3), name="hy_spectrum")(fwd_hi, fwd_lo, hp, hm)


def _hg_gate(z, lb):
    f = lb + (1.0 - lb) * jax.nn.sigmoid(z)
    return jnp.log(f), 1.0 - f


def _chunk_cumsum(g, rev):
    n = g.shape[0]
    pos = lax.broadcasted_iota(jnp.int32, g.shape, 0) % HG_CHUNK
    d = 1
    while d < HG_CHUNK:
        if rev:
            g = g + jnp.where(pos < HG_CHUNK - d, pltpu.roll(g, n - d, 0), 0.0)
        else:
            g = g + jnp.where(pos >= d, pltpu.roll(g, d, 0), 0.0)
        d *= 2
    return g


def _row_at(b, step, rev):
    r = HG_CHUNK - 1 - step if rev else step
    return b[r:r + 1, :]


def _hg_kv(k, v, b, rev):
    bl = _row_at(b, HG_CHUNK - 1, rev)
    return _dot_tn(v.astype(BF16), (k * jnp.exp(bl - b)).astype(BF16))


def _hg_state(k, v, b, st, rev):
    return st * jnp.exp(_row_at(b, HG_CHUNK - 1, rev)) + _hg_kv(k, v, b, rev)


def _hg_scores(q, k, b, rev):
    C, SB = HG_CHUNK, HG_SUB
    nsb = C // SB

    def blk(x, i):
        r = nsb - 1 - i if rev else i
        return x[r * SB:(r + 1) * SB]

    def rows(parts):
        return jnp.concatenate(parts[::-1] if rev else parts, axis=0)

    mid_rows = [_row_at(b, i * SB + SB // 2, rev) for i in range(nsb)]
    end_rows = [_row_at(b, i * SB + SB - 1, rev) for i in range(nsb)]
    zero = jnp.zeros((SB, q.shape[1]), BF16)

    def only(i, piece):
        return rows([zero] * i + [piece.astype(BF16)] + [zero] * (nsb - 1 - i))

    qs = [only(i, blk(q, i) * jnp.exp(blk(b, i) - mid_rows[i])) for i in range(nsb)]
    ks = [only(i, blk(k, i) * jnp.exp(mid_rows[i] - blk(b, i))) for i in range(nsb)]
    for j in range(nsb - 1):
        qs.append(rows([zero] * (j + 1)
                       + [(blk(q, i) * jnp.exp(blk(b, i) - end_rows[j])).astype(BF16) for i in range(j + 1, nsb)]))
        ks.append(only(j, blk(k, j) * jnp.exp(end_rows[j] - blk(b, j))))
    tt = lax.broadcasted_iota(jnp.int32, (C, C), 0)
    ss = lax.broadcasted_iota(jnp.int32, (C, C), 1)
    causal = (ss >= tt) if rev else (ss <= tt)
    return jnp.where(causal, _dot_nt(jnp.concatenate(qs, axis=1), jnp.concatenate(ks, axis=1)), 0.0)


def _hgrn_kernel(q_ref, zf_ref, zb_ref, i_ref, g_ref, lbf_ref, lbb_ref, gn_ref, sf_ref, sb_ref, o_ref,
                 kf_s, kb_s, bf_s, cb_s, of_s, ob_s):
    C = HG_CHUNK
    nc = kf_s.shape[0] // C
    nh = kf_s.shape[1] // HG_HEAD
    gf, kf = _hg_gate(zf_ref[0].astype(F32), lbf_ref[0])
    kf_s[...] = kf
    bf_s[...] = _chunk_cumsum(gf, False)
    gb, kb = _hg_gate(zb_ref[0].astype(F32), lbb_ref[0])
    kb_s[...] = kb
    cb_s[...] = _chunk_cumsum(gb, True)
    dirs = ((kf_s, bf_s, of_s, False), (kb_s, cb_s, ob_s, True))

    def body(c, carry):
        work = []
        for h in range(nh):
            ln = slice(h * HG_HEAD, (h + 1) * HG_HEAD)
            for d, (k_s, b_s, o_s, rev) in enumerate(dirs):
                cc = nc - 1 - c if rev else c
                rows = pl.ds(pl.multiple_of(cc * C, C), C)
                q = _silu(q_ref[0, rows, ln].astype(F32)) * (HG_HEAD ** -0.5)
                v = i_ref[0, rows, ln].astype(F32)
                work.append((o_s, rows, ln, rev, q, k_s[rows, ln], v, b_s[rows, ln], carry[2 * h + d]))
        scores = [_hg_scores(q, k, b, rev) for (_, _, _, rev, q, k, _, b, _) in work]
        for (o_s, rows, ln, rev, q, k, v, b, st), a in zip(work, scores):
            o_s[rows, ln] = (_dot(a.astype(BF16), v.astype(BF16))
                             + _dot_nt((q * jnp.exp(b)).astype(BF16), st.astype(BF16)))
        return tuple(_hg_state(k, v, b, st, rev) for (_, _, _, rev, _, k, v, b, st) in work)

    init = []
    for h in range(nh):
        init += [sf_ref[0, h], sb_ref[0, h]]
    lax.fori_loop(0, nc, body, tuple(init))
    for h in range(nh):
        ln = slice(h * HG_HEAD, (h + 1) * HG_HEAD)
        o = of_s[:, ln] + ob_s[:, ln]
        o = o * lax.rsqrt(jnp.mean(o * o, axis=-1, keepdims=True) + EPS)
        o_ref[0, :, ln] = (o * gn_ref[0, :, ln] * _silu(g_ref[0, :, ln].astype(F32))).astype(o_ref.dtype)


def _hgrn(proj3, off_hg, width, lb_f, lb_b, gn, st_f, st_b):
    B, L, _ = proj3.shape
    H = width // HG_HEAD
    nh = _tile(HG_HEADS_PER_STEP, H)
    wb = nh * HG_HEAD
    assert L % HG_CHUNK == 0 and off_hg % wb == 0
    c0, hb = off_hg // wb, H // nh

    def col(i):
        return pl.BlockSpec((1, L, wb), lambda b, h: (b, 0, c0 + i * hb + h))

    par = pl.BlockSpec((1, 1, wb), lambda b, h: (0, 0, h))
    st = pl.BlockSpec((1, nh, HG_HEAD, HG_HEAD), lambda b, h: (b, h, 0, 0))
    return pl.pallas_call(
        _hgrn_kernel, grid=(B, hb),
        in_specs=[col(0), col(1), col(2), col(3), col(4), par, par, par, st, st],
        out_specs=pl.BlockSpec((1, L, wb), lambda b, h: (b, 0, h)),
        out_shape=jax.ShapeDtypeStruct((B, L, width), BF16),
        scratch_shapes=[pltpu.VMEM((L, wb), F32)] * 6,
        compiler_params=_cparams(2), name="hgrn")(
            proj3, proj3, proj3, proj3, proj3,
            lb_f.reshape(1, 1, width), lb_b.reshape(1, 1, width), gn.reshape(1, 1, width), st_f, st_b)


def _hgctx_kernel(zf_ref, zb_ref, i_ref, lbf_ref, lbb_ref, sf_ref, sb_ref):
    C = HG_CHUNK
    nc = zf_ref.shape[1] // C
    gf, kf = _hg_gate(zf_ref[0].astype(F32), lbf_ref[0])
    gb, kb = _hg_gate(zb_ref[0].astype(F32), lbb_ref[0])
    bf = _chunk_cumsum(gf, False)
    cb = _chunk_cumsum(gb, True)
    v = i_ref[0].astype(F32)
    stf = jnp.zeros((HG_HEAD, HG_HEAD), F32)
    stb = jnp.zeros((HG_HEAD, HG_HEAD), F32)
    for c in range(nc):
        sl = slice(c * C, (c + 1) * C)
        stf = _hg_state(kf[sl], v[sl], bf[sl], stf, False)
    for c in reversed(range(nc)):
        sl = slice(c * C, (c + 1) * C)
        stb = _hg_state(kb[sl], v[sl], cb[sl], stb, True)
    sf_ref[0, 0] = stf
    sb_ref[0, 0] = stb


def _hgrn_context_states(pc3, width, lb_f, lb_b):
    B, Lc, _ = pc3.shape
    H = width // HG_HEAD
    assert Lc % HG_CHUNK == 0

    def col(i):
        return pl.BlockSpec((1, Lc, HG_HEAD), lambda b, h: (b, 0, i * H + h))

    par = pl.BlockSpec((1, 1, HG_HEAD), lambda b, h: (h, 0, 0))
    st = pl.BlockSpec((1, 1, HG_HEAD, HG_HEAD), lambda b, h: (b, h, 0, 0))
    shp = jax.ShapeDtypeStruct((B, H, HG_HEAD, HG_HEAD), F32)
    return pl.pallas_call(
        _hgctx_kernel, grid=(B, H),
        in_specs=[col(0), col(1), col(2), par, par],
        out_specs=[st, st], out_shape=[shp, shp],
        compiler_params=_cparams(2), name="hgrn_ctx")(
            pc3, pc3, pc3, lb_f.reshape(H, 1, HG_HEAD), lb_b.reshape(H, 1, HG_HEAD))


def _norm_router_kernel(x_ref, g_ref, sh_ref, sc_ref, wr_ref, xm_ref, aff_ref, *, n_experts):
    xm = _rms_mod(x_ref[0], g_ref[...], sh_ref[0], sc_ref[0])
    xm_ref[0] = xm.astype(xm_ref.dtype)
    logits = _dot3(xm, wr_ref[...])
    lane = lax.broadcasted_iota(jnp.int32, logits.shape, 1)
    logits = jnp.where(lane < n_experts, logits, -1e30)
    e = jnp.exp(logits - jnp.max(logits, axis=-1, keepdims=True))
    aff_ref[0] = e / jnp.sum(e, axis=-1, keepdims=True)


def _norm_router(x, g, sh, sc, w_router):
    B, L, D = x.shape
    E = w_router.shape[1]
    tm = _tile(256, L)
    wr = jnp.pad(w_router, ((0, 0), (0, LANES - E)))
    row = pl.BlockSpec((1, tm, D), lambda b, m: (b, m, 0))
    par = pl.BlockSpec((1, 1, D), lambda b, m: (b, 0, 0))
    return pl.pallas_call(
        functools.partial(_norm_router_kernel, n_experts=E), grid=(B, L // tm),
        in_specs=[row, pl.BlockSpec((1, D), lambda b, m: (0, 0)), par, par,
                  pl.BlockSpec((D, LANES), lambda b, m: (0, 0))],
        out_specs=[row, pl.BlockSpec((1, tm, LANES), lambda b, m: (b, m, 0))],
        out_shape=[jax.ShapeDtypeStruct((B, L, D), BF16), jax.ShapeDtypeStruct((B, L, LANES), F32)],
        compiler_params=_cparams(2), name="norm_router")(
            x, g.reshape(1, D), sh.reshape(B, 1, D), sc.reshape(B, 1, D), wr)


def _route_kernel(a_ref, tri_ref, slot_ref, *, cap):
    bits = lax.bitcast_convert_type(a_ref[...], jnp.int32)

    def count(mask):
        return jnp.sum(jnp.where(mask, 1.0, 0.0), axis=-1, keepdims=True)

    def body(i, thr):
        cand = thr | jnp.left_shift(jnp.int32(1), 30 - i)
        return jnp.where(count(bits >= cand) >= cap, cand, thr)

    thr = lax.fori_loop(0, 31, body, jnp.zeros((bits.shape[0], 1), jnp.int32))
    above, tie = bits > thr, bits == thr
    tri = tri_ref[...]
    ties_before = _dot(jnp.where(tie, 1.0, 0.0).astype(BF16), tri)
    chosen = above | (tie & (ties_before < cap - count(above)))
    before = _dot(jnp.where(chosen, 1.0, 0.0).astype(BF16), tri)
    slot_ref[...] = jnp.where(chosen, before.astype(jnp.int32), -1)


def _route(aff_t, cap):
    B, E, L = aff_t.shape
    idx = jnp.arange(L, dtype=jnp.int32)
    tri = (idx[:, None] < idx[None, :]).astype(BF16)
    slot = pl.pallas_call(
        functools.partial(_route_kernel, cap=cap), grid=(1,),
        in_specs=[pl.BlockSpec((B * E, L), lambda i: (0, 0)), pl.BlockSpec((L, L), lambda i: (0, 0))],
        out_specs=pl.BlockSpec((B * E, L), lambda i: (0, 0)),
        out_shape=jax.ShapeDtypeStruct((B * E, L), jnp.int32),
        compiler_params=_cparams(1), name="route")(aff_t.reshape(B * E, L), tri)
    return slot.reshape(B, E, 1, L)


def _gather_kernel(slot_ref, xm_ref, o_ref):
    ne, _, cap, td = o_ref.shape
    L = slot_ref.shape[3]
    want = lax.broadcasted_iota(jnp.int32, (cap, L), 0)
    sel = jnp.concatenate([jnp.where(want == slot_ref[0, e], 1.0, 0.0).astype(BF16) for e in range(ne)], axis=0)
    o_ref[:, 0] = _dot(sel, xm_ref[0]).astype(o_ref.dtype).reshape(ne, cap, td)


def _gather(slot_row, xm, cap):
    B, E, _, L = slot_row.shape
    D = xm.shape[2]
    td, ne = _tile(1024, D), _tile(4, E)
    return pl.pallas_call(
        _gather_kernel, grid=(B, D // td, E // ne),
        in_specs=[pl.BlockSpec((1, ne, 1, L), lambda b, d, e: (b, e, 0, 0)),
                  pl.BlockSpec((1, L, td), lambda b, d, e: (b, 0, d))],
        out_specs=pl.BlockSpec((ne, 1, cap, td), lambda b, d, e: (e, b, 0, d)),
        out_shape=jax.ShapeDtypeStruct((E, B, cap, D), BF16),
        compiler_params=_cparams(3), name="moe_gather")(slot_row, xm)


def _expert_up_kernel(x_ref, wg_ref, wu_ref, h_ref):
    x = x_ref[0]
    a = _dot(x, wg_ref[0].astype(BF16))
    u = _dot(x, wu_ref[0].astype(BF16))
    h_ref[0] = (_silu(a) * u).astype(h_ref.dtype)


def _expert_down_kernel(h_ref, wd_ref, y_ref):
    y_ref[0] = _dot(h_ref[0], wd_ref[0].astype(BF16)).astype(y_ref.dtype)


def _experts(xg, w_gate, w_up, w_down):
    E, M, D = xg.shape
    FF = w_gate.shape[2]
    tf, td = _tile(256, FF), _tile(1024, D)
    h = pl.pallas_call(
        _expert_up_kernel, grid=(E, FF // tf),
        in_specs=[pl.BlockSpec((1, M, D), lambda e, n: (e, 0, 0)),
                  pl.BlockSpec((1, D, tf), lambda e, n: (e, 0, n)),
                  pl.BlockSpec((1, D, tf), lambda e, n: (e, 0, n))],
        out_specs=pl.BlockSpec((1, M, tf), lambda e, n: (e, 0, n)),
        out_shape=jax.ShapeDtypeStruct((E, M, FF), BF16),
        compiler_params=_cparams(2), name="expert_up")(xg, w_gate, w_up)
    return pl.pallas_call(
        _expert_down_kernel, grid=(E, D // td),
        in_specs=[pl.BlockSpec((1, M, FF), lambda e, n: (e, 0, 0)),
                  pl.BlockSpec((1, FF, td), lambda e, n: (e, 0, n))],
        out_specs=pl.BlockSpec((1, M, td), lambda e, n: (e, 0, n)),
        out_shape=jax.ShapeDtypeStruct((E, M, D), BF16),
        compiler_params=_cparams(2), name="expert_down")(h, w_down)


def _combine_kernel(sl_ref, af_ref, y_ref, x_ref, g_ref, o_ref, w_s):
    E, _, cap, td = y_ref.shape

    @pl.when(pl.program_id(2) == 0)
    def _():
        sl, af = sl_ref[0], af_ref[0]
        want = lax.broadcasted_iota(jnp.int32, (w_s.shape[0], cap), 1)
        for e in range(E):
            w_s[:, e * cap:(e + 1) * cap] = jnp.where(sl[:, e:e + 1] == want, af[:, e:e + 1], 0.0).astype(w_s.dtype)

    y = y_ref[:, 0].reshape(E * cap, td)
    o_ref[0] = x_ref[0] + g_ref[0] * _dot(w_s[...], y)


def _combine(slot_tok, aff, y, x, gate):
    B, L, _ = slot_tok.shape
    E, cap, D = y.shape[0], y.shape[2], y.shape[3]
    tm, td = _tile(512, L), _tile(1024, D)
    lanespec = pl.BlockSpec((1, tm, LANES), lambda b, m, d: (b, m, 0))
    xspec = pl.BlockSpec((1, tm, td), lambda b, m, d: (b, m, d))
    return pl.pallas_call(
        _combine_kernel, grid=(B, L // tm, D // td),
        in_specs=[lanespec, lanespec,
                  pl.BlockSpec((E, 1, cap, td), lambda b, m, d: (0, b, 0, d)),
                  xspec, pl.BlockSpec((1, 1, td), lambda b, m, d: (b, 0, d))],
        out_specs=xspec, out_shape=jax.ShapeDtypeStruct((B, L, D), F32),
        scratch_shapes=[pltpu.VMEM((tm, E * cap), BF16)],
        compiler_params=_cparams(3), name="moe_combine")(slot_tok, aff, y, x, gate.reshape(B, 1, D))


def kernel(x, c, ctx, c_ctx, w_ada, b_ada, norm1, norm2, w_in, hy_conv_w, hy_conv_b, hy_w1, hy_b1, hy_w2, hy_b2, hy_w3, hy_b3, hy_freq, hy_wout, hy_skip, hg_lb, hg_norm, w_proj_hy, w_proj_hg, w_out, w_router, w_gate, w_up, w_down, norm_f):
    assert w_ada.shape[0] == 1, "single-layer block"
    B, L, D = x.shape
    Lc = ctx.shape[1]
    hyw, hgw = hy_skip.shape[-1], hg_norm.shape[-1]
    H = hgw // HG_HEAD
    E = w_router.shape[-1]
    off_hg = 3 * hyw
    off_f = off_hg + hgw
    off_gate = off_hg + 5 * hgw
    in_cols = off_gate + 2 * D
    assert w_in.shape[-1] == in_cols and L % GRID_W == 0
    cap = EC_CAPACITY * L // E

    lb_all = jnp.cumsum(jax.nn.softmax(hg_lb.astype(F32), axis=0), axis=0)
    lb_f, lb_b = lb_all[0, 0], lb_all[0, 1]

    rows = -(-(B + 1) // 8) * 8
    cond = jnp.concatenate([c, c_ctx[None], jnp.zeros((rows - B - 1, D), F32)], axis=0)
    ada = _ada(cond, w_ada[0], b_ada[0])
    sh1, sc1, g1, sh2, sc2, g2 = [ada[:B, i * D:(i + 1) * D] for i in range(6)]
    csh1 = jnp.broadcast_to(ada[B:B + 1, 0:D], (B, D))
    csc1 = jnp.broadcast_to(ada[B:B + 1, D:2 * D], (B, D))

    xc_m = _normmod(ctx, norm1[0], csh1, csc1)
    pc = _mm(xc_m.reshape(B * Lc, D), w_in[0], off_f, 3 * hgw, 1024, 512, BF16, "ctx_proj")
    st_f, st_b = _hgrn_context_states(pc.reshape(B, Lc, 3 * hgw), hgw, lb_f, lb_b)

    x_m = _normmod(x, norm1[0], sh1, sc1).reshape(B * L, D)
    hyp = _mm_short_conv(x_m, w_in[0], off_hg, hy_conv_w[0], hy_conv_b[0], B, L)
    proj = _mm(x_m, w_in[0], off_hg, in_cols - off_hg, 1024, 512, BF16, "in_proj")

    fwd, fwd_lo, inv = _dft_tables(L)
    hp, hm = _hy_filters(L, hy_w1[0], hy_b1[0], hy_w2[0], hy_b2[0], hy_w3[0], hy_b3[0], hy_freq[0],
                         hy_wout[0], hyw)
    n_orders = hp.shape[0]
    spectra = _filter_spectra(hp.reshape(n_orders, 2, L // 2, hyw), hm.reshape(n_orders, 2, L // 2, hyw),
                              fwd, fwd_lo)
    z = _long_conv_gated(hyp, 0, hyp, hyw, hyw, fwd, inv, spectra, 0, hy_skip[0, 0], False)
    y_hy = _long_conv_gated(z, 0, hyp, 2 * hyw, hyw, fwd, inv, spectra, 1, hy_skip[0, 1], True)

    y_hg = _hgrn(proj.reshape(B, L, in_cols - off_hg), 0, hgw, lb_f, lb_b, hg_norm[0], st_f, st_b)

    merged = _merge(y_hy.reshape(B * L, hyw), y_hg.reshape(B * L, hgw), w_proj_hy[0], w_proj_hg[0],
                    proj, off_gate - off_hg)
    x1 = _mm_res(merged, w_out[0], x.reshape(B * L, D), g1, L, 1024, 512).reshape(B, L, D)

    xm2, aff = _norm_router(x1, norm2[0], sh2, sc2, w_router[0])
    aff_t = jnp.swapaxes(aff[:, :, :E], 1, 2)
    slot_row = _route(aff_t, cap)
    xg = _gather(slot_row, xm2, cap)
    y = _experts(xg.reshape(E, B * cap, D), w_gate[0], w_up[0], w_down[0]).reshape(E, B, cap, D)
    slot_tok = jnp.pad(jnp.swapaxes(slot_row.reshape(B, E, L), 1, 2), ((0, 0), (0, 0), (0, LANES - E)),
                       constant_values=-1)
    x2 = _combine(slot_tok, aff, y, x1, g2)

    return _rmsnorm(x2.reshape(B * L, D), norm_f).reshape(B, L, D)
```

```python
import functools
import math

import jax
import jax.numpy as jnp
from jax import lax
from jax.experimental import pallas as pl
from jax.experimental.pallas import tpu as pltpu

F32 = jnp.float32
BF16 = jnp.bfloat16

EPS = 1e-6
GRID_W = 64
HY_FAST_DECAY = 0.3
HY_SLOW_DECAY = 1.5
HY_TARGET = 1e-2
HG_HEAD = 128
HG_CHUNK = 128
HG_SUB = 32
HG_HEADS_PER_STEP = 4
EC_CAPACITY = 2
LANES = 128
VMEM_LIMIT_BYTES = 56 * 1024 * 1024


def _cparams(n_axes):
    return pltpu.CompilerParams(dimension_semantics=("arbitrary",) * n_axes,
                                vmem_limit_bytes=VMEM_LIMIT_BYTES)


def _tile(pref, dim):
    t = min(pref, dim)
    while dim % t:
        t //= 2
    return t


def _dot(a, b):
    return jnp.dot(a, b, preferred_element_type=F32)


def _dot_nt(a, b):
    return lax.dot_general(a, b, (((1,), (1,)), ((), ())), preferred_element_type=F32)


def _dot_tn(a, b):
    return lax.dot_general(a, b, (((0,), (0,)), ((), ())), preferred_element_type=F32)


def _split(x):
    hi = x.astype(BF16)
    return hi, (x - hi.astype(F32)).astype(BF16)


def _dot3(a, b):
    ah, al = _split(a)
    bh, bl = _split(b)
    return _dot(ah, bh) + _dot(al, bh) + _dot(ah, bl)


def _silu(x):
    return x * jax.nn.sigmoid(x)


def _ada_kernel(c_ref, w_ref, b_ref, o_ref):
    o_ref[...] = _dot3(_silu(c_ref[...]), w_ref[...]) + b_ref[...]


def _ada(cond, w, b):
    R, D = cond.shape
    N = w.shape[1]
    tn = _tile(512, N)
    return pl.pallas_call(
        _ada_kernel, grid=(N // tn,),
        in_specs=[pl.BlockSpec((R, D), lambda n: (0, 0)),
                  pl.BlockSpec((D, tn), lambda n: (0, n)),
                  pl.BlockSpec((1, tn), lambda n: (0, n))],
        out_specs=pl.BlockSpec((R, tn), lambda n: (0, n)),
        out_shape=jax.ShapeDtypeStruct((R, N), F32),
        compiler_params=_cparams(1), name="ada")(cond, w, b.reshape(1, N))


def _rms_mod(x, g, sh, sc):
    y = x * lax.rsqrt(jnp.mean(x * x, axis=-1, keepdims=True) + EPS) * g
    return y * (1.0 + sc) + sh


def _normmod_kernel(x_ref, g_ref, sh_ref, sc_ref, o_ref):
    o_ref[0] = _rms_mod(x_ref[0], g_ref[...], sh_ref[0], sc_ref[0]).astype(o_ref.dtype)


def _normmod(x, g, sh, sc):
    B, L, D = x.shape
    tm = _tile(256, L)
    row = pl.BlockSpec((1, tm, D), lambda b, m: (b, m, 0))
    par = pl.BlockSpec((1, 1, D), lambda b, m: (b, 0, 0))
    return pl.pallas_call(
        _normmod_kernel, grid=(B, L // tm),
        in_specs=[row, pl.BlockSpec((1, D), lambda b, m: (0, 0)), par, par],
        out_specs=row, out_shape=jax.ShapeDtypeStruct((B, L, D), BF16),
        compiler_params=_cparams(2), name="normmod")(x, g.reshape(1, D), sh.reshape(B, 1, D), sc.reshape(B, 1, D))


def _rmsnorm_kernel(x_ref, g_ref, o_ref):
    x = x_ref[...]
    o_ref[...] = x * lax.rsqrt(jnp.mean(x * x, axis=-1, keepdims=True) + EPS) * g_ref[...]


def _rmsnorm(x, g):
    M, D = x.shape
    tm = _tile(256, M)
    row = pl.BlockSpec((tm, D), lambda m: (m, 0))
    return pl.pallas_call(
        _rmsnorm_kernel, grid=(M // tm,),
        in_specs=[row, pl.BlockSpec((1, D), lambda m: (0, 0))],
        out_specs=row, out_shape=jax.ShapeDtypeStruct((M, D), F32),
        compiler_params=_cparams(1), name="final_norm")(x, g.reshape(1, D))


def _mm_kernel(x_ref, w_ref, o_ref, wbf_ref):
    @pl.when(pl.program_id(1) == 0)
    def _():
        wbf_ref[...] = w_ref[...].astype(BF16)
    o_ref[...] = _dot(x_ref[...], wbf_ref[...]).astype(o_ref.dtype)


def _mm(x, w, col0, ncols, tm, tn, out_dtype, name):
    M, K = x.shape
    tm, tn = _tile(tm, M), _tile(tn, ncols)
    assert col0 % tn == 0
    c0 = col0 // tn
    return pl.pallas_call(
        _mm_kernel, grid=(ncols // tn, M // tm),
        in_specs=[pl.BlockSpec((tm, K), lambda n, m: (m, 0)),
                  pl.BlockSpec((K, tn), lambda n, m: (0, c0 + n))],
        out_specs=pl.BlockSpec((tm, tn), lambda n, m: (m, n)),
        out_shape=jax.ShapeDtypeStruct((M, ncols), out_dtype),
        scratch_shapes=[pltpu.VMEM((K, tn), BF16)],
        compiler_params=_cparams(2), name=name)(x, w)


def _mm_res_kernel(x_ref, w_ref, r_ref, g_ref, o_ref, wbf_ref):
    @pl.when(pl.program_id(1) == 0)
    def _():
        wbf_ref[...] = w_ref[...].astype(BF16)
    o_ref[...] = r_ref[...] + g_ref[0] * _dot(x_ref[...], wbf_ref[...])


def _mm_res(x, w, res, gate, rows_per_batch, tm, tn):
    M, K = x.shape
    N = w.shape[1]
    tm, tn = _tile(tm, rows_per_batch), _tile(tn, N)
    mpb = rows_per_batch // tm
    B = gate.shape[0]
    return pl.pallas_call(
        _mm_res_kernel, grid=(N // tn, M // tm),
        in_specs=[pl.BlockSpec((tm, K), lambda n, m: (m, 0)),
                  pl.BlockSpec((K, tn), lambda n, m: (0, n)),
                  pl.BlockSpec((tm, tn), lambda n, m: (m, n)),
                  pl.BlockSpec((1, 1, tn), lambda n, m: (m // mpb, 0, n))],
        out_specs=pl.BlockSpec((tm, tn), lambda n, m: (m, n)),
        out_shape=jax.ShapeDtypeStruct((M, N), F32),
        scratch_shapes=[pltpu.VMEM((K, tn), BF16)],
        compiler_params=_cparams(2), name="out_proj")(x, w, res, gate.reshape(B, 1, N))


def _merge_kernel(yhy_ref, yhg_ref, why_ref, whg_ref, ghy_ref, ghg_ref, o_ref, why_s, whg_s):
    @pl.when(pl.program_id(1) == 0)
    def _():
        why_s[...] = why_ref[...].astype(BF16)
        whg_s[...] = whg_ref[...].astype(BF16)
    ts = _tile(256, o_ref.shape[0])
    blocks = [slice(r * ts, (r + 1) * ts) for r in range(o_ref.shape[0] // ts)]
    prods = [(_dot(yhy_ref[rows, :], why_s[...]), _dot(yhg_ref[rows, :], whg_s[...])) for rows in blocks]
    for rows, (a, b) in zip(blocks, prods):
        o_ref[rows, :] = (jax.nn.sigmoid(ghy_ref[rows, :].astype(F32)) * a
                          + jax.nn.sigmoid(ghg_ref[rows, :].astype(F32)) * b).astype(o_ref.dtype)


def _merge(yhy, yhg, why, whg, proj, off_gate):
    M, K1 = yhy.shape
    K2 = yhg.shape[1]
    D = why.shape[1]
    tm, tn = _tile(1024, M), _tile(512, D)
    assert off_gate % tn == 0
    g0, g1 = off_gate // tn, (off_gate + D) // tn
    return pl.pallas_call(
        _merge_kernel, grid=(D // tn, M // tm),
        in_specs=[pl.BlockSpec((tm, K1), lambda n, m: (m, 0)),
                  pl.BlockSpec((tm, K2), lambda n, m: (m, 0)),
                  pl.BlockSpec((K1, tn), lambda n, m: (0, n)),
                  pl.BlockSpec((K2, tn), lambda n, m: (0, n)),
                  pl.BlockSpec((tm, tn), lambda n, m: (m, g0 + n)),
                  pl.BlockSpec((tm, tn), lambda n, m: (m, g1 + n))],
        out_specs=pl.BlockSpec((tm, tn), lambda n, m: (m, n)),
        out_shape=jax.ShapeDtypeStruct((M, D), BF16),
        scratch_shapes=[pltpu.VMEM((K1, tn), BF16), pltpu.VMEM((K2, tn), BF16)],
        compiler_params=_cparams(2), name="merge")(yhy, yhg, why, whg, proj, proj)


def _mm_sconv_kernel(xe_ref, xo_ref, w_ref, cw_ref, cb_ref, o_ref, wbf_ref):
    @pl.when(pl.program_id(1) == 0)
    def _():
        wbf_ref[...] = w_ref[...].astype(BF16)
    pe = _dot(xe_ref[0], wbf_ref[...])
    po = _dot(xo_ref[0], wbf_ref[...])
    th = pe.shape[0]
    pos = lax.broadcasted_iota(jnp.int32, pe.shape, 0) % (GRID_W // 2)
    before_even = jnp.where(pos != 0, pltpu.roll(po, 1, 0), 0.0)
    after_odd = jnp.where(pos != GRID_W // 2 - 1, pltpu.roll(pe, th - 1, 0), 0.0)
    cw, cb = cw_ref[...], cb_ref[...]
    o_ref[0, 0] = (before_even * cw[0:1] + pe * cw[1:2] + po * cw[2:3] + cb).astype(o_ref.dtype)
    o_ref[0, 1] = (pe * cw[0:1] + po * cw[1:2] + after_odd * cw[2:3] + cb).astype(o_ref.dtype)


def _mm_short_conv(x, w, ncols, conv_w, conv_b):
    B, L, K = x.shape
    lh = L // 2
    th, tn = _tile(512, lh), _tile(512, ncols)
    assert th % (GRID_W // 2) == 0
    nt = lh // th
    x2 = x.reshape(B, lh, 2 * K)
    return pl.pallas_call(
        _mm_sconv_kernel, grid=(ncols // tn, B * nt),
        in_specs=[pl.BlockSpec((1, th, K), lambda n, m: (m // nt, m % nt, 0)),
                  pl.BlockSpec((1, th, K), lambda n, m: (m // nt, m % nt, 1)),
                  pl.BlockSpec((K, tn), lambda n, m: (0, n)),
                  pl.BlockSpec((3, tn), lambda n, m: (0, n)),
                  pl.BlockSpec((1, tn), lambda n, m: (0, n))],
        out_specs=pl.BlockSpec((1, 2, th, tn), lambda n, m: (m // nt, 0, m % nt, n)),
        out_shape=jax.ShapeDtypeStruct((B, 2, lh, ncols), BF16),
        scratch_shapes=[pltpu.VMEM((K, tn), BF16)],
        compiler_params=_cparams(2), name="in_proj_hyena")(x2, x2, w, conv_w, conv_b.reshape(1, ncols))


def _dft_tables(L):
    lh, n = L // 2, 2 * L
    k = jnp.arange(lh, dtype=jnp.int32)
    alt = jnp.where(k % 2 == 0, 1.0, -1.0).astype(F32)
    ang_e = ((k[:, None] * (2 * k)[None, :]) % n).astype(F32) * (2.0 * math.pi / n)
    ce, se = jnp.cos(ang_e), -jnp.sin(ang_e)
    ang_1 = k.astype(F32)[:, None] * (2.0 * math.pi / n)
    c1, s1 = jnp.cos(ang_1), jnp.sin(ang_1)
    co, so = ce * c1 + se * s1, se * c1 - ce * s1
    fwd = jnp.stack([ce, co, se.at[0].set(alt), so.at[0].set(-alt)])
    inv = (2.0 / n) * jnp.stack([ce.T.at[:, 0].set(0.5), co.T.at[:, 0].set(0.5),
                                 se.T.at[:, 0].set(alt), so.T.at[:, 0].set(-alt)])
    fwd_hi, fwd_lo = _split(fwd)
    return fwd_hi, fwd_lo, inv.astype(BF16)


def _dftf_kernel(ue_ref, uo_ref, t_ref, krl_ref, krh_ref, kil_ref, kih_ref, ae_ref, ao_ref, be_ref, bo_ref):
    tm = t_ref.shape[1]
    ue, uo = ue_ref[0, 0], uo_ref[0, 0]
    p, q = _dot(t_ref[0], ue), _dot(t_ref[1], uo)
    r, t = _dot(t_ref[2], ue), _dot(t_ref[3], uo)
    first = (lax.broadcasted_iota(jnp.int32, p.shape, 0) + pl.program_id(1) * tm) == 0
    krl, krh, kil, kih = krl_ref[0], krh_ref[0], kil_ref[0], kih_ref[0]
    m_re, m_im, km_re, km_im = r[0:1], t[0:1], kil[0:1], kih[0:1]
    r, t = jnp.where(first, 0.0, r), jnp.where(first, 0.0, t)
    kil, kih = jnp.where(first, 0.0, kil), jnp.where(first, 0.0, kih)
    re_l, re_h, im_l, im_h = p + q, p - q, r + t, t - r
    yr_l, yi_l = re_l * krl - im_l * kil, re_l * kil + im_l * krl
    yr_h, yi_h = re_h * krh - im_h * kih, re_h * kih + im_h * krh
    ae_ref[0] = (yr_l + yr_h).astype(ae_ref.dtype)
    ao_ref[0] = (yr_l - yr_h).astype(ao_ref.dtype)
    be_ref[0] = jnp.where(first, m_re * km_re - m_im * km_im, yi_l - yi_h).astype(be_ref.dtype)
    bo_ref[0] = jnp.where(first, m_re * km_im + m_im * km_re, yi_l + yi_h).astype(bo_ref.dtype)


def _dfti_kernel(ae_ref, ao_ref, be_ref, bo_ref, g_ref, u_ref, gate_ref, skip_ref, o_ref, *y_s):
    skip = skip_ref[...]
    ye = _dot(g_ref[0], ae_ref[0]) + _dot(g_ref[2], be_ref[0])
    yo = _dot(g_ref[1], ao_ref[0]) + _dot(g_ref[3], bo_ref[0])
    oe = gate_ref[0, 0].astype(F32) * (ye + skip * u_ref[0, 0].astype(F32))
    oo = gate_ref[0, 1].astype(F32) * (yo + skip * u_ref[0, 1].astype(F32))
    if y_s:
        tm = oe.shape[0]
        for g in range(y_s[0].shape[0]):
            ln = slice(g * LANES, (g + 1) * LANES)
            y_s[0][g, pl.ds(0, tm, stride=2), :] = oe[:, ln]
            y_s[0][g, pl.ds(1, tm, stride=2), :] = oo[:, ln]
            o_ref[0, :, ln] = y_s[0][g].astype(o_ref.dtype)
    else:
        o_ref[0, 0] = oe.astype(o_ref.dtype)
        o_ref[0, 1] = oo.astype(o_ref.dtype)


def _long_conv_gated(u_arr, u_col0, gate_arr, gate_col0, C, fwd, inv, spectra, order, skip, natural_out):
    B, _, lh, _ = u_arr.shape
    tm, tn = _tile(512, lh), _tile(512, C)
    assert u_col0 % tn == 0 and gate_col0 % tn == 0
    uc, gc = u_col0 // tn, gate_col0 // tn
    grid = (C // tn, lh // tm, B)
    tab = pl.BlockSpec((4, tm, lh), lambda n, m, b: (0, m, 0))
    kspec = pl.BlockSpec((1, tm, tn), lambda n, m, b: (order, m, n))
    ys = pl.pallas_call(
        _dftf_kernel, grid=grid,
        in_specs=[pl.BlockSpec((1, 1, lh, tn), lambda n, m, b: (b, 0, 0, uc + n)),
                  pl.BlockSpec((1, 1, lh, tn), lambda n, m, b: (b, 1, 0, uc + n)),
                  tab, kspec, kspec, kspec, kspec],
        out_specs=[pl.BlockSpec((1, tm, tn), lambda n, m, b: (b, m, n))] * 4,
        out_shape=[jax.ShapeDtypeStruct((B, lh, C), BF16)] * 4,
        compiler_params=_cparams(3), name="dft_fwd")(u_arr, u_arr, fwd, *spectra)
    yspec = pl.BlockSpec((1, lh, tn), lambda n, m, b: (b, 0, n))
    if natural_out:
        out_spec = pl.BlockSpec((1, 2 * tm, tn), lambda n, m, b: (b, m, n))
        out_shape = jax.ShapeDtypeStruct((B, 2 * lh, C), BF16)
        scratch = [pltpu.VMEM((tn // LANES, 2 * tm, LANES), F32)]
    else:
        out_spec = pl.BlockSpec((1, 2, tm, tn), lambda n, m, b: (b, 0, m, n))
        out_shape = jax.ShapeDtypeStruct((B, 2, lh, C), BF16)
        scratch = []
    return pl.pallas_call(
        _dfti_kernel, grid=grid,
        in_specs=[yspec, yspec, yspec, yspec, tab,
                  pl.BlockSpec((1, 2, tm, tn), lambda n, m, b: (b, 0, m, uc + n)),
                  pl.BlockSpec((1, 2, tm, tn), lambda n, m, b: (b, 0, m, gc + n)),
                  pl.BlockSpec((1, tn), lambda n, m, b: (0, n))],
        out_specs=out_spec, out_shape=out_shape, scratch_shapes=scratch,
        compiler_params=_cparams(3), name="dft_inv")(*ys, inv, u_arr, gate_arr, skip.reshape(1, C))


def _hy_filter_kernel(z_ref, w1_ref, b1_ref, w2_ref, b2_ref, w3_ref, b3_ref, fr_ref, wf_ref, wb_ref,
                      t_ref, ad_ref, hp_ref, hm_ref, hh_s, hl_s):
    @pl.when((pl.program_id(0) == 0) & (pl.program_id(1) == 0))
    def _():
        fr = fr_ref[...]
        h = jnp.sin(fr * (_dot3(z_ref[...], w1_ref[...]) + b1_ref[...]))
        h = jnp.sin(fr * (_dot3(h, w2_ref[...]) + b2_ref[...]))
        h = jnp.sin(fr * (_dot3(h, w3_ref[...]) + b3_ref[...]))
        hh_s[...], hl_s[...] = _split(h)

    def dot3h(w):
        wh, wl = _split(w)
        return _dot(hh_s[...], wh) + _dot(hl_s[...], wh) + _dot(hh_s[...], wl)

    decay = jnp.exp(-t_ref[...] * ad_ref[...])
    hf = dot3h(wf_ref[...]) * decay
    hb = dot3h(wb_ref[...]) * decay
    row = lax.broadcasted_iota(jnp.int32, hb.shape, 0)
    hb = jnp.where(row == 0, 0.0, hb)
    hp_ref[0] = hf + hb
    hm_ref[0] = hf - hb


def _hy_filters(L, w1, b1, w2, b2, w3, b3, freq, wout, width):
    emb, ffn = w1.shape
    n_orders = wout.shape[1] // (2 * width)
    t = jnp.linspace(0.0, 1.0, L, dtype=F32)[:, None]
    bands = (emb - 1) // 2
    w = 2.0 * math.pi * jnp.arange(L, dtype=F32) / L
    f = jnp.linspace(1e-4, bands - 1, bands, dtype=F32)
    fw = w[:, None] * f[None, :]
    z = jnp.concatenate([t, jnp.cos(fw), -jnp.sin(fw)], axis=-1)
    z = jnp.concatenate([z[0::2], z[1::2]], axis=0)
    t = jnp.concatenate([t[0::2], t[1::2]], axis=0)
    deltas = jnp.linspace(math.log(HY_TARGET) / HY_SLOW_DECAY, math.log(HY_TARGET) / HY_FAST_DECAY,
                          width, dtype=F32)
    pe, pf = LANES - emb, LANES - ffn
    z = jnp.pad(z, ((0, 0), (0, pe)))
    w1p = jnp.pad(w1, ((0, pe), (0, pf)))
    w2p = jnp.pad(w2, ((0, pf), (0, pf)))
    w3p = jnp.pad(w3, ((0, pf), (0, pf)))
    woutp = jnp.pad(wout, ((0, pf), (0, 0)))
    vec = lambda a: jnp.pad(a, (0, pf)).reshape(1, LANES)
    tc = _tile(512, width)
    nt = width // tc
    full = lambda shp: pl.BlockSpec(shp, lambda o, n: (0, 0))
    out = pl.BlockSpec((1, L, tc), lambda o, n: (o, 0, n))
    shp = jax.ShapeDtypeStruct((n_orders, L, width), F32)
    return pl.pallas_call(
        _hy_filter_kernel, grid=(n_orders, nt),
        in_specs=[full((L, LANES)), full((LANES, LANES)), full((1, LANES)), full((LANES, LANES)), full((1, LANES)),
                  full((LANES, LANES)), full((1, LANES)), full((1, LANES)),
                  pl.BlockSpec((LANES, tc), lambda o, n: (0, (2 * o) * nt + n)),
                  pl.BlockSpec((LANES, tc), lambda o, n: (0, (2 * o + 1) * nt + n)),
                  full((L, 1)), pl.BlockSpec((1, tc), lambda o, n: (0, n))],
        out_specs=[out, out], out_shape=[shp, shp],
        scratch_shapes=[pltpu.VMEM((L, LANES), BF16)] * 2,
        compiler_params=_cparams(2), name="hy_filter")(
            z, w1p, vec(b1), w2p, vec(b2), w3p, vec(b3), vec(freq), woutp, woutp, t, jnp.abs(deltas).reshape(1, width))


def _spectrum_kernel(th_ref, tl_ref, hp_ref, hm_ref, krl_ref, krh_ref, kil_ref, kih_ref):
    tm = th_ref.shape[1]

    def dot3(i, x, rows=slice(None)):
        xh, xl = x
        return _dot(th_ref[i, rows], xh) + _dot(tl_ref[i, rows], xh) + _dot(th_ref[i, rows], xl)

    pe, po = _split(hp_ref[0, 0]), _split(hp_ref[0, 1])
    me, mo = _split(hm_ref[0, 0]), _split(hm_ref[0, 1])
    p, q, r, t = dot3(0, pe), dot3(1, po), dot3(2, me), dot3(3, mo)
    mid_re = dot3(2, pe, slice(0, 8))[0:1]
    first = (lax.broadcasted_iota(jnp.int32, p.shape, 0) + pl.program_id(2) * tm) == 0
    krl_ref[0] = p + q
    krh_ref[0] = p - q
    kil_ref[0] = jnp.where(first, mid_re, r + t)
    kih_ref[0] = jnp.where(first, t[0:1], t - r)


def _filter_spectra(hp, hm, fwd_hi, fwd_lo):
    n_orders, _, lh, width = hp.shape
    tm, tn = _tile(512, lh), _tile(512, width)
    tab = pl.BlockSpec((4, tm, lh), lambda o, n, m: (0, m, 0))
    hs = pl.BlockSpec((1, 2, lh, tn), lambda o, n, m: (o, 0, 0, n))
    out = pl.BlockSpec((1, tm, tn), lambda o, n, m: (o, m, n))
    shp = jax.ShapeDtypeStruct((n_orders, lh, width), F32)
    return pl.pallas_call(
        _spectrum_kernel, grid=(n_orders, width // tn, lh // tm),
        in_specs=[tab, tab, hs, hs], out_specs=[out] * 4, out_shape=[shp] * 4,
        compiler_params=_cparams(3), name="hy_spectrum")(fwd_hi, fwd_lo, hp, hm)


def _hg_gate(z, lb):
    f = lb + (1.0 - lb) * jax.nn.sigmoid(z)
    return jnp.log(f), 1.0 - f


def _chunk_cumsum(g, rev):
    C = HG_CHUNK
    t = lax.broadcasted_iota(jnp.int32, (C, C), 0)
    s = lax.broadcasted_iota(jnp.int32, (C, C), 1)
    tri = jnp.where((s >= t) if rev else (s <= t), 1.0, 0.0).astype(BF16)
    hi, lo = _split(g)
    return jnp.concatenate([_dot(tri, hi[i * C:(i + 1) * C]) + _dot(tri, lo[i * C:(i + 1) * C])
                            for i in range(g.shape[0] // C)], axis=0)


def _row_at(b, step, rev):
    r = HG_CHUNK - 1 - step if rev else step
    return b[r:r + 1, :]


def _hg_kv(k, v, b, rev):
    bl = _row_at(b, HG_CHUNK - 1, rev)
    return _dot_tn(v.astype(BF16), (k * jnp.exp(bl - b)).astype(BF16))


def _hg_state(k, v, b, st, rev):
    return st * jnp.exp(_row_at(b, HG_CHUNK - 1, rev)) + _hg_kv(k, v, b, rev)


def _hg_scores(q, k, b, rev):
    C, SB = HG_CHUNK, HG_SUB
    nsb = C // SB

    def blk(x, i):
        r = nsb - 1 - i if rev else i
        return x[r * SB:(r + 1) * SB]

    def rows(parts):
        return jnp.concatenate(parts[::-1] if rev else parts, axis=0)

    mid_rows = [_row_at(b, i * SB + SB // 2, rev) for i in range(nsb)]
    end_rows = [_row_at(b, i * SB + SB - 1, rev) for i in range(nsb)]
    zero = jnp.zeros((SB, q.shape[1]), BF16)

    def only(i, piece):
        return rows([zero] * i + [piece.astype(BF16)] + [zero] * (nsb - 1 - i))

    qs = [only(i, blk(q, i) * jnp.exp(blk(b, i) - mid_rows[i])) for i in range(nsb)]
    ks = [only(i, blk(k, i) * jnp.exp(mid_rows[i] - blk(b, i))) for i in range(nsb)]
    for j in range(nsb - 1):
        qs.append(rows([zero] * (j + 1)
                       + [(blk(q, i) * jnp.exp(blk(b, i) - end_rows[j])).astype(BF16) for i in range(j + 1, nsb)]))
        ks.append(only(j, blk(k, j) * jnp.exp(end_rows[j] - blk(b, j))))
    tt = lax.broadcasted_iota(jnp.int32, (C, C), 0)
    ss = lax.broadcasted_iota(jnp.int32, (C, C), 1)
    causal = (ss >= tt) if rev else (ss <= tt)
    return jnp.where(causal, _dot_nt(jnp.concatenate(qs, axis=1), jnp.concatenate(ks, axis=1)), 0.0)


def _hgrn_kernel(q_ref, zf_ref, zb_ref, i_ref, g_ref, lbf_ref, lbb_ref, gn_ref, sf_ref, sb_ref, o_ref,
                 kf_s, kb_s, bf_s, cb_s, of_s, ob_s):
    C = HG_CHUNK
    nc = kf_s.shape[0] // C
    nh = kf_s.shape[1] // HG_HEAD
    gf, kf = _hg_gate(zf_ref[0].astype(F32), lbf_ref[0])
    kf_s[...] = kf
    bf_s[...] = _chunk_cumsum(gf, False)
    gb, kb = _hg_gate(zb_ref[0].astype(F32), lbb_ref[0])
    kb_s[...] = kb
    cb_s[...] = _chunk_cumsum(gb, True)
    dirs = ((kf_s, bf_s, of_s, False), (kb_s, cb_s, ob_s, True))

    def body(c, carry):
        work = []
        for h in range(nh):
            ln = slice(h * HG_HEAD, (h + 1) * HG_HEAD)
            for d, (k_s, b_s, o_s, rev) in enumerate(dirs):
                cc = nc - 1 - c if rev else c
                rows = pl.ds(pl.multiple_of(cc * C, C), C)
                q = _silu(q_ref[0, rows, ln].astype(F32)) * (HG_HEAD ** -0.5)
                v = i_ref[0, rows, ln].astype(F32)
                work.append((o_s, rows, ln, rev, q, k_s[rows, ln], v, b_s[rows, ln], carry[2 * h + d]))
        scores = [_hg_scores(q, k, b, rev) for (_, _, _, rev, q, k, _, b, _) in work]
        for (o_s, rows, ln, rev, q, k, v, b, st), a in zip(work, scores):
            o_s[rows, ln] = (_dot(a.astype(BF16), v.astype(BF16))
                             + _dot_nt((q * jnp.exp(b)).astype(BF16), st.astype(BF16)))
        return tuple(_hg_state(k, v, b, st, rev) for (_, _, _, rev, _, k, v, b, st) in work)

    init = []
    for h in range(nh):
        init += [sf_ref[0, h], sb_ref[0, h]]
    lax.fori_loop(0, nc, body, tuple(init))
    for h in range(nh):
        ln = slice(h * HG_HEAD, (h + 1) * HG_HEAD)
        o = of_s[:, ln] + ob_s[:, ln]
        o = o * lax.rsqrt(jnp.mean(o * o, axis=-1, keepdims=True) + EPS)
        o_ref[0, :, ln] = (o * gn_ref[0, :, ln] * _silu(g_ref[0, :, ln].astype(F32))).astype(o_ref.dtype)


def _hgrn(proj3, off_hg, width, lb_f, lb_b, gn, st_f, st_b):
    B, L, _ = proj3.shape
    H = width // HG_HEAD
    nh = _tile(HG_HEADS_PER_STEP, H)
    wb = nh * HG_HEAD
    assert L % HG_CHUNK == 0 and off_hg % wb == 0
    c0, hb = off_hg // wb, H // nh

    def col(i):
        return pl.BlockSpec((1, L, wb), lambda b, h: (b, 0, c0 + i * hb + h))

    par = pl.BlockSpec((1, 1, wb), lambda b, h: (0, 0, h))
    st = pl.BlockSpec((1, nh, HG_HEAD, HG_HEAD), lambda b, h: (b, h, 0, 0))
    return pl.pallas_call(
        _hgrn_kernel, grid=(B, hb),
        in_specs=[col(0), col(1), col(2), col(3), col(4), par, par, par, st, st],
        out_specs=pl.BlockSpec((1, L, wb), lambda b, h: (b, 0, h)),
        out_shape=jax.ShapeDtypeStruct((B, L, width), BF16),
        scratch_shapes=[pltpu.VMEM((L, wb), F32)] * 6,
        compiler_params=_cparams(2), name="hgrn")(
            proj3, proj3, proj3, proj3, proj3,
            lb_f.reshape(1, 1, width), lb_b.reshape(1, 1, width), gn.reshape(1, 1, width), st_f, st_b)


def _hgctx_kernel(zf_ref, zb_ref, i_ref, lbf_ref, lbb_ref, sf_ref, sb_ref):
    C = HG_CHUNK
    nc = zf_ref.shape[1] // C
    gf, kf = _hg_gate(zf_ref[0].astype(F32), lbf_ref[0])
    gb, kb = _hg_gate(zb_ref[0].astype(F32), lbb_ref[0])
    bf = _chunk_cumsum(gf, False)
    cb = _chunk_cumsum(gb, True)
    v = i_ref[0].astype(F32)
    stf = jnp.zeros((HG_HEAD, HG_HEAD), F32)
    stb = jnp.zeros((HG_HEAD, HG_HEAD), F32)
    for c in range(nc):
        sl = slice(c * C, (c + 1) * C)
        stf = _hg_state(kf[sl], v[sl], bf[sl], stf, False)
    for c in reversed(range(nc)):
        sl = slice(c * C, (c + 1) * C)
        stb = _hg_state(kb[sl], v[sl], cb[sl], stb, True)
    sf_ref[0, 0] = stf
    sb_ref[0, 0] = stb


def _hgrn_context_states(pc3, width, lb_f, lb_b):
    B, Lc, _ = pc3.shape
    H = width // HG_HEAD
    assert Lc % HG_CHUNK == 0

    def col(i):
        return pl.BlockSpec((1, Lc, HG_HEAD), lambda b, h: (b, 0, i * H + h))

    par = pl.BlockSpec((1, 1, HG_HEAD), lambda b, h: (h, 0, 0))
    st = pl.BlockSpec((1, 1, HG_HEAD, HG_HEAD), lambda b, h: (b, h, 0, 0))
    shp = jax.ShapeDtypeStruct((B, H, HG_HEAD, HG_HEAD), F32)
    return pl.pallas_call(
        _hgctx_kernel, grid=(B, H),
        in_specs=[col(0), col(1), col(2), par, par],
        out_specs=[st, st], out_shape=[shp, shp],
        compiler_params=_cparams(2), name="hgrn_ctx")(
            pc3, pc3, pc3, lb_f.reshape(H, 1, HG_HEAD), lb_b.reshape(H, 1, HG_HEAD))


def _norm_router_kernel(x_ref, g_ref, sh_ref, sc_ref, wr_ref, xm_ref, aff_ref, *, n_experts):
    xm = _rms_mod(x_ref[0], g_ref[...], sh_ref[0], sc_ref[0])
    xm_ref[0] = xm.astype(xm_ref.dtype)
    logits = _dot3(xm, wr_ref[...])
    lane = lax.broadcasted_iota(jnp.int32, logits.shape, 1)
    logits = jnp.where(lane < n_experts, logits, -1e30)
    e = jnp.exp(logits - jnp.max(logits, axis=-1, keepdims=True))
    aff_ref[0] = e / jnp.sum(e, axis=-1, keepdims=True)


def _norm_router(x, g, sh, sc, w_router):
    B, L, D = x.shape
    E = w_router.shape[1]
    tm = _tile(256, L)
    wr = jnp.pad(w_router, ((0, 0), (0, LANES - E)))
    row = pl.BlockSpec((1, tm, D), lambda b, m: (b, m, 0))
    par = pl.BlockSpec((1, 1, D), lambda b, m: (b, 0, 0))
    return pl.pallas_call(
        functools.partial(_norm_router_kernel, n_experts=E), grid=(B, L // tm),
        in_specs=[row, pl.BlockSpec((1, D), lambda b, m: (0, 0)), par, par,
                  pl.BlockSpec((D, LANES), lambda b, m: (0, 0))],
        out_specs=[row, pl.BlockSpec((1, tm, LANES), lambda b, m: (b, m, 0))],
        out_shape=[jax.ShapeDtypeStruct((B, L, D), BF16), jax.ShapeDtypeStruct((B, L, LANES), F32)],
        compiler_params=_cparams(2), name="norm_router")(
            x, g.reshape(1, D), sh.reshape(B, 1, D), sc.reshape(B, 1, D), wr)


def _route_kernel(a_ref, tri_ref, slot_ref, *, cap):
    bits = lax.bitcast_convert_type(a_ref[...], jnp.int32)

    def count(mask):
        return jnp.sum(jnp.where(mask, 1.0, 0.0), axis=-1, keepdims=True)

    def body(i, thr):
        cand = thr | jnp.left_shift(jnp.int32(1), 30 - i)
        return jnp.where(count(bits >= cand) >= cap, cand, thr)

    thr = lax.fori_loop(0, 31, body, jnp.zeros((bits.shape[0], 1), jnp.int32))
    above, tie = bits > thr, bits == thr
    tri = tri_ref[...]
    ties_before = _dot(jnp.where(tie, 1.0, 0.0).astype(BF16), tri)
    chosen = above | (tie & (ties_before < cap - count(above)))
    before = _dot(jnp.where(chosen, 1.0, 0.0).astype(BF16), tri)
    slot_ref[...] = jnp.where(chosen, before.astype(jnp.int32), -1)


def _route(aff_t, cap):
    B, E, L = aff_t.shape
    idx = jnp.arange(L, dtype=jnp.int32)
    tri = (idx[:, None] < idx[None, :]).astype(BF16)
    slot = pl.pallas_call(
        functools.partial(_route_kernel, cap=cap), grid=(1,),
        in_specs=[pl.BlockSpec((B * E, L), lambda i: (0, 0)), pl.BlockSpec((L, L), lambda i: (0, 0))],
        out_specs=pl.BlockSpec((B * E, L), lambda i: (0, 0)),
        out_shape=jax.ShapeDtypeStruct((B * E, L), jnp.int32),
        compiler_params=_cparams(1), name="route")(aff_t.reshape(B * E, L), tri)
    return slot.reshape(B, E, 1, L)


def _gather_kernel(slot_ref, xm_ref, o_ref):
    ne, _, cap, td = o_ref.shape
    L = slot_ref.shape[3]
    want = lax.broadcasted_iota(jnp.int32, (cap, L), 0)
    sel = jnp.concatenate([jnp.where(want == slot_ref[0, e], 1.0, 0.0).astype(BF16) for e in range(ne)], axis=0)
    o_ref[:, 0] = _dot(sel, xm_ref[0]).astype(o_ref.dtype).reshape(ne, cap, td)


def _gather(slot_row, xm, cap):
    B, E, _, L = slot_row.shape
    D = xm.shape[2]
    td, ne = _tile(1024, D), _tile(4, E)
    return pl.pallas_call(
        _gather_kernel, grid=(B, D // td, E // ne),
        in_specs=[pl.BlockSpec((1, ne, 1, L), lambda b, d, e: (b, e, 0, 0)),
                  pl.BlockSpec((1, L, td), lambda b, d, e: (b, 0, d))],
        out_specs=pl.BlockSpec((ne, 1, cap, td), lambda b, d, e: (e, b, 0, d)),
        out_shape=jax.ShapeDtypeStruct((E, B, cap, D), BF16),
        compiler_params=_cparams(3), name="moe_gather")(slot_row, xm)


def _expert_up_kernel(x_ref, wg_ref, wu_ref, h_ref):
    x = x_ref[0]
    a = _dot(x, wg_ref[0].astype(BF16))
    u = _dot(x, wu_ref[0].astype(BF16))
    h_ref[0] = (_silu(a) * u).astype(h_ref.dtype)


def _expert_down_kernel(h_ref, wd_ref, y_ref):
    y_ref[0] = _dot(h_ref[0], wd_ref[0].astype(BF16)).astype(y_ref.dtype)


def _experts(xg, w_gate, w_up, w_down):
    E, M, D = xg.shape
    FF = w_gate.shape[2]
    tf, td = _tile(256, FF), _tile(1024, D)
    h = pl.pallas_call(
        _expert_up_kernel, grid=(E, FF // tf),
        in_specs=[pl.BlockSpec((1, M, D), lambda e, n: (e, 0, 0)),
                  pl.BlockSpec((1, D, tf), lambda e, n: (e, 0, n)),
                  pl.BlockSpec((1, D, tf), lambda e, n: (e, 0, n))],
        out_specs=pl.BlockSpec((1, M, tf), lambda e, n: (e, 0, n)),
        out_shape=jax.ShapeDtypeStruct((E, M, FF), BF16),
        compiler_params=_cparams(2), name="expert_up")(xg, w_gate, w_up)
    return pl.pallas_call(
        _expert_down_kernel, grid=(E, D // td),
        in_specs=[pl.BlockSpec((1, M, FF), lambda e, n: (e, 0, 0)),
                  pl.BlockSpec((1, FF, td), lambda e, n: (e, 0, n))],
        out_specs=pl.BlockSpec((1, M, td), lambda e, n: (e, 0, n)),
        out_shape=jax.ShapeDtypeStruct((E, M, D), BF16),
        compiler_params=_cparams(2), name="expert_down")(h, w_down)


def _combine_kernel(sl_ref, af_ref, y_ref, x_ref, g_ref, o_ref, w_s):
    E, _, cap, td = y_ref.shape

    @pl.when(pl.program_id(2) == 0)
    def _():
        sl, af = sl_ref[0], af_ref[0]
        want = lax.broadcasted_iota(jnp.int32, (w_s.shape[0], cap), 1)
        for e in range(E):
            w_s[:, e * cap:(e + 1) * cap] = jnp.where(sl[:, e:e + 1] == want, af[:, e:e + 1], 0.0).astype(w_s.dtype)

    y = y_ref[:, 0].reshape(E * cap, td)
    o_ref[0] = x_ref[0] + g_ref[0] * _dot(w_s[...], y)


def _combine(slot_tok, aff, y, x, gate):
    B, L, _ = slot_tok.shape
    E, cap, D = y.shape[0], y.shape[2], y.shape[3]
    tm, td = _tile(512, L), _tile(1024, D)
    lanespec = pl.BlockSpec((1, tm, LANES), lambda b, m, d: (b, m, 0))
    xspec = pl.BlockSpec((1, tm, td), lambda b, m, d: (b, m, d))
    return pl.pallas_call(
        _combine_kernel, grid=(B, L // tm, D // td),
        in_specs=[lanespec, lanespec,
                  pl.BlockSpec((E, 1, cap, td), lambda b, m, d: (0, b, 0, d)),
                  xspec, pl.BlockSpec((1, 1, td), lambda b, m, d: (b, 0, d))],
        out_specs=xspec, out_shape=jax.ShapeDtypeStruct((B, L, D), F32),
        scratch_shapes=[pltpu.VMEM((tm, E * cap), BF16)],
        compiler_params=_cparams(3), name="moe_combine")(slot_tok, aff, y, x, gate.reshape(B, 1, D))


def kernel(x, c, ctx, c_ctx, w_ada, b_ada, norm1, norm2, w_in, hy_conv_w, hy_conv_b, hy_w1, hy_b1, hy_w2, hy_b2, hy_w3, hy_b3, hy_freq, hy_wout, hy_skip, hg_lb, hg_norm, w_proj_hy, w_proj_hg, w_out, w_router, w_gate, w_up, w_down, norm_f):
    assert w_ada.shape[0] == 1, "single-layer block"
    B, L, D = x.shape
    Lc = ctx.shape[1]
    hyw, hgw = hy_skip.shape[-1], hg_norm.shape[-1]
    H = hgw // HG_HEAD
    E = w_router.shape[-1]
    off_hg = 3 * hyw
    off_f = off_hg + hgw
    off_gate = off_hg + 5 * hgw
    in_cols = off_gate + 2 * D
    assert w_in.shape[-1] == in_cols and L % GRID_W == 0
    cap = EC_CAPACITY * L // E

    lb_all = jnp.cumsum(jax.nn.softmax(hg_lb.astype(F32), axis=0), axis=0)
    lb_f, lb_b = lb_all[0, 0], lb_all[0, 1]

    rows = -(-(B + 1) // 8) * 8
    cond = jnp.concatenate([c, c_ctx[None], jnp.zeros((rows - B - 1, D), F32)], axis=0)
    ada = _ada(cond, w_ada[0], b_ada[0])
    sh1, sc1, g1, sh2, sc2, g2 = [ada[:B, i * D:(i + 1) * D] for i in range(6)]
    csh1 = jnp.broadcast_to(ada[B:B + 1, 0:D], (B, D))
    csc1 = jnp.broadcast_to(ada[B:B + 1, D:2 * D], (B, D))

    xc_m = _normmod(ctx, norm1[0], csh1, csc1)
    pc = _mm(xc_m.reshape(B * Lc, D), w_in[0], off_f, 3 * hgw, 1024, 512, BF16, "ctx_proj")
    st_f, st_b = _hgrn_context_states(pc.reshape(B, Lc, 3 * hgw), hgw, lb_f, lb_b)

    x_m = _normmod(x, norm1[0], sh1, sc1)
    hyp = _mm_short_conv(x_m, w_in[0], off_hg, hy_conv_w[0], hy_conv_b[0])
    proj = _mm(x_m.reshape(B * L, D), w_in[0], off_hg, in_cols - off_hg, 1024, 512, BF16,
               "in_proj")

    fwd, fwd_lo, inv = _dft_tables(L)
    hp, hm = _hy_filters(L, hy_w1[0], hy_b1[0], hy_w2[0], hy_b2[0], hy_w3[0], hy_b3[0], hy_freq[0],
                         hy_wout[0], hyw)
    n_orders = hp.shape[0]
    spectra = _filter_spectra(hp.reshape(n_orders, 2, L // 2, hyw), hm.reshape(n_orders, 2, L // 2, hyw),
                              fwd, fwd_lo)
    z = _long_conv_gated(hyp, 0, hyp, hyw, hyw, fwd, inv, spectra, 0, hy_skip[0, 0], False)
    y_hy = _long_conv_gated(z, 0, hyp, 2 * hyw, hyw, fwd, inv, spectra, 1, hy_skip[0, 1], True)

    y_hg = _hgrn(proj.reshape(B, L, in_cols - off_hg), 0, hgw, lb_f, lb_b, hg_norm[0], st_f, st_b)

    merged = _merge(y_hy.reshape(B * L, hyw), y_hg.reshape(B * L, hgw), w_proj_hy[0], w_proj_hg[0],
                    proj, off_gate - off_hg)
    x1 = _mm_res(merged, w_out[0], x.reshape(B * L, D), g1, L, 1024, 512).reshape(B, L, D)

    xm2, aff = _norm_router(x1, norm2[0], sh2, sc2, w_router[0])
    aff_t = jnp.swapaxes(aff[:, :, :E], 1, 2)
    slot_row = _route(aff_t, cap)
    xg = _gather(slot_row, xm2, cap)
    y = _experts(xg.reshape(E, B * cap, D), w_gate[0], w_up[0], w_down[0]).reshape(E, B, cap, D)
    slot_tok = jnp.pad(jnp.swapaxes(slot_row.reshape(B, E, L), 1, 2), ((0, 0), (0, 0), (0, LANES - E)),
                       constant_values=-1)
    x2 = _combine(slot_tok, aff, y, x1, g2)

    return _rmsnorm(x2.reshape(B * L, D), norm_f).reshape(B, L, D)
```

```python
import functools
import math

import jax
import jax.numpy as jnp
from jax import lax
from jax.experimental import pallas as pl
from jax.experimental.pallas import tpu as pltpu

F32 = jnp.float32
BF16 = jnp.bfloat16

EPS = 1e-6
GRID_W = 64
HY_FAST_DECAY = 0.3
HY_SLOW_DECAY = 1.5
HY_TARGET = 1e-2
HG_HEAD = 128
HG_CHUNK = 128
HG_SUB = 32
HG_HEADS_PER_STEP = 4
EC_CAPACITY = 2
LANES = 128
VMEM_LIMIT_BYTES = 56 * 1024 * 1024


def _cparams(n_axes):
    return pltpu.CompilerParams(dimension_semantics=("arbitrary",) * n_axes,
                                vmem_limit_bytes=VMEM_LIMIT_BYTES)


def _tile(pref, dim):
    t = min(pref, dim)
    while dim % t:
        t //= 2
    return t


def _dot(a, b):
    return jnp.dot(a, b, preferred_element_type=F32)


def _dot_nt(a, b):
    return lax.dot_general(a, b, (((1,), (1,)), ((), ())), preferred_element_type=F32)


def _dot_tn(a, b):
    return lax.dot_general(a, b, (((0,), (0,)), ((), ())), preferred_element_type=F32)


def _split(x):
    hi = x.astype(BF16)
    return hi, (x - hi.astype(F32)).astype(BF16)


def _dot3(a, b):
    ah, al = _split(a)
    bh, bl = _split(b)
    return _dot(ah, bh) + _dot(al, bh) + _dot(ah, bl)


def _silu(x):
    return x * jax.nn.sigmoid(x)


def _ada_kernel(c_ref, w_ref, b_ref, o_ref):
    o_ref[...] = _dot3(_silu(c_ref[...]), w_ref[...]) + b_ref[...]


def _ada(cond, w, b):
    R, D = cond.shape
    N = w.shape[1]
    tn = _tile(512, N)
    return pl.pallas_call(
        _ada_kernel, grid=(N // tn,),
        in_specs=[pl.BlockSpec((R, D), lambda n: (0, 0)),
                  pl.BlockSpec((D, tn), lambda n: (0, n)),
                  pl.BlockSpec((1, tn), lambda n: (0, n))],
        out_specs=pl.BlockSpec((R, tn), lambda n: (0, n)),
        out_shape=jax.ShapeDtypeStruct((R, N), F32),
        compiler_params=_cparams(1), name="ada")(cond, w, b.reshape(1, N))


def _rms_mod(x, g, sh, sc):
    y = x * lax.rsqrt(jnp.mean(x * x, axis=-1, keepdims=True) + EPS) * g
    return y * (1.0 + sc) + sh


def _normmod_kernel(x_ref, g_ref, sh_ref, sc_ref, o_ref):
    o_ref[0] = _rms_mod(x_ref[0], g_ref[...], sh_ref[0], sc_ref[0]).astype(o_ref.dtype)


def _normmod(x, g, sh, sc):
    B, L, D = x.shape
    tm = _tile(256, L)
    row = pl.BlockSpec((1, tm, D), lambda b, m: (b, m, 0))
    par = pl.BlockSpec((1, 1, D), lambda b, m: (b, 0, 0))
    return pl.pallas_call(
        _normmod_kernel, grid=(B, L // tm),
        in_specs=[row, pl.BlockSpec((1, D), lambda b, m: (0, 0)), par, par],
        out_specs=row, out_shape=jax.ShapeDtypeStruct((B, L, D), BF16),
        compiler_params=_cparams(2), name="normmod")(x, g.reshape(1, D), sh.reshape(B, 1, D), sc.reshape(B, 1, D))


def _normmod_parity_kernel(x_ref, g_ref, sh_ref, sc_ref, o_ref, op_ref, y_s):
    y = _rms_mod(x_ref[0], g_ref[...], sh_ref[0], sc_ref[0])
    o_ref[0] = y.astype(o_ref.dtype)
    half = y.shape[0] // 2
    for g in range(y_s.shape[0]):
        ln = slice(g * LANES, (g + 1) * LANES)
        y_s[g] = y[:, ln]
        op_ref[0, 0, :, ln] = y_s[g, pl.ds(0, half, stride=2), :].astype(op_ref.dtype)
        op_ref[0, 1, :, ln] = y_s[g, pl.ds(1, half, stride=2), :].astype(op_ref.dtype)


def _normmod_parity(x, g, sh, sc):
    B, L, D = x.shape
    tm = _tile(256, L)
    row = pl.BlockSpec((1, tm, D), lambda b, m: (b, m, 0))
    par = pl.BlockSpec((1, 1, D), lambda b, m: (b, 0, 0))
    return pl.pallas_call(
        _normmod_parity_kernel, grid=(B, L // tm),
        in_specs=[row, pl.BlockSpec((1, D), lambda b, m: (0, 0)), par, par],
        out_specs=[row, pl.BlockSpec((1, 2, tm // 2, D), lambda b, m: (b, 0, m, 0))],
        out_shape=[jax.ShapeDtypeStruct((B, L, D), BF16), jax.ShapeDtypeStruct((B, 2, L // 2, D), BF16)],
        scratch_shapes=[pltpu.VMEM((D // LANES, tm, LANES), F32)],
        compiler_params=_cparams(2), name="normmod_parity")(
            x, g.reshape(1, D), sh.reshape(B, 1, D), sc.reshape(B, 1, D))


def _rmsnorm_kernel(x_ref, g_ref, o_ref):
    x = x_ref[...]
    o_ref[...] = x * lax.rsqrt(jnp.mean(x * x, axis=-1, keepdims=True) + EPS) * g_ref[...]


def _rmsnorm(x, g):
    M, D = x.shape
    tm = _tile(256, M)
    row = pl.BlockSpec((tm, D), lambda m: (m, 0))
    return pl.pallas_call(
        _rmsnorm_kernel, grid=(M // tm,),
        in_specs=[row, pl.BlockSpec((1, D), lambda m: (0, 0))],
        out_specs=row, out_shape=jax.ShapeDtypeStruct((M, D), F32),
        compiler_params=_cparams(1), name="final_norm")(x, g.reshape(1, D))


def _mm_kernel(x_ref, w_ref, o_ref, wbf_ref):
    @pl.when(pl.program_id(1) == 0)
    def _():
        wbf_ref[...] = w_ref[...].astype(BF16)
    o_ref[...] = _dot(x_ref[...], wbf_ref[...]).astype(o_ref.dtype)


def _mm(x, w, col0, ncols, tm, tn, out_dtype, name):
    M, K = x.shape
    tm, tn = _tile(tm, M), _tile(tn, ncols)
    assert col0 % tn == 0
    c0 = col0 // tn
    return pl.pallas_call(
        _mm_kernel, grid=(ncols // tn, M // tm),
        in_specs=[pl.BlockSpec((tm, K), lambda n, m: (m, 0)),
                  pl.BlockSpec((K, tn), lambda n, m: (0, c0 + n))],
        out_specs=pl.BlockSpec((tm, tn), lambda n, m: (m, n)),
        out_shape=jax.ShapeDtypeStruct((M, ncols), out_dtype),
        scratch_shapes=[pltpu.VMEM((K, tn), BF16)],
        compiler_params=_cparams(2), name=name)(x, w)


def _mm_res_kernel(x_ref, w_ref, r_ref, g_ref, o_ref, wbf_ref):
    @pl.when(pl.program_id(1) == 0)
    def _():
        wbf_ref[...] = w_ref[...].astype(BF16)
    o_ref[...] = r_ref[...] + g_ref[0] * _dot(x_ref[...], wbf_ref[...])


def _mm_res(x, w, res, gate, rows_per_batch, tm, tn):
    M, K = x.shape
    N = w.shape[1]
    tm, tn = _tile(tm, rows_per_batch), _tile(tn, N)
    mpb = rows_per_batch // tm
    B = gate.shape[0]
    return pl.pallas_call(
        _mm_res_kernel, grid=(N // tn, M // tm),
        in_specs=[pl.BlockSpec((tm, K), lambda n, m: (m, 0)),
                  pl.BlockSpec((K, tn), lambda n, m: (0, n)),
                  pl.BlockSpec((tm, tn), lambda n, m: (m, n)),
                  pl.BlockSpec((1, 1, tn), lambda n, m: (m // mpb, 0, n))],
        out_specs=pl.BlockSpec((tm, tn), lambda n, m: (m, n)),
        out_shape=jax.ShapeDtypeStruct((M, N), F32),
        scratch_shapes=[pltpu.VMEM((K, tn), BF16)],
        compiler_params=_cparams(2), name="out_proj")(x, w, res, gate.reshape(B, 1, N))


def _merge_kernel(yhy_ref, yhg_ref, why_ref, whg_ref, ghy_ref, ghg_ref, o_ref, why_s, whg_s):
    @pl.when(pl.program_id(1) == 0)
    def _():
        why_s[...] = why_ref[...].astype(BF16)
        whg_s[...] = whg_ref[...].astype(BF16)
    ts = _tile(256, o_ref.shape[0])
    blocks = [slice(r * ts, (r + 1) * ts) for r in range(o_ref.shape[0] // ts)]
    prods = [(_dot(yhy_ref[rows, :], why_s[...]), _dot(yhg_ref[rows, :], whg_s[...])) for rows in blocks]
    for rows, (a, b) in zip(blocks, prods):
        o_ref[rows, :] = (jax.nn.sigmoid(ghy_ref[rows, :].astype(F32)) * a
                          + jax.nn.sigmoid(ghg_ref[rows, :].astype(F32)) * b).astype(o_ref.dtype)


def _merge(yhy, yhg, why, whg, proj, off_gate):
    M, K1 = yhy.shape
    K2 = yhg.shape[1]
    D = why.shape[1]
    tm, tn = _tile(1024, M), _tile(512, D)
    assert off_gate % tn == 0
    g0, g1 = off_gate // tn, (off_gate + D) // tn
    return pl.pallas_call(
        _merge_kernel, grid=(D // tn, M // tm),
        in_specs=[pl.BlockSpec((tm, K1), lambda n, m: (m, 0)),
                  pl.BlockSpec((tm, K2), lambda n, m: (m, 0)),
                  pl.BlockSpec((K1, tn), lambda n, m: (0, n)),
                  pl.BlockSpec((K2, tn), lambda n, m: (0, n)),
                  pl.BlockSpec((tm, tn), lambda n, m: (m, g0 + n)),
                  pl.BlockSpec((tm, tn), lambda n, m: (m, g1 + n))],
        out_specs=pl.BlockSpec((tm, tn), lambda n, m: (m, n)),
        out_shape=jax.ShapeDtypeStruct((M, D), BF16),
        scratch_shapes=[pltpu.VMEM((K1, tn), BF16), pltpu.VMEM((K2, tn), BF16)],
        compiler_params=_cparams(2), name="merge")(yhy, yhg, why, whg, proj, proj)


def _mm_sconv_kernel(xe_ref, xo_ref, w_ref, cw_ref, cb_ref, o_ref, wbf_ref):
    @pl.when(pl.program_id(1) == 0)
    def _():
        wbf_ref[...] = w_ref[...].astype(BF16)
    pe = _dot(xe_ref[0, 0], wbf_ref[...])
    po = _dot(xo_ref[0, 0], wbf_ref[...])
    th = pe.shape[0]
    pos = lax.broadcasted_iota(jnp.int32, pe.shape, 0) % (GRID_W // 2)
    before_even = jnp.where(pos != 0, pltpu.roll(po, 1, 0), 0.0)
    after_odd = jnp.where(pos != GRID_W // 2 - 1, pltpu.roll(pe, th - 1, 0), 0.0)
    cw, cb = cw_ref[...], cb_ref[...]
    o_ref[0, 0] = (before_even * cw[0:1] + pe * cw[1:2] + po * cw[2:3] + cb).astype(o_ref.dtype)
    o_ref[0, 1] = (pe * cw[0:1] + po * cw[1:2] + after_odd * cw[2:3] + cb).astype(o_ref.dtype)


def _mm_short_conv(xp, w, ncols, conv_w, conv_b):
    B, _, lh, K = xp.shape
    th, tn = _tile(512, lh), _tile(512, ncols)
    assert th % (GRID_W // 2) == 0
    nt = lh // th
    return pl.pallas_call(
        _mm_sconv_kernel, grid=(ncols // tn, B * nt),
        in_specs=[pl.BlockSpec((1, 1, th, K), lambda n, m: (m // nt, 0, m % nt, 0)),
                  pl.BlockSpec((1, 1, th, K), lambda n, m: (m // nt, 1, m % nt, 0)),
                  pl.BlockSpec((K, tn), lambda n, m: (0, n)),
                  pl.BlockSpec((3, tn), lambda n, m: (0, n)),
                  pl.BlockSpec((1, tn), lambda n, m: (0, n))],
        out_specs=pl.BlockSpec((1, 2, th, tn), lambda n, m: (m // nt, 0, m % nt, n)),
        out_shape=jax.ShapeDtypeStruct((B, 2, lh, ncols), BF16),
        scratch_shapes=[pltpu.VMEM((K, tn), BF16)],
        compiler_params=_cparams(2), name="in_proj_hyena")(xp, xp, w, conv_w, conv_b.reshape(1, ncols))


def _dft_tables(L):
    lh, n = L // 2, 2 * L
    k = jnp.arange(lh, dtype=jnp.int32)
    alt = jnp.where(k % 2 == 0, 1.0, -1.0).astype(F32)
    ang_e = ((k[:, None] * (2 * k)[None, :]) % n).astype(F32) * (2.0 * math.pi / n)
    ce, se = jnp.cos(ang_e), -jnp.sin(ang_e)
    ang_1 = k.astype(F32)[:, None] * (2.0 * math.pi / n)
    c1, s1 = jnp.cos(ang_1), jnp.sin(ang_1)
    co, so = ce * c1 + se * s1, se * c1 - ce * s1
    fwd = jnp.stack([ce, co, se.at[0].set(alt), so.at[0].set(-alt)])
    inv = (2.0 / n) * jnp.stack([ce.T.at[:, 0].set(0.5), co.T.at[:, 0].set(0.5),
                                 se.T.at[:, 0].set(alt), so.T.at[:, 0].set(-alt)])
    fwd_hi, fwd_lo = _split(fwd)
    return fwd_hi, fwd_lo, inv.astype(BF16)


def _dftf_kernel(ue_ref, uo_ref, t_ref, krl_ref, krh_ref, kil_ref, kih_ref, ae_ref, ao_ref, be_ref, bo_ref):
    tm = t_ref.shape[1]
    ue, uo = ue_ref[0, 0], uo_ref[0, 0]
    p, q = _dot(t_ref[0], ue), _dot(t_ref[1], uo)
    r, t = _dot(t_ref[2], ue), _dot(t_ref[3], uo)
    first = (lax.broadcasted_iota(jnp.int32, p.shape, 0) + pl.program_id(1) * tm) == 0
    krl, krh, kil, kih = krl_ref[0], krh_ref[0], kil_ref[0], kih_ref[0]
    m_re, m_im, km_re, km_im = r[0:1], t[0:1], kil[0:1], kih[0:1]
    r, t = jnp.where(first, 0.0, r), jnp.where(first, 0.0, t)
    kil, kih = jnp.where(first, 0.0, kil), jnp.where(first, 0.0, kih)
    re_l, re_h, im_l, im_h = p + q, p - q, r + t, t - r
    yr_l, yi_l = re_l * krl - im_l * kil, re_l * kil + im_l * krl
    yr_h, yi_h = re_h * krh - im_h * kih, re_h * kih + im_h * krh
    ae_ref[0] = (yr_l + yr_h).astype(ae_ref.dtype)
    ao_ref[0] = (yr_l - yr_h).astype(ao_ref.dtype)
    be_ref[0] = jnp.where(first, m_re * km_re - m_im * km_im, yi_l - yi_h).astype(be_ref.dtype)
    bo_ref[0] = jnp.where(first, m_re * km_im + m_im * km_re, yi_l + yi_h).astype(bo_ref.dtype)


def _dfti_kernel(ae_ref, ao_ref, be_ref, bo_ref, g_ref, u_ref, gate_ref, skip_ref, o_ref, *y_s):
    skip = skip_ref[...]
    ye = _dot(g_ref[0], ae_ref[0]) + _dot(g_ref[2], be_ref[0])
    yo = _dot(g_ref[1], ao_ref[0]) + _dot(g_ref[3], bo_ref[0])
    oe = gate_ref[0, 0].astype(F32) * (ye + skip * u_ref[0, 0].astype(F32))
    oo = gate_ref[0, 1].astype(F32) * (yo + skip * u_ref[0, 1].astype(F32))
    if y_s:
        tm = oe.shape[0]
        for g in range(y_s[0].shape[0]):
            ln = slice(g * LANES, (g + 1) * LANES)
            y_s[0][g, pl.ds(0, tm, stride=2), :] = oe[:, ln]
            y_s[0][g, pl.ds(1, tm, stride=2), :] = oo[:, ln]
            o_ref[0, :, ln] = y_s[0][g].astype(o_ref.dtype)
    else:
        o_ref[0, 0] = oe.astype(o_ref.dtype)
        o_ref[0, 1] = oo.astype(o_ref.dtype)


def _long_conv_gated(u_arr, u_col0, gate_arr, gate_col0, C, fwd, inv, spectra, order, skip, natural_out):
    B, _, lh, _ = u_arr.shape
    tm, tn = _tile(512, lh), _tile(512, C)
    assert u_col0 % tn == 0 and gate_col0 % tn == 0
    uc, gc = u_col0 // tn, gate_col0 // tn
    grid = (C // tn, lh // tm, B)
    tab = pl.BlockSpec((4, tm, lh), lambda n, m, b: (0, m, 0))
    kspec = pl.BlockSpec((1, tm, tn), lambda n, m, b: (order, m, n))
    ys = pl.pallas_call(
        _dftf_kernel, grid=grid,
        in_specs=[pl.BlockSpec((1, 1, lh, tn), lambda n, m, b: (b, 0, 0, uc + n)),
                  pl.BlockSpec((1, 1, lh, tn), lambda n, m, b: (b, 1, 0, uc + n)),
                  tab, kspec, kspec, kspec, kspec],
        out_specs=[pl.BlockSpec((1, tm, tn), lambda n, m, b: (b, m, n))] * 4,
        out_shape=[jax.ShapeDtypeStruct((B, lh, C), BF16)] * 4,
        compiler_params=_cparams(3), name="dft_fwd")(u_arr, u_arr, fwd, *spectra)
    yspec = pl.BlockSpec((1, lh, tn), lambda n, m, b: (b, 0, n))
    if natural_out:
        out_spec = pl.BlockSpec((1, 2 * tm, tn), lambda n, m, b: (b, m, n))
        out_shape = jax.ShapeDtypeStruct((B, 2 * lh, C), BF16)
        scratch = [pltpu.VMEM((tn // LANES, 2 * tm, LANES), F32)]
    else:
        out_spec = pl.BlockSpec((1, 2, tm, tn), lambda n, m, b: (b, 0, m, n))
        out_shape = jax.ShapeDtypeStruct((B, 2, lh, C), BF16)
        scratch = []
    return pl.pallas_call(
        _dfti_kernel, grid=grid,
        in_specs=[yspec, yspec, yspec, yspec, tab,
                  pl.BlockSpec((1, 2, tm, tn), lambda n, m, b: (b, 0, m, uc + n)),
                  pl.BlockSpec((1, 2, tm, tn), lambda n, m, b: (b, 0, m, gc + n)),
                  pl.BlockSpec((1, tn), lambda n, m, b: (0, n))],
        out_specs=out_spec, out_shape=out_shape, scratch_shapes=scratch,
        compiler_params=_cparams(3), name="dft_inv")(*ys, inv, u_arr, gate_arr, skip.reshape(1, C))


def _hy_filter_kernel(z_ref, w1_ref, b1_ref, w2_ref, b2_ref, w3_ref, b3_ref, fr_ref, wf_ref, wb_ref,
                      t_ref, ad_ref, hp_ref, hm_ref, hh_s, hl_s):
    @pl.when((pl.program_id(0) == 0) & (pl.program_id(1) == 0))
    def _():
        fr = fr_ref[...]
        h = jnp.sin(fr * (_dot3(z_ref[...], w1_ref[...]) + b1_ref[...]))
        h = jnp.sin(fr * (_dot3(h, w2_ref[...]) + b2_ref[...]))
        h = jnp.sin(fr * (_dot3(h, w3_ref[...]) + b3_ref[...]))
        hh_s[...], hl_s[...] = _split(h)

    def dot3h(w):
        wh, wl = _split(w)
        return _dot(hh_s[...], wh) + _dot(hl_s[...], wh) + _dot(hh_s[...], wl)

    decay = jnp.exp(-t_ref[...] * ad_ref[...])
    hf = dot3h(wf_ref[...]) * decay
    hb = dot3h(wb_ref[...]) * decay
    row = lax.broadcasted_iota(jnp.int32, hb.shape, 0)
    hb = jnp.where(row == 0, 0.0, hb)
    hp_ref[0] = hf + hb
    hm_ref[0] = hf - hb


def _hy_filters(L, w1, b1, w2, b2, w3, b3, freq, wout, width):
    emb, ffn = w1.shape
    n_orders = wout.shape[1] // (2 * width)
    t = jnp.linspace(0.0, 1.0, L, dtype=F32)[:, None]
    bands = (emb - 1) // 2
    w = 2.0 * math.pi * jnp.arange(L, dtype=F32) / L
    f = jnp.linspace(1e-4, bands - 1, bands, dtype=F32)
    fw = w[:, None] * f[None, :]
    z = jnp.concatenate([t, jnp.cos(fw), -jnp.sin(fw)], axis=-1)
    z = jnp.concatenate([z[0::2], z[1::2]], axis=0)
    t = jnp.concatenate([t[0::2], t[1::2]], axis=0)
    deltas = jnp.linspace(math.log(HY_TARGET) / HY_SLOW_DECAY, math.log(HY_TARGET) / HY_FAST_DECAY,
                          width, dtype=F32)
    pe, pf = LANES - emb, LANES - ffn
    z = jnp.pad(z, ((0, 0), (0, pe)))
    w1p = jnp.pad(w1, ((0, pe), (0, pf)))
    w2p = jnp.pad(w2, ((0, pf), (0, pf)))
    w3p = jnp.pad(w3, ((0, pf), (0, pf)))
    woutp = jnp.pad(wout, ((0, pf), (0, 0)))
    vec = lambda a: jnp.pad(a, (0, pf)).reshape(1, LANES)
    tc = _tile(512, width)
    nt = width // tc
    full = lambda shp: pl.BlockSpec(shp, lambda o, n: (0, 0))
    out = pl.BlockSpec((1, L, tc), lambda o, n: (o, 0, n))
    shp = jax.ShapeDtypeStruct((n_orders, L, width), F32)
    return pl.pallas_call(
        _hy_filter_kernel, grid=(n_orders, nt),
        in_specs=[full((L, LANES)), full((LANES, LANES)), full((1, LANES)), full((LANES, LANES)), full((1, LANES)),
                  full((LANES, LANES)), full((1, LANES)), full((1, LANES)),
                  pl.BlockSpec((LANES, tc), lambda o, n: (0, (2 * o) * nt + n)),
                  pl.BlockSpec((LANES, tc), lambda o, n: (0, (2 * o + 1) * nt + n)),
                  full((L, 1)), pl.BlockSpec((1, tc), lambda o, n: (0, n))],
        out_specs=[out, out], out_shape=[shp, shp],
        scratch_shapes=[pltpu.VMEM((L, LANES), BF16)] * 2,
        compiler_params=_cparams(2), name="hy_filter")(
            z, w1p, vec(b1), w2p, vec(b2), w3p, vec(b3), vec(freq), woutp, woutp, t, jnp.abs(deltas).reshape(1, width))


def _spectrum_kernel(th_ref, tl_ref, hp_ref, hm_ref, krl_ref, krh_ref, kil_ref, kih_ref):
    tm = th_ref.shape[1]

    def dot3(i, x, rows=slice(None)):
        xh, xl = x
        return _dot(th_ref[i, rows], xh) + _dot(tl_ref[i, rows], xh) + _dot(th_ref[i, rows], xl)

    pe, po = _split(hp_ref[0, 0]), _split(hp_ref[0, 1])
    me, mo = _split(hm_ref[0, 0]), _split(hm_ref[0, 1])
    p, q, r, t = dot3(0, pe), dot3(1, po), dot3(2, me), dot3(3, mo)
    mid_re = dot3(2, pe, slice(0, 8))[0:1]
    first = (lax.broadcasted_iota(jnp.int32, p.shape, 0) + pl.program_id(2) * tm) == 0
    krl_ref[0] = p + q
    krh_ref[0] = p - q
    kil_ref[0] = jnp.where(first, mid_re, r + t)
    kih_ref[0] = jnp.where(first, t[0:1], t - r)


def _filter_spectra(hp, hm, fwd_hi, fwd_lo):
    n_orders, _, lh, width = hp.shape
    tm, tn = _tile(512, lh), _tile(512, width)
    tab = pl.BlockSpec((4, tm, lh), lambda o, n, m: (0, m, 0))
    hs = pl.BlockSpec((1, 2, lh, tn), lambda o, n, m: (o, 0, 0, n))
    out = pl.BlockSpec((1, tm, tn), lambda o, n, m: (o, m, n))
    shp = jax.ShapeDtypeStruct((n_orders, lh, width), F32)
    return pl.pallas_call(
        _spectrum_kernel, grid=(n_orders, width // tn, lh // tm),
        in_specs=[tab, tab, hs, hs], out_specs=[out] * 4, out_shape=[shp] * 4,
        compiler_params=_cparams(3), name="hy_spectrum")(fwd_hi, fwd_lo, hp, hm)


def _hg_gate(z, lb):
    f = lb + (1.0 - lb) * jax.nn.sigmoid(z)
    return jnp.log(f), 1.0 - f


def _chunk_cumsum(g, rev):
    C = HG_CHUNK
    t = lax.broadcasted_iota(jnp.int32, (C, C), 0)
    s = lax.broadcasted_iota(jnp.int32, (C, C), 1)
    tri = jnp.where((s >= t) if rev else (s <= t), 1.0, 0.0).astype(BF16)
    hi, lo = _split(g)
    return jnp.concatenate([_dot(tri, hi[i * C:(i + 1) * C]) + _dot(tri, lo[i * C:(i + 1) * C])
                            for i in range(g.shape[0] // C)], axis=0)


def _row_at(b, step, rev):
    r = HG_CHUNK - 1 - step if rev else step
    return b[r:r + 1, :]


def _hg_kv(k, v, b, rev):
    bl = _row_at(b, HG_CHUNK - 1, rev)
    return _dot_tn(v.astype(BF16), (k * jnp.exp(bl - b)).astype(BF16))


def _hg_state(k, v, b, st, rev):
    return st * jnp.exp(_row_at(b, HG_CHUNK - 1, rev)) + _hg_kv(k, v, b, rev)


def _hg_scores(q, k, b, rev):
    C, SB = HG_CHUNK, HG_SUB
    nsb = C // SB

    def blk(x, i):
        r = nsb - 1 - i if rev else i
        return x[r * SB:(r + 1) * SB]

    def rows(parts):
        return jnp.concatenate(parts[::-1] if rev else parts, axis=0)

    mid_rows = [_row_at(b, i * SB + SB // 2, rev) for i in range(nsb)]
    end_rows = [_row_at(b, i * SB + SB - 1, rev) for i in range(nsb)]
    zero = jnp.zeros((SB, q.shape[1]), BF16)

    def only(i, piece):
        return rows([zero] * i + [piece.astype(BF16)] + [zero] * (nsb - 1 - i))

    qs = [only(i, blk(q, i) * jnp.exp(blk(b, i) - mid_rows[i])) for i in range(nsb)]
    ks = [only(i, blk(k, i) * jnp.exp(mid_rows[i] - blk(b, i))) for i in range(nsb)]
    for j in range(nsb - 1):
        qs.append(rows([zero] * (j + 1)
                       + [(blk(q, i) * jnp.exp(blk(b, i) - end_rows[j])).astype(BF16) for i in range(j + 1, nsb)]))
        ks.append(only(j, blk(k, j) * jnp.exp(end_rows[j] - blk(b, j))))
    tt = lax.broadcasted_iota(jnp.int32, (C, C), 0)
    ss = lax.broadcasted_iota(jnp.int32, (C, C), 1)
    causal = (ss >= tt) if rev else (ss <= tt)
    return jnp.where(causal, _dot_nt(jnp.concatenate(qs, axis=1), jnp.concatenate(ks, axis=1)), 0.0)


def _hgrn_kernel(q_ref, zf_ref, zb_ref, i_ref, g_ref, lbf_ref, lbb_ref, gn_ref, sf_ref, sb_ref, o_ref,
                 kf_s, kb_s, bf_s, cb_s, of_s, ob_s):
    C = HG_CHUNK
    nc = kf_s.shape[0] // C
    nh = kf_s.shape[1] // HG_HEAD
    gf, kf = _hg_gate(zf_ref[0].astype(F32), lbf_ref[0])
    kf_s[...] = kf
    bf_s[...] = _chunk_cumsum(gf, False)
    gb, kb = _hg_gate(zb_ref[0].astype(F32), lbb_ref[0])
    kb_s[...] = kb
    cb_s[...] = _chunk_cumsum(gb, True)
    dirs = ((kf_s, bf_s, of_s, False), (kb_s, cb_s, ob_s, True))

    def body(c, carry):
        work = []
        for h in range(nh):
            ln = slice(h * HG_HEAD, (h + 1) * HG_HEAD)
            for d, (k_s, b_s, o_s, rev) in enumerate(dirs):
                cc = nc - 1 - c if rev else c
                rows = pl.ds(pl.multiple_of(cc * C, C), C)
                q = _silu(q_ref[0, rows, ln].astype(F32)) * (HG_HEAD ** -0.5)
                v = i_ref[0, rows, ln].astype(F32)
                work.append((o_s, rows, ln, rev, q, k_s[rows, ln], v, b_s[rows, ln], carry[2 * h + d]))
        scores = [_hg_scores(q, k, b, rev) for (_, _, _, rev, q, k, _, b, _) in work]
        for (o_s, rows, ln, rev, q, k, v, b, st), a in zip(work, scores):
            o_s[rows, ln] = (_dot(a.astype(BF16), v.astype(BF16))
                             + _dot_nt((q * jnp.exp(b)).astype(BF16), st.astype(BF16)))
        return tuple(_hg_state(k, v, b, st, rev) for (_, _, _, rev, _, k, v, b, st) in work)

    init = []
    for h in range(nh):
        init += [sf_ref[0, h], sb_ref[0, h]]
    lax.fori_loop(0, nc, body, tuple(init))
    for h in range(nh):
        ln = slice(h * HG_HEAD, (h + 1) * HG_HEAD)
        o = of_s[:, ln] + ob_s[:, ln]
        o = o * lax.rsqrt(jnp.mean(o * o, axis=-1, keepdims=True) + EPS)
        o_ref[0, :, ln] = (o * gn_ref[0, :, ln] * _silu(g_ref[0, :, ln].astype(F32))).astype(o_ref.dtype)


def _hgrn(proj3, off_hg, width, lb_f, lb_b, gn, st_f, st_b):
    B, L, _ = proj3.shape
    H = width // HG_HEAD
    nh = _tile(HG_HEADS_PER_STEP, H)
    wb = nh * HG_HEAD
    assert L % HG_CHUNK == 0 and off_hg % wb == 0
    c0, hb = off_hg // wb, H // nh

    def col(i):
        return pl.BlockSpec((1, L, wb), lambda b, h: (b, 0, c0 + i * hb + h))

    par = pl.BlockSpec((1, 1, wb), lambda b, h: (0, 0, h))
    st = pl.BlockSpec((1, nh, HG_HEAD, HG_HEAD), lambda b, h: (b, h, 0, 0))
    return pl.pallas_call(
        _hgrn_kernel, grid=(B, hb),
        in_specs=[col(0), col(1), col(2), col(3), col(4), par, par, par, st, st],
        out_specs=pl.BlockSpec((1, L, wb), lambda b, h: (b, 0, h)),
        out_shape=jax.ShapeDtypeStruct((B, L, width), BF16),
        scratch_shapes=[pltpu.VMEM((L, wb), F32)] * 6,
        compiler_params=_cparams(2), name="hgrn")(
            proj3, proj3, proj3, proj3, proj3,
            lb_f.reshape(1, 1, width), lb_b.reshape(1, 1, width), gn.reshape(1, 1, width), st_f, st_b)


def _hgctx_kernel(zf_ref, zb_ref, i_ref, lbf_ref, lbb_ref, sf_ref, sb_ref):
    C = HG_CHUNK
    nc = zf_ref.shape[1] // C
    gf, kf = _hg_gate(zf_ref[0].astype(F32), lbf_ref[0])
    gb, kb = _hg_gate(zb_ref[0].astype(F32), lbb_ref[0])
    bf = _chunk_cumsum(gf, False)
    cb = _chunk_cumsum(gb, True)
    v = i_ref[0].astype(F32)
    stf = jnp.zeros((HG_HEAD, HG_HEAD), F32)
    stb = jnp.zeros((HG_HEAD, HG_HEAD), F32)
    for c in range(nc):
        sl = slice(c * C, (c + 1) * C)
        stf = _hg_state(kf[sl], v[sl], bf[sl], stf, False)
    for c in reversed(range(nc)):
        sl = slice(c * C, (c + 1) * C)
        stb = _hg_state(kb[sl], v[sl], cb[sl], stb, True)
    sf_ref[0, 0] = stf
    sb_ref[0, 0] = stb


def _hgrn_context_states(pc3, width, lb_f, lb_b):
    B, Lc, _ = pc3.shape
    H = width // HG_HEAD
    assert Lc % HG_CHUNK == 0

    def col(i):
        return pl.BlockSpec((1, Lc, HG_HEAD), lambda b, h: (b, 0, i * H + h))

    par = pl.BlockSpec((1, 1, HG_HEAD), lambda b, h: (h, 0, 0))
    st = pl.BlockSpec((1, 1, HG_HEAD, HG_HEAD), lambda b, h: (b, h, 0, 0))
    shp = jax.ShapeDtypeStruct((B, H, HG_HEAD, HG_HEAD), F32)
    return pl.pallas_call(
        _hgctx_kernel, grid=(B, H),
        in_specs=[col(0), col(1), col(2), par, par],
        out_specs=[st, st], out_shape=[shp, shp],
        compiler_params=_cparams(2), name="hgrn_ctx")(
            pc3, pc3, pc3, lb_f.reshape(H, 1, HG_HEAD), lb_b.reshape(H, 1, HG_HEAD))


def _norm_router_kernel(x_ref, g_ref, sh_ref, sc_ref, wr_ref, xm_ref, aff_ref, *, n_experts):
    xm = _rms_mod(x_ref[0], g_ref[...], sh_ref[0], sc_ref[0])
    xm_ref[0] = xm.astype(xm_ref.dtype)
    logits = _dot3(xm, wr_ref[...])
    lane = lax.broadcasted_iota(jnp.int32, logits.shape, 1)
    logits = jnp.where(lane < n_experts, logits, -1e30)
    e = jnp.exp(logits - jnp.max(logits, axis=-1, keepdims=True))
    aff_ref[0] = e / jnp.sum(e, axis=-1, keepdims=True)


def _norm_router(x, g, sh, sc, w_router):
    B, L, D = x.shape
    E = w_router.shape[1]
    tm = _tile(256, L)
    wr = jnp.pad(w_router, ((0, 0), (0, LANES - E)))
    row = pl.BlockSpec((1, tm, D), lambda b, m: (b, m, 0))
    par = pl.BlockSpec((1, 1, D), lambda b, m: (b, 0, 0))
    return pl.pallas_call(
        functools.partial(_norm_router_kernel, n_experts=E), grid=(B, L // tm),
        in_specs=[row, pl.BlockSpec((1, D), lambda b, m: (0, 0)), par, par,
                  pl.BlockSpec((D, LANES), lambda b, m: (0, 0))],
        out_specs=[row, pl.BlockSpec((1, tm, LANES), lambda b, m: (b, m, 0))],
        out_shape=[jax.ShapeDtypeStruct((B, L, D), BF16), jax.ShapeDtypeStruct((B, L, LANES), F32)],
        compiler_params=_cparams(2), name="norm_router")(
            x, g.reshape(1, D), sh.reshape(B, 1, D), sc.reshape(B, 1, D), wr)


def _route_kernel(a_ref, tri_ref, slot_ref, *, cap):
    bits = lax.bitcast_convert_type(a_ref[...], jnp.int32)

    def count(mask):
        return jnp.sum(jnp.where(mask, 1.0, 0.0), axis=-1, keepdims=True)

    def body(i, thr):
        cand = thr | jnp.left_shift(jnp.int32(1), 30 - i)
        return jnp.where(count(bits >= cand) >= cap, cand, thr)

    thr = lax.fori_loop(0, 31, body, jnp.zeros((bits.shape[0], 1), jnp.int32))
    above, tie = bits > thr, bits == thr
    tri = tri_ref[...]
    ties_before = _dot(jnp.where(tie, 1.0, 0.0).astype(BF16), tri)
    chosen = above | (tie & (ties_before < cap - count(above)))
    before = _dot(jnp.where(chosen, 1.0, 0.0).astype(BF16), tri)
    slot_ref[...] = jnp.where(chosen, before.astype(jnp.int32), -1)


def _route(aff_t, cap):
    B, E, L = aff_t.shape
    idx = jnp.arange(L, dtype=jnp.int32)
    tri = (idx[:, None] < idx[None, :]).astype(BF16)
    slot = pl.pallas_call(
        functools.partial(_route_kernel, cap=cap), grid=(1,),
        in_specs=[pl.BlockSpec((B * E, L), lambda i: (0, 0)), pl.BlockSpec((L, L), lambda i: (0, 0))],
        out_specs=pl.BlockSpec((B * E, L), lambda i: (0, 0)),
        out_shape=jax.ShapeDtypeStruct((B * E, L), jnp.int32),
        compiler_params=_cparams(1), name="route")(aff_t.reshape(B * E, L), tri)
    return slot.reshape(B, E, 1, L)


def _gather_kernel(slot_ref, xm_ref, o_ref):
    ne, _, cap, td = o_ref.shape
    L = slot_ref.shape[3]
    want = lax.broadcasted_iota(jnp.int32, (cap, L), 0)
    sel = jnp.concatenate([jnp.where(want == slot_ref[0, e], 1.0, 0.0).astype(BF16) for e in range(ne)], axis=0)
    o_ref[:, 0] = _dot(sel, xm_ref[0]).astype(o_ref.dtype).reshape(ne, cap, td)


def _gather(slot_row, xm, cap):
    B, E, _, L = slot_row.shape
    D = xm.shape[2]
    td, ne = _tile(1024, D), _tile(4, E)
    return pl.pallas_call(
        _gather_kernel, grid=(B, D // td, E // ne),
        in_specs=[pl.BlockSpec((1, ne, 1, L), lambda b, d, e: (b, e, 0, 0)),
                  pl.BlockSpec((1, L, td), lambda b, d, e: (b, 0, d))],
        out_specs=pl.BlockSpec((ne, 1, cap, td), lambda b, d, e: (e, b, 0, d)),
        out_shape=jax.ShapeDtypeStruct((E, B, cap, D), BF16),
        compiler_params=_cparams(3), name="moe_gather")(slot_row, xm)


def _expert_up_kernel(x_ref, wg_ref, wu_ref, h_ref):
    x = x_ref[0]
    a = _dot(x, wg_ref[0].astype(BF16))
    u = _dot(x, wu_ref[0].astype(BF16))
    h_ref[0] = (_silu(a) * u).astype(h_ref.dtype)


def _expert_down_kernel(h_ref, wd_ref, y_ref):
    y_ref[0] = _dot(h_ref[0], wd_ref[0].astype(BF16)).astype(y_ref.dtype)


def _experts(xg, w_gate, w_up, w_down):
    E, M, D = xg.shape
    FF = w_gate.shape[2]
    tf, td = _tile(256, FF), _tile(1024, D)
    h = pl.pallas_call(
        _expert_up_kernel, grid=(E, FF // tf),
        in_specs=[pl.BlockSpec((1, M, D), lambda e, n: (e, 0, 0)),
                  pl.BlockSpec((1, D, tf), lambda e, n: (e, 0, n)),
                  pl.BlockSpec((1, D, tf), lambda e, n: (e, 0, n))],
        out_specs=pl.BlockSpec((1, M, tf), lambda e, n: (e, 0, n)),
        out_shape=jax.ShapeDtypeStruct((E, M, FF), BF16),
        compiler_params=_cparams(2), name="expert_up")(xg, w_gate, w_up)
    return pl.pallas_call(
        _expert_down_kernel, grid=(E, D // td),
        in_specs=[pl.BlockSpec((1, M, FF), lambda e, n: (e, 0, 0)),
                  pl.BlockSpec((1, FF, td), lambda e, n: (e, 0, n))],
        out_specs=pl.BlockSpec((1, M, td), lambda e, n: (e, 0, n)),
        out_shape=jax.ShapeDtypeStruct((E, M, D), BF16),
        compiler_params=_cparams(2), name="expert_down")(h, w_down)


def _combine_kernel(sl_ref, af_ref, y_ref, x_ref, g_ref, o_ref, w_s):
    E, _, cap, td = y_ref.shape

    @pl.when(pl.program_id(2) == 0)
    def _():
        sl, af = sl_ref[0], af_ref[0]
        want = lax.broadcasted_iota(jnp.int32, (w_s.shape[0], cap), 1)
        for e in range(E):
            w_s[:, e * cap:(e + 1) * cap] = jnp.where(sl[:, e:e + 1] == want, af[:, e:e + 1], 0.0).astype(w_s.dtype)

    y = y_ref[:, 0].reshape(E * cap, td)
    o_ref[0] = x_ref[0] + g_ref[0] * _dot(w_s[...], y)


def _combine(slot_tok, aff, y, x, gate):
    B, L, _ = slot_tok.shape
    E, cap, D = y.shape[0], y.shape[2], y.shape[3]
    tm, td = _tile(512, L), _tile(1024, D)
    lanespec = pl.BlockSpec((1, tm, LANES), lambda b, m, d: (b, m, 0))
    xspec = pl.BlockSpec((1, tm, td), lambda b, m, d: (b, m, d))
    return pl.pallas_call(
        _combine_kernel, grid=(B, L // tm, D // td),
        in_specs=[lanespec, lanespec,
                  pl.BlockSpec((E, 1, cap, td), lambda b, m, d: (0, b, 0, d)),
                  xspec, pl.BlockSpec((1, 1, td), lambda b, m, d: (b, 0, d))],
        out_specs=xspec, out_shape=jax.ShapeDtypeStruct((B, L, D), F32),
        scratch_shapes=[pltpu.VMEM((tm, E * cap), BF16)],
        compiler_params=_cparams(3), name="moe_combine")(slot_tok, aff, y, x, gate.reshape(B, 1, D))


def kernel(x, c, ctx, c_ctx, w_ada, b_ada, norm1, norm2, w_in, hy_conv_w, hy_conv_b, hy_w1, hy_b1, hy_w2, hy_b2, hy_w3, hy_b3, hy_freq, hy_wout, hy_skip, hg_lb, hg_norm, w_proj_hy, w_proj_hg, w_out, w_router, w_gate, w_up, w_down, norm_f):
    assert w_ada.shape[0] == 1, "single-layer block"
    B, L, D = x.shape
    Lc = ctx.shape[1]
    hyw, hgw = hy_skip.shape[-1], hg_norm.shape[-1]
    H = hgw // HG_HEAD
    E = w_router.shape[-1]
    off_hg = 3 * hyw
    off_f = off_hg + hgw
    off_gate = off_hg + 5 * hgw
    in_cols = off_gate + 2 * D
    assert w_in.shape[-1] == in_cols and L % GRID_W == 0
    cap = EC_CAPACITY * L // E

    lb_all = jnp.cumsum(jax.nn.softmax(hg_lb.astype(F32), axis=0), axis=0)
    lb_f, lb_b = lb_all[0, 0], lb_all[0, 1]

    rows = -(-(B + 1) // 8) * 8
    cond = jnp.concatenate([c, c_ctx[None], jnp.zeros((rows - B - 1, D), F32)], axis=0)
    ada = _ada(cond, w_ada[0], b_ada[0])
    sh1, sc1, g1, sh2, sc2, g2 = [ada[:B, i * D:(i + 1) * D] for i in range(6)]
    csh1 = jnp.broadcast_to(ada[B:B + 1, 0:D], (B, D))
    csc1 = jnp.broadcast_to(ada[B:B + 1, D:2 * D], (B, D))

    xc_m = _normmod(ctx, norm1[0], csh1, csc1)
    pc = _mm(xc_m.reshape(B * Lc, D), w_in[0], off_f, 3 * hgw, 1024, 512, BF16, "ctx_proj")
    st_f, st_b = _hgrn_context_states(pc.reshape(B, Lc, 3 * hgw), hgw, lb_f, lb_b)

    x_m, x_mp = _normmod_parity(x, norm1[0], sh1, sc1)
    hyp = _mm_short_conv(x_mp, w_in[0], off_hg, hy_conv_w[0], hy_conv_b[0])
    proj = _mm(x_m.reshape(B * L, D), w_in[0], off_hg, in_cols - off_hg, 1024, 512, BF16,
               "in_proj")

    fwd, fwd_lo, inv = _dft_tables(L)
    hp, hm = _hy_filters(L, hy_w1[0], hy_b1[0], hy_w2[0], hy_b2[0], hy_w3[0], hy_b3[0], hy_freq[0],
                         hy_wout[0], hyw)
    n_orders = hp.shape[0]
    spectra = _filter_spectra(hp.reshape(n_orders, 2, L // 2, hyw), hm.reshape(n_orders, 2, L // 2, hyw),
                              fwd, fwd_lo)
    z = _long_conv_gated(hyp, 0, hyp, hyw, hyw, fwd, inv, spectra, 0, hy_skip[0, 0], False)
    y_hy = _long_conv_gated(z, 0, hyp, 2 * hyw, hyw, fwd, inv, spectra, 1, hy_skip[0, 1], True)

    y_hg = _hgrn(proj.reshape(B, L, in_cols - off_hg), 0, hgw, lb_f, lb_b, hg_norm[0], st_f, st_b)

    merged = _merge(y_hy.reshape(B * L, hyw), y_hg.reshape(B * L, hgw), w_proj_hy[0], w_proj_hg[0],
                    proj, off_gate - off_hg)
    x1 = _mm_res(merged, w_out[0], x.reshape(B * L, D), g1, L, 1024, 512).reshape(B, L, D)

    xm2, aff = _norm_router(x1, norm2[0], sh2, sc2, w_router[0])
    aff_t = jnp.swapaxes(aff[:, :, :E], 1, 2)
    slot_row = _route(aff_t, cap)
    xg = _gather(slot_row, xm2, cap)
    y = _experts(xg.reshape(E, B * cap, D), w_gate[0], w_up[0], w_down[0]).reshape(E, B, cap, D)
    slot_tok = jnp.pad(jnp.swapaxes(slot_row.reshape(B, E, L), 1, 2), ((0, 0), (0, 0), (0, LANES - E)),
                       constant_values=-1)
    x2 = _combine(slot_tok, aff, y, x1, g2)

    return _rmsnorm(x2.reshape(B * L, D), norm_f).reshape(B, L, D)
```

```python
import functools
import math

import jax
import jax.numpy as jnp
from jax import lax
from jax.experimental import pallas as pl
from jax.experimental.pallas import tpu as pltpu

F32 = jnp.float32
BF16 = jnp.bfloat16

EPS = 1e-6
GRID_W = 64
HY_FAST_DECAY = 0.3
HY_SLOW_DECAY = 1.5
HY_TARGET = 1e-2
HG_HEAD = 128
HG_CHUNK = 128
HG_SUB = 32
HG_HEADS_PER_STEP = 4
EC_CAPACITY = 2
LANES = 128
VMEM_LIMIT_BYTES = 56 * 1024 * 1024


def _cparams(n_axes):
    return pltpu.CompilerParams(dimension_semantics=("arbitrary",) * n_axes,
                                vmem_limit_bytes=VMEM_LIMIT_BYTES)


def _tile(pref, dim):
    t = min(pref, dim)
    while dim % t:
        t //= 2
    return t


def _dot(a, b):
    return jnp.dot(a, b, preferred_element_type=F32)


def _dot_nt(a, b):
    return lax.dot_general(a, b, (((1,), (1,)), ((), ())), preferred_element_type=F32)


def _dot_tn(a, b):
    return lax.dot_general(a, b, (((0,), (0,)), ((), ())), preferred_element_type=F32)


def _split(x):
    hi = x.astype(BF16)
    return hi, (x - hi.astype(F32)).astype(BF16)


def _dot3(a, b):
    ah, al = _split(a)
    bh, bl = _split(b)
    return _dot(ah, bh) + _dot(al, bh) + _dot(ah, bl)


def _silu(x):
    return x * jax.nn.sigmoid(x)


def _ada_kernel(c_ref, w_ref, b_ref, o_ref):
    o_ref[...] = _dot3(_silu(c_ref[...]), w_ref[...]) + b_ref[...]


def _ada(cond, w, b):
    R, D = cond.shape
    N = w.shape[1]
    tn = _tile(512, N)
    return pl.pallas_call(
        _ada_kernel, grid=(N // tn,),
        in_specs=[pl.BlockSpec((R, D), lambda n: (0, 0)),
                  pl.BlockSpec((D, tn), lambda n: (0, n)),
                  pl.BlockSpec((1, tn), lambda n: (0, n))],
        out_specs=pl.BlockSpec((R, tn), lambda n: (0, n)),
        out_shape=jax.ShapeDtypeStruct((R, N), F32),
        compiler_params=_cparams(1), name="ada")(cond, w, b.reshape(1, N))


def _rms_mod(x, g, sh, sc):
    y = x * lax.rsqrt(jnp.mean(x * x, axis=-1, keepdims=True) + EPS) * g
    return y * (1.0 + sc) + sh


def _normmod_kernel(x_ref, g_ref, sh_ref, sc_ref, o_ref):
    o_ref[0] = _rms_mod(x_ref[0], g_ref[...], sh_ref[0], sc_ref[0]).astype(o_ref.dtype)


def _normmod(x, g, sh, sc):
    B, L, D = x.shape
    tm = _tile(256, L)
    row = pl.BlockSpec((1, tm, D), lambda b, m: (b, m, 0))
    par = pl.BlockSpec((1, 1, D), lambda b, m: (b, 0, 0))
    return pl.pallas_call(
        _normmod_kernel, grid=(B, L // tm),
        in_specs=[row, pl.BlockSpec((1, D), lambda b, m: (0, 0)), par, par],
        out_specs=row, out_shape=jax.ShapeDtypeStruct((B, L, D), BF16),
        compiler_params=_cparams(2), name="normmod")(x, g.reshape(1, D), sh.reshape(B, 1, D), sc.reshape(B, 1, D))


def _normmod_parity_kernel(x_ref, g_ref, sh_ref, sc_ref, o_ref, op_ref, y_s):
    y = _rms_mod(x_ref[0], g_ref[...], sh_ref[0], sc_ref[0])
    o_ref[0] = y.astype(o_ref.dtype)
    half = y.shape[0] // 2
    for g in range(y_s.shape[0]):
        ln = slice(g * LANES, (g + 1) * LANES)
        y_s[g] = y[:, ln]
        op_ref[0, 0, :, ln] = y_s[g, pl.ds(0, half, stride=2), :].astype(op_ref.dtype)
        op_ref[0, 1, :, ln] = y_s[g, pl.ds(1, half, stride=2), :].astype(op_ref.dtype)


def _normmod_parity(x, g, sh, sc):
    B, L, D = x.shape
    tm = _tile(256, L)
    row = pl.BlockSpec((1, tm, D), lambda b, m: (b, m, 0))
    par = pl.BlockSpec((1, 1, D), lambda b, m: (b, 0, 0))
    return pl.pallas_call(
        _normmod_parity_kernel, grid=(B, L // tm),
        in_specs=[row, pl.BlockSpec((1, D), lambda b, m: (0, 0)), par, par],
        out_specs=[row, pl.BlockSpec((1, 2, tm // 2, D), lambda b, m: (b, 0, m, 0))],
        out_shape=[jax.ShapeDtypeStruct((B, L, D), BF16), jax.ShapeDtypeStruct((B, 2, L // 2, D), BF16)],
        scratch_shapes=[pltpu.VMEM((D // LANES, tm, LANES), F32)],
        compiler_params=_cparams(2), name="normmod_parity")(
            x, g.reshape(1, D), sh.reshape(B, 1, D), sc.reshape(B, 1, D))


def _rmsnorm_kernel(x_ref, g_ref, o_ref):
    x = x_ref[...]
    o_ref[...] = x * lax.rsqrt(jnp.mean(x * x, axis=-1, keepdims=True) + EPS) * g_ref[...]


def _rmsnorm(x, g):
    M, D = x.shape
    tm = _tile(256, M)
    row = pl.BlockSpec((tm, D), lambda m: (m, 0))
    return pl.pallas_call(
        _rmsnorm_kernel, grid=(M // tm,),
        in_specs=[row, pl.BlockSpec((1, D), lambda m: (0, 0))],
        out_specs=row, out_shape=jax.ShapeDtypeStruct((M, D), F32),
        compiler_params=_cparams(1), name="final_norm")(x, g.reshape(1, D))


def _mm_kernel(x_ref, w_ref, o_ref, wbf_ref):
    @pl.when(pl.program_id(1) == 0)
    def _():
        wbf_ref[...] = w_ref[...].astype(BF16)
    o_ref[...] = _dot(x_ref[...], wbf_ref[...]).astype(o_ref.dtype)


def _mm(x, w, col0, ncols, tm, tn, out_dtype, name):
    M, K = x.shape
    tm, tn = _tile(tm, M), _tile(tn, ncols)
    assert col0 % tn == 0
    c0 = col0 // tn
    return pl.pallas_call(
        _mm_kernel, grid=(ncols // tn, M // tm),
        in_specs=[pl.BlockSpec((tm, K), lambda n, m: (m, 0)),
                  pl.BlockSpec((K, tn), lambda n, m: (0, c0 + n))],
        out_specs=pl.BlockSpec((tm, tn), lambda n, m: (m, n)),
        out_shape=jax.ShapeDtypeStruct((M, ncols), out_dtype),
        scratch_shapes=[pltpu.VMEM((K, tn), BF16)],
        compiler_params=_cparams(2), name=name)(x, w)


def _mm_res_kernel(x_ref, w_ref, r_ref, g_ref, o_ref, wbf_ref):
    @pl.when(pl.program_id(1) == 0)
    def _():
        wbf_ref[...] = w_ref[...].astype(BF16)
    o_ref[...] = r_ref[...] + g_ref[0] * _dot(x_ref[...], wbf_ref[...])


def _mm_res(x, w, res, gate, rows_per_batch, tm, tn):
    M, K = x.shape
    N = w.shape[1]
    tm, tn = _tile(tm, rows_per_batch), _tile(tn, N)
    mpb = rows_per_batch // tm
    B = gate.shape[0]
    return pl.pallas_call(
        _mm_res_kernel, grid=(N // tn, M // tm),
        in_specs=[pl.BlockSpec((tm, K), lambda n, m: (m, 0)),
                  pl.BlockSpec((K, tn), lambda n, m: (0, n)),
                  pl.BlockSpec((tm, tn), lambda n, m: (m, n)),
                  pl.BlockSpec((1, 1, tn), lambda n, m: (m // mpb, 0, n))],
        out_specs=pl.BlockSpec((tm, tn), lambda n, m: (m, n)),
        out_shape=jax.ShapeDtypeStruct((M, N), F32),
        scratch_shapes=[pltpu.VMEM((K, tn), BF16)],
        compiler_params=_cparams(2), name="out_proj")(x, w, res, gate.reshape(B, 1, N))


def _merge_kernel(yhy_ref, yhg_ref, why_ref, whg_ref, ghy_ref, ghg_ref, o_ref, why_s, whg_s):
    @pl.when(pl.program_id(1) == 0)
    def _():
        why_s[...] = why_ref[...].astype(BF16)
        whg_s[...] = whg_ref[...].astype(BF16)
    ts = _tile(256, o_ref.shape[0])
    blocks = [slice(r * ts, (r + 1) * ts) for r in range(o_ref.shape[0] // ts)]
    prods = [(_dot(yhy_ref[rows, :], why_s[...]), _dot(yhg_ref[rows, :], whg_s[...])) for rows in blocks]
    for rows, (a, b) in zip(blocks, prods):
        o_ref[rows, :] = (jax.nn.sigmoid(ghy_ref[rows, :].astype(F32)) * a
                          + jax.nn.sigmoid(ghg_ref[rows, :].astype(F32)) * b).astype(o_ref.dtype)


def _merge(yhy, yhg, why, whg, proj, off_gate):
    M, K1 = yhy.shape
    K2 = yhg.shape[1]
    D = why.shape[1]
    tm, tn = _tile(1024, M), _tile(512, D)
    assert off_gate % tn == 0
    g0, g1 = off_gate // tn, (off_gate + D) // tn
    return pl.pallas_call(
        _merge_kernel, grid=(D // tn, M // tm),
        in_specs=[pl.BlockSpec((tm, K1), lambda n, m: (m, 0)),
                  pl.BlockSpec((tm, K2), lambda n, m: (m, 0)),
                  pl.BlockSpec((K1, tn), lambda n, m: (0, n)),
                  pl.BlockSpec((K2, tn), lambda n, m: (0, n)),
                  pl.BlockSpec((tm, tn), lambda n, m: (m, g0 + n)),
                  pl.BlockSpec((tm, tn), lambda n, m: (m, g1 + n))],
        out_specs=pl.BlockSpec((tm, tn), lambda n, m: (m, n)),
        out_shape=jax.ShapeDtypeStruct((M, D), BF16),
        scratch_shapes=[pltpu.VMEM((K1, tn), BF16), pltpu.VMEM((K2, tn), BF16)],
        compiler_params=_cparams(2), name="merge")(yhy, yhg, why, whg, proj, proj)


def _mm_sconv_kernel(xe_ref, xo_ref, w_ref, cw_ref, cb_ref, o_ref, wbf_ref):
    @pl.when(pl.program_id(1) == 0)
    def _():
        wbf_ref[...] = w_ref[...].astype(BF16)
    pe = _dot(xe_ref[0, 0], wbf_ref[...])
    po = _dot(xo_ref[0, 0], wbf_ref[...])
    th = pe.shape[0]
    pos = lax.broadcasted_iota(jnp.int32, pe.shape, 0) % (GRID_W // 2)
    before_even = jnp.where(pos != 0, pltpu.roll(po, 1, 0), 0.0)
    after_odd = jnp.where(pos != GRID_W // 2 - 1, pltpu.roll(pe, th - 1, 0), 0.0)
    cw, cb = cw_ref[...], cb_ref[...]
    o_ref[0, 0] = (before_even * cw[0:1] + pe * cw[1:2] + po * cw[2:3] + cb).astype(o_ref.dtype)
    o_ref[0, 1] = (pe * cw[0:1] + po * cw[1:2] + after_odd * cw[2:3] + cb).astype(o_ref.dtype)


def _mm_short_conv(xp, w, ncols, conv_w, conv_b):
    B, _, lh, K = xp.shape
    th, tn = _tile(512, lh), _tile(512, ncols)
    assert th % (GRID_W // 2) == 0
    nt = lh // th
    return pl.pallas_call(
        _mm_sconv_kernel, grid=(ncols // tn, B * nt),
        in_specs=[pl.BlockSpec((1, 1, th, K), lambda n, m: (m // nt, 0, m % nt, 0)),
                  pl.BlockSpec((1, 1, th, K), lambda n, m: (m // nt, 1, m % nt, 0)),
                  pl.BlockSpec((K, tn), lambda n, m: (0, n)),
                  pl.BlockSpec((3, tn), lambda n, m: (0, n)),
                  pl.BlockSpec((1, tn), lambda n, m: (0, n))],
        out_specs=pl.BlockSpec((1, 2, th, tn), lambda n, m: (m // nt, 0, m % nt, n)),
        out_shape=jax.ShapeDtypeStruct((B, 2, lh, ncols), BF16),
        scratch_shapes=[pltpu.VMEM((K, tn), BF16)],
        compiler_params=_cparams(2), name="in_proj_hyena")(xp, xp, w, conv_w, conv_b.reshape(1, ncols))


def _dft_tables(L):
    lh, n = L // 2, 2 * L
    k = jnp.arange(lh, dtype=jnp.int32)
    alt = jnp.where(k % 2 == 0, 1.0, -1.0).astype(F32)
    ang_e = ((k[:, None] * (2 * k)[None, :]) % n).astype(F32) * (2.0 * math.pi / n)
    ce, se = jnp.cos(ang_e), -jnp.sin(ang_e)
    ang_1 = k.astype(F32)[:, None] * (2.0 * math.pi / n)
    c1, s1 = jnp.cos(ang_1), jnp.sin(ang_1)
    co, so = ce * c1 + se * s1, se * c1 - ce * s1
    fwd = jnp.stack([ce, co, se.at[0].set(alt), so.at[0].set(-alt)])
    inv = (2.0 / n) * jnp.stack([ce.T.at[:, 0].set(0.5), co.T.at[:, 0].set(0.5),
                                 se.T.at[:, 0].set(alt), so.T.at[:, 0].set(-alt)])
    fwd_hi, fwd_lo = _split(fwd)
    return fwd_hi, fwd_lo, inv.astype(BF16)


def _dftf_kernel(ue_ref, uo_ref, t_ref, krl_ref, krh_ref, kil_ref, kih_ref, ae_ref, ao_ref, be_ref, bo_ref):
    tm = t_ref.shape[1]
    ue, uo = ue_ref[0, 0], uo_ref[0, 0]
    p, q = _dot(t_ref[0], ue), _dot(t_ref[1], uo)
    r, t = _dot(t_ref[2], ue), _dot(t_ref[3], uo)
    first = (lax.broadcasted_iota(jnp.int32, p.shape, 0) + pl.program_id(1) * tm) == 0
    krl, krh, kil, kih = krl_ref[0], krh_ref[0], kil_ref[0], kih_ref[0]
    m_re, m_im, km_re, km_im = r[0:1], t[0:1], kil[0:1], kih[0:1]
    r, t = jnp.where(first, 0.0, r), jnp.where(first, 0.0, t)
    kil, kih = jnp.where(first, 0.0, kil), jnp.where(first, 0.0, kih)
    re_l, re_h, im_l, im_h = p + q, p - q, r + t, t - r
    yr_l, yi_l = re_l * krl - im_l * kil, re_l * kil + im_l * krl
    yr_h, yi_h = re_h * krh - im_h * kih, re_h * kih + im_h * krh
    ae_ref[0] = (yr_l + yr_h).astype(ae_ref.dtype)
    ao_ref[0] = (yr_l - yr_h).astype(ao_ref.dtype)
    be_ref[0] = jnp.where(first, m_re * km_re - m_im * km_im, yi_l - yi_h).astype(be_ref.dtype)
    bo_ref[0] = jnp.where(first, m_re * km_im + m_im * km_re, yi_l + yi_h).astype(bo_ref.dtype)


def _dfti_kernel(ae_ref, ao_ref, be_ref, bo_ref, g_ref, u_ref, gate_ref, skip_ref, o_ref, *y_s):
    skip = skip_ref[...]
    ye = _dot(g_ref[0], ae_ref[0]) + _dot(g_ref[2], be_ref[0])
    yo = _dot(g_ref[1], ao_ref[0]) + _dot(g_ref[3], bo_ref[0])
    oe = gate_ref[0, 0].astype(F32) * (ye + skip * u_ref[0, 0].astype(F32))
    oo = gate_ref[0, 1].astype(F32) * (yo + skip * u_ref[0, 1].astype(F32))
    if y_s:
        tm = oe.shape[0]
        for g in range(y_s[0].shape[0]):
            ln = slice(g * LANES, (g + 1) * LANES)
            y_s[0][g, pl.ds(0, tm, stride=2), :] = oe[:, ln]
            y_s[0][g, pl.ds(1, tm, stride=2), :] = oo[:, ln]
            o_ref[0, :, ln] = y_s[0][g].astype(o_ref.dtype)
    else:
        o_ref[0, 0] = oe.astype(o_ref.dtype)
        o_ref[0, 1] = oo.astype(o_ref.dtype)


def _long_conv_gated(u_arr, u_col0, gate_arr, gate_col0, C, fwd, inv, spectra, order, skip, natural_out):
    B, _, lh, _ = u_arr.shape
    tm, tn = _tile(512, lh), _tile(512, C)
    assert u_col0 % tn == 0 and gate_col0 % tn == 0
    uc, gc = u_col0 // tn, gate_col0 // tn
    grid = (C // tn, lh // tm, B)
    tab = pl.BlockSpec((4, tm, lh), lambda n, m, b: (0, m, 0))
    kspec = pl.BlockSpec((1, tm, tn), lambda n, m, b: (order, m, n))
    ys = pl.pallas_call(
        _dftf_kernel, grid=grid,
        in_specs=[pl.BlockSpec((1, 1, lh, tn), lambda n, m, b: (b, 0, 0, uc + n)),
                  pl.BlockSpec((1, 1, lh, tn), lambda n, m, b: (b, 1, 0, uc + n)),
                  tab, kspec, kspec, kspec, kspec],
        out_specs=[pl.BlockSpec((1, tm, tn), lambda n, m, b: (b, m, n))] * 4,
        out_shape=[jax.ShapeDtypeStruct((B, lh, C), BF16)] * 4,
        compiler_params=_cparams(3), name="dft_fwd")(u_arr, u_arr, fwd, *spectra)
    yspec = pl.BlockSpec((1, lh, tn), lambda n, m, b: (b, 0, n))
    if natural_out:
        out_spec = pl.BlockSpec((1, 2 * tm, tn), lambda n, m, b: (b, m, n))
        out_shape = jax.ShapeDtypeStruct((B, 2 * lh, C), BF16)
        scratch = [pltpu.VMEM((tn // LANES, 2 * tm, LANES), F32)]
    else:
        out_spec = pl.BlockSpec((1, 2, tm, tn), lambda n, m, b: (b, 0, m, n))
        out_shape = jax.ShapeDtypeStruct((B, 2, lh, C), BF16)
        scratch = []
    return pl.pallas_call(
        _dfti_kernel, grid=grid,
        in_specs=[yspec, yspec, yspec, yspec, tab,
                  pl.BlockSpec((1, 2, tm, tn), lambda n, m, b: (b, 0, m, uc + n)),
                  pl.BlockSpec((1, 2, tm, tn), lambda n, m, b: (b, 0, m, gc + n)),
                  pl.BlockSpec((1, tn), lambda n, m, b: (0, n))],
        out_specs=out_spec, out_shape=out_shape, scratch_shapes=scratch,
        compiler_params=_cparams(3), name="dft_inv")(*ys, inv, u_arr, gate_arr, skip.reshape(1, C))


def _hy_filter_kernel(z_ref, w1_ref, b1_ref, w2_ref, b2_ref, w3_ref, b3_ref, fr_ref, wf_ref, wb_ref,
                      t_ref, ad_ref, hp_ref, hm_ref, hh_s, hl_s):
    @pl.when((pl.program_id(0) == 0) & (pl.program_id(1) == 0))
    def _():
        fr = fr_ref[...]
        h = jnp.sin(fr * (_dot3(z_ref[...], w1_ref[...]) + b1_ref[...]))
        h = jnp.sin(fr * (_dot3(h, w2_ref[...]) + b2_ref[...]))
        h = jnp.sin(fr * (_dot3(h, w3_ref[...]) + b3_ref[...]))
        hh_s[...], hl_s[...] = _split(h)

    def dot3h(w):
        wh, wl = _split(w)
        return _dot(hh_s[...], wh) + _dot(hl_s[...], wh) + _dot(hh_s[...], wl)

    decay = jnp.exp(-t_ref[...] * ad_ref[...])
    hf = dot3h(wf_ref[...]) * decay
    hb = dot3h(wb_ref[...]) * decay
    row = lax.broadcasted_iota(jnp.int32, hb.shape, 0)
    hb = jnp.where(row == 0, 0.0, hb)
    hp_ref[0] = hf + hb
    hm_ref[0] = hf - hb


def _hy_filters(L, w1, b1, w2, b2, w3, b3, freq, wout, width):
    emb, ffn = w1.shape
    n_orders = wout.shape[1] // (2 * width)
    t = jnp.linspace(0.0, 1.0, L, dtype=F32)[:, None]
    bands = (emb - 1) // 2
    w = 2.0 * math.pi * jnp.arange(L, dtype=F32) / L
    f = jnp.linspace(1e-4, bands - 1, bands, dtype=F32)
    fw = w[:, None] * f[None, :]
    z = jnp.concatenate([t, jnp.cos(fw), -jnp.sin(fw)], axis=-1)
    z = jnp.concatenate([z[0::2], z[1::2]], axis=0)
    t = jnp.concatenate([t[0::2], t[1::2]], axis=0)
    deltas = jnp.linspace(math.log(HY_TARGET) / HY_SLOW_DECAY, math.log(HY_TARGET) / HY_FAST_DECAY,
                          width, dtype=F32)
    pe, pf = LANES - emb, LANES - ffn
    z = jnp.pad(z, ((0, 0), (0, pe)))
    w1p = jnp.pad(w1, ((0, pe), (0, pf)))
    w2p = jnp.pad(w2, ((0, pf), (0, pf)))
    w3p = jnp.pad(w3, ((0, pf), (0, pf)))
    woutp = jnp.pad(wout, ((0, pf), (0, 0)))
    vec = lambda a: jnp.pad(a, (0, pf)).reshape(1, LANES)
    tc = _tile(512, width)
    nt = width // tc
    full = lambda shp: pl.BlockSpec(shp, lambda o, n: (0, 0))
    out = pl.BlockSpec((1, L, tc), lambda o, n: (o, 0, n))
    shp = jax.ShapeDtypeStruct((n_orders, L, width), F32)
    return pl.pallas_call(
        _hy_filter_kernel, grid=(n_orders, nt),
        in_specs=[full((L, LANES)), full((LANES, LANES)), full((1, LANES)), full((LANES, LANES)), full((1, LANES)),
                  full((LANES, LANES)), full((1, LANES)), full((1, LANES)),
                  pl.BlockSpec((LANES, tc), lambda o, n: (0, (2 * o) * nt + n)),
                  pl.BlockSpec((LANES, tc), lambda o, n: (0, (2 * o + 1) * nt + n)),
                  full((L, 1)), pl.BlockSpec((1, tc), lambda o, n: (0, n))],
        out_specs=[out, out], out_shape=[shp, shp],
        scratch_shapes=[pltpu.VMEM((L, LANES), BF16)] * 2,
        compiler_params=_cparams(2), name="hy_filter")(
            z, w1p, vec(b1), w2p, vec(b2), w3p, vec(b3), vec(freq), woutp, woutp, t, jnp.abs(deltas).reshape(1, width))


def _spectrum_kernel(th_ref, tl_ref, hp_ref, hm_ref, krl_ref, krh_ref, kil_ref, kih_ref):
    tm = th_ref.shape[1]

    def dot3(i, x, rows=slice(None)):
        xh, xl = x
        return _dot(th_ref[i, rows], xh) + _dot(tl_ref[i, rows], xh) + _dot(th_ref[i, rows], xl)

    pe, po = _split(hp_ref[0, 0]), _split(hp_ref[0, 1])
    me, mo = _split(hm_ref[0, 0]), _split(hm_ref[0, 1])
    p, q, r, t = dot3(0, pe), dot3(1, po), dot3(2, me), dot3(3, mo)
    mid_re = dot3(2, pe, slice(0, 8))[0:1]
    first = (lax.broadcasted_iota(jnp.int32, p.shape, 0) + pl.program_id(2) * tm) == 0
    krl_ref[0] = p + q
    krh_ref[0] = p - q
    kil_ref[0] = jnp.where(first, mid_re, r + t)
    kih_ref[0] = jnp.where(first, t[0:1], t - r)


def _filter_spectra(hp, hm, fwd_hi, fwd_lo):
    n_orders, _, lh, width = hp.shape
    tm, tn = _tile(512, lh), _tile(512, width)
    tab = pl.BlockSpec((4, tm, lh), lambda o, n, m: (0, m, 0))
    hs = pl.BlockSpec((1, 2, lh, tn), lambda o, n, m: (o, 0, 0, n))
    out = pl.BlockSpec((1, tm, tn), lambda o, n, m: (o, m, n))
    shp = jax.ShapeDtypeStruct((n_orders, lh, width), F32)
    return pl.pallas_call(
        _spectrum_kernel, grid=(n_orders, width // tn, lh // tm),
        in_specs=[tab, tab, hs, hs], out_specs=[out] * 4, out_shape=[shp] * 4,
        compiler_params=_cparams(3), name="hy_spectrum")(fwd_hi, fwd_lo, hp, hm)


def _hg_gate(z, lb):
    f = lb + (1.0 - lb) * jax.nn.sigmoid(z)
    return jnp.log(f), 1.0 - f


def _chunk_cumsum(g, rev):
    C = HG_CHUNK
    t = lax.broadcasted_iota(jnp.int32, (C, C), 0)
    s = lax.broadcasted_iota(jnp.int32, (C, C), 1)
    tri = jnp.where((s >= t) if rev else (s <= t), 1.0, 0.0).astype(BF16)
    hi, lo = _split(g)
    return jnp.concatenate([_dot(tri, hi[i * C:(i + 1) * C]) + _dot(tri, lo[i * C:(i + 1) * C])
                            for i in range(g.shape[0] // C)], axis=0)


def _row_at(b, step, rev):
    r = HG_CHUNK - 1 - step if rev else step
    return b[r:r + 1, :]


def _hg_kv(k, v, b, rev):
    bl = _row_at(b, HG_CHUNK - 1, rev)
    return _dot_tn(v.astype(BF16), (k * jnp.exp(bl - b)).astype(BF16))


def _hg_state(k, v, b, st, rev):
    return st * jnp.exp(_row_at(b, HG_CHUNK - 1, rev)) + _hg_kv(k, v, b, rev)


def _hg_scores(q, k, b, rev):
    C, SB = HG_CHUNK, HG_SUB
    nsb = C // SB

    def blk(x, i):
        r = nsb - 1 - i if rev else i
        return x[r * SB:(r + 1) * SB]

    def rows(parts):
        return jnp.concatenate(parts[::-1] if rev else parts, axis=0)

    mid_rows = [_row_at(b, i * SB + SB // 2, rev) for i in range(nsb)]
    end_rows = [_row_at(b, i * SB + SB - 1, rev) for i in range(nsb)]
    zero = jnp.zeros((SB, q.shape[1]), BF16)

    def only(i, piece):
        return rows([zero] * i + [piece.astype(BF16)] + [zero] * (nsb - 1 - i))

    qs = [only(i, blk(q, i) * jnp.exp(blk(b, i) - mid_rows[i])) for i in range(nsb)]
    ks = [only(i, blk(k, i) * jnp.exp(mid_rows[i] - blk(b, i))) for i in range(nsb)]
    for j in range(nsb - 1):
        qs.append(rows([zero] * (j + 1)
                       + [(blk(q, i) * jnp.exp(blk(b, i) - end_rows[j])).astype(BF16) for i in range(j + 1, nsb)]))
        ks.append(only(j, blk(k, j) * jnp.exp(end_rows[j] - blk(b, j))))
    tt = lax.broadcasted_iota(jnp.int32, (C, C), 0)
    ss = lax.broadcasted_iota(jnp.int32, (C, C), 1)
    causal = (ss >= tt) if rev else (ss <= tt)
    return jnp.where(causal, _dot_nt(jnp.concatenate(qs, axis=1), jnp.concatenate(ks, axis=1)), 0.0)


def _hgrn_kernel(q_ref, zf_ref, zb_ref, i_ref, g_ref, lbf_ref, lbb_ref, gn_ref, sf_ref, sb_ref, o_ref,
                 kf_s, kb_s, bf_s, cb_s, of_s, ob_s):
    C = HG_CHUNK
    nc = kf_s.shape[0] // C
    nh = kf_s.shape[1] // HG_HEAD
    gf, kf = _hg_gate(zf_ref[0].astype(F32), lbf_ref[0])
    kf_s[...] = kf
    bf_s[...] = _chunk_cumsum(gf, False)
    gb, kb = _hg_gate(zb_ref[0].astype(F32), lbb_ref[0])
    kb_s[...] = kb
    cb_s[...] = _chunk_cumsum(gb, True)
    dirs = ((kf_s, bf_s, of_s, False), (kb_s, cb_s, ob_s, True))

    def body(c, carry):
        work = []
        for h in range(nh):
            ln = slice(h * HG_HEAD, (h + 1) * HG_HEAD)
            for d, (k_s, b_s, o_s, rev) in enumerate(dirs):
                cc = nc - 1 - c if rev else c
                rows = pl.ds(pl.multiple_of(cc * C, C), C)
                q = _silu(q_ref[0, rows, ln].astype(F32)) * (HG_HEAD ** -0.5)
                v = i_ref[0, rows, ln].astype(F32)
                work.append((o_s, rows, ln, rev, q, k_s[rows, ln], v, b_s[rows, ln], carry[2 * h + d]))
        scores = [_hg_scores(q, k, b, rev) for (_, _, _, rev, q, k, _, b, _) in work]
        for (o_s, rows, ln, rev, q, k, v, b, st), a in zip(work, scores):
            o_s[rows, ln] = (_dot(a.astype(BF16), v.astype(BF16))
                             + _dot_nt((q * jnp.exp(b)).astype(BF16), st.astype(BF16)))
        return tuple(_hg_state(k, v, b, st, rev) for (_, _, _, rev, _, k, v, b, st) in work)

    init = []
    for h in range(nh):
        init += [sf_ref[0, h], sb_ref[0, h]]
    lax.fori_loop(0, nc, body, tuple(init))
    for h in range(nh):
        ln = slice(h * HG_HEAD, (h + 1) * HG_HEAD)
        o = of_s[:, ln] + ob_s[:, ln]
        o = o * lax.rsqrt(jnp.mean(o * o, axis=-1, keepdims=True) + EPS)
        o_ref[0, :, ln] = (o * gn_ref[0, :, ln] * _silu(g_ref[0, :, ln].astype(F32))).astype(o_ref.dtype)


def _hgrn(proj3, off_hg, width, lb_f, lb_b, gn, st_f, st_b):
    B, L, _ = proj3.shape
    H = width // HG_HEAD
    nh = _tile(HG_HEADS_PER_STEP, H)
    wb = nh * HG_HEAD
    assert L % HG_CHUNK == 0 and off_hg % wb == 0
    c0, hb = off_hg // wb, H // nh

    def col(i):
        return pl.BlockSpec((1, L, wb), lambda b, h: (b, 0, c0 + i * hb + h))

    par = pl.BlockSpec((1, 1, wb), lambda b, h: (0, 0, h))
    st = pl.BlockSpec((1, nh, HG_HEAD, HG_HEAD), lambda b, h: (b, h, 0, 0))
    return pl.pallas_call(
        _hgrn_kernel, grid=(B, hb),
        in_specs=[col(0), col(1), col(2), col(3), col(4), par, par, par, st, st],
        out_specs=pl.BlockSpec((1, L, wb), lambda b, h: (b, 0, h)),
        out_shape=jax.ShapeDtypeStruct((B, L, width), BF16),
        scratch_shapes=[pltpu.VMEM((L, wb), F32)] * 6,
        compiler_params=_cparams(2), name="hgrn")(
            proj3, proj3, proj3, proj3, proj3,
            lb_f.reshape(1, 1, width), lb_b.reshape(1, 1, width), gn.reshape(1, 1, width), st_f, st_b)


def _hgctx_kernel(zf_ref, zb_ref, i_ref, lbf_ref, lbb_ref, sf_ref, sb_ref):
    C = HG_CHUNK
    nc = zf_ref.shape[1] // C
    gf, kf = _hg_gate(zf_ref[0].astype(F32), lbf_ref[0])
    gb, kb = _hg_gate(zb_ref[0].astype(F32), lbb_ref[0])
    bf = _chunk_cumsum(gf, False)
    cb = _chunk_cumsum(gb, True)
    v = i_ref[0].astype(F32)
    stf = jnp.zeros((HG_HEAD, HG_HEAD), F32)
    stb = jnp.zeros((HG_HEAD, HG_HEAD), F32)
    for c in range(nc):
        sl = slice(c * C, (c + 1) * C)
        stf = _hg_state(kf[sl], v[sl], bf[sl], stf, False)
    for c in reversed(range(nc)):
        sl = slice(c * C, (c + 1) * C)
        stb = _hg_state(kb[sl], v[sl], cb[sl], stb, True)
    sf_ref[0, 0] = stf
    sb_ref[0, 0] = stb


def _hgrn_context_states(pc3, width, lb_f, lb_b):
    B, Lc, _ = pc3.shape
    H = width // HG_HEAD
    assert Lc % HG_CHUNK == 0

    def col(i):
        return pl.BlockSpec((1, Lc, HG_HEAD), lambda b, h: (b, 0, i * H + h))

    par = pl.BlockSpec((1, 1, HG_HEAD), lambda b, h: (h, 0, 0))
    st = pl.BlockSpec((1, 1, HG_HEAD, HG_HEAD), lambda b, h: (b, h, 0, 0))
    shp = jax.ShapeDtypeStruct((B, H, HG_HEAD, HG_HEAD), F32)
    return pl.pallas_call(
        _hgctx_kernel, grid=(B, H),
        in_specs=[col(0), col(1), col(2), par, par],
        out_specs=[st, st], out_shape=[shp, shp],
        compiler_params=_cparams(2), name="hgrn_ctx")(
            pc3, pc3, pc3, lb_f.reshape(H, 1, HG_HEAD), lb_b.reshape(H, 1, HG_HEAD))


def _norm_router_kernel(x_ref, g_ref, sh_ref, sc_ref, wr_ref, xm_ref, aff_ref, *, n_experts):
    xm = _rms_mod(x_ref[0], g_ref[...], sh_ref[0], sc_ref[0])
    xm_ref[0] = xm.astype(xm_ref.dtype)
    logits = _dot3(xm, wr_ref[...])
    lane = lax.broadcasted_iota(jnp.int32, logits.shape, 1)
    logits = jnp.where(lane < n_experts, logits, -1e30)
    e = jnp.exp(logits - jnp.max(logits, axis=-1, keepdims=True))
    aff_ref[0] = e / jnp.sum(e, axis=-1, keepdims=True)


def _norm_router(x, g, sh, sc, w_router):
    B, L, D = x.shape
    E = w_router.shape[1]
    tm = _tile(256, L)
    wr = jnp.pad(w_router, ((0, 0), (0, LANES - E)))
    row = pl.BlockSpec((1, tm, D), lambda b, m: (b, m, 0))
    par = pl.BlockSpec((1, 1, D), lambda b, m: (b, 0, 0))
    return pl.pallas_call(
        functools.partial(_norm_router_kernel, n_experts=E), grid=(B, L // tm),
        in_specs=[row, pl.BlockSpec((1, D), lambda b, m: (0, 0)), par, par,
                  pl.BlockSpec((D, LANES), lambda b, m: (0, 0))],
        out_specs=[row, pl.BlockSpec((1, tm, LANES), lambda b, m: (b, m, 0))],
        out_shape=[jax.ShapeDtypeStruct((B, L, D), BF16), jax.ShapeDtypeStruct((B, L, LANES), F32)],
        compiler_params=_cparams(2), name="norm_router")(
            x, g.reshape(1, D), sh.reshape(B, 1, D), sc.reshape(B, 1, D), wr)


def _route_kernel(a_ref, tri_ref, slot_ref, *, cap):
    bits = lax.bitcast_convert_type(a_ref[...], jnp.int32)

    def count(mask):
        return jnp.sum(jnp.where(mask, 1.0, 0.0), axis=-1, keepdims=True)

    def body(i, thr):
        cand = thr | jnp.left_shift(jnp.int32(1), 30 - i)
        return jnp.where(count(bits >= cand) >= cap, cand, thr)

    thr = lax.fori_loop(0, 31, body, jnp.zeros((bits.shape[0], 1), jnp.int32))
    above, tie = bits > thr, bits == thr
    tri = tri_ref[...]
    ties_before = _dot(jnp.where(tie, 1.0, 0.0).astype(BF16), tri)
    chosen = above | (tie & (ties_before < cap - count(above)))
    before = _dot(jnp.where(chosen, 1.0, 0.0).astype(BF16), tri)
    slot_ref[...] = jnp.where(chosen, before.astype(jnp.int32), -1)


def _route(aff_t, cap):
    B, E, L = aff_t.shape
    idx = jnp.arange(L, dtype=jnp.int32)
    tri = (idx[:, None] < idx[None, :]).astype(BF16)
    slot = pl.pallas_call(
        functools.partial(_route_kernel, cap=cap), grid=(1,),
        in_specs=[pl.BlockSpec((B * E, L), lambda i: (0, 0)), pl.BlockSpec((L, L), lambda i: (0, 0))],
        out_specs=pl.BlockSpec((B * E, L), lambda i: (0, 0)),
        out_shape=jax.ShapeDtypeStruct((B * E, L), jnp.int32),
        compiler_params=_cparams(1), name="route")(aff_t.reshape(B * E, L), tri)
    return slot.reshape(B, E, 1, L)


def _gather_kernel(slot_ref, xm_ref, o_ref):
    ne, _, cap, td = o_ref.shape
    L = slot_ref.shape[3]
    want = lax.broadcasted_iota(jnp.int32, (cap, L), 0)
    sel = jnp.concatenate([jnp.where(want == slot_ref[0, e], 1.0, 0.0).astype(BF16) for e in range(ne)], axis=0)
    o_ref[:, 0] = _dot(sel, xm_ref[0]).astype(o_ref.dtype).reshape(ne, cap, td)


def _gather(slot_row, xm, cap):
    B, E, _, L = slot_row.shape
    D = xm.shape[2]
    td, ne = _tile(1024, D), _tile(4, E)
    return pl.pallas_call(
        _gather_kernel, grid=(B, D // td, E // ne),
        in_specs=[pl.BlockSpec((1, ne, 1, L), lambda b, d, e: (b, e, 0, 0)),
                  pl.BlockSpec((1, L, td), lambda b, d, e: (b, 0, d))],
        out_specs=pl.BlockSpec((ne, 1, cap, td), lambda b, d, e: (e, b, 0, d)),
        out_shape=jax.ShapeDtypeStruct((E, B, cap, D), BF16),
        compiler_params=_cparams(3), name="moe_gather")(slot_row, xm)


def _expert_up_kernel(x_ref, wg_ref, wu_ref, h_ref):
    x = x_ref[0]
    a = _dot(x, wg_ref[0].astype(BF16))
    u = _dot(x, wu_ref[0].astype(BF16))
    h_ref[0] = (_silu(a) * u).astype(h_ref.dtype)


def _expert_down_kernel(h_ref, wd_ref, y_ref):
    y_ref[0] = _dot(h_ref[0], wd_ref[0].astype(BF16)).astype(y_ref.dtype)


def _experts(xg, w_gate, w_up, w_down):
    E, M, D = xg.shape
    FF = w_gate.shape[2]
    tf, td = _tile(256, FF), _tile(1024, D)
    h = pl.pallas_call(
        _expert_up_kernel, grid=(E, FF // tf),
        in_specs=[pl.BlockSpec((1, M, D), lambda e, n: (e, 0, 0)),
                  pl.BlockSpec((1, D, tf), lambda e, n: (e, 0, n)),
                  pl.BlockSpec((1, D, tf), lambda e, n: (e, 0, n))],
        out_specs=pl.BlockSpec((1, M, tf), lambda e, n: (e, 0, n)),
        out_shape=jax.ShapeDtypeStruct((E, M, FF), BF16),
        compiler_params=_cparams(2), name="expert_up")(xg, w_gate, w_up)
    return pl.pallas_call(
        _expert_down_kernel, grid=(E, D // td),
        in_specs=[pl.BlockSpec((1, M, FF), lambda e, n: (e, 0, 0)),
                  pl.BlockSpec((1, FF, td), lambda e, n: (e, 0, n))],
        out_specs=pl.BlockSpec((1, M, td), lambda e, n: (e, 0, n)),
        out_shape=jax.ShapeDtypeStruct((E, M, D), BF16),
        compiler_params=_cparams(2), name="expert_down")(h, w_down)


def _combine_kernel(sl_ref, af_ref, y_ref, x_ref, g_ref, o_ref, w_s):
    E, _, cap, td = y_ref.shape

    @pl.when(pl.program_id(2) == 0)
    def _():
        sl, af = sl_ref[0], af_ref[0]
        want = lax.broadcasted_iota(jnp.int32, (w_s.shape[0], cap), 1)
        for e in range(E):
            w_s[:, e * cap:(e + 1) * cap] = jnp.where(sl[:, e:e + 1] == want, af[:, e:e + 1], 0.0).astype(w_s.dtype)

    y = y_ref[:, 0].reshape(E * cap, td)
    o_ref[0] = x_ref[0] + g_ref[0] * _dot(w_s[...], y)


def _combine(slot_tok, aff, y, x, gate):
    B, L, _ = slot_tok.shape
    E, cap, D = y.shape[0], y.shape[2], y.shape[3]
    tm, td = _tile(512, L), _tile(1024, D)
    lanespec = pl.BlockSpec((1, tm, LANES), lambda b, m, d: (b, m, 0))
    xspec = pl.BlockSpec((1, tm, td), lambda b, m, d: (b, m, d))
    return pl.pallas_call(
        _combine_kernel, grid=(B, L // tm, D // td),
        in_specs=[lanespec, lanespec,
                  pl.BlockSpec((E, 1, cap, td), lambda b, m, d: (0, b, 0, d)),
                  xspec, pl.BlockSpec((1, 1, td), lambda b, m, d: (b, 0, d))],
        out_specs=xspec, out_shape=jax.ShapeDtypeStruct((B, L, D), F32),
        scratch_shapes=[pltpu.VMEM((tm, E * cap), BF16)],
        compiler_params=_cparams(3), name="moe_combine")(slot_tok, aff, y, x, gate.reshape(B, 1, D))


def kernel(x, c, ctx, c_ctx, w_ada, b_ada, norm1, norm2, w_in, hy_conv_w, hy_conv_b, hy_w1, hy_b1, hy_w2, hy_b2, hy_w3, hy_b3, hy_freq, hy_wout, hy_skip, hg_lb, hg_norm, w_proj_hy, w_proj_hg, w_out, w_router, w_gate, w_up, w_down, norm_f):
    assert w_ada.shape[0] == 1, "single-layer block"
    B, L, D = x.shape
    Lc = ctx.shape[1]
    hyw, hgw = hy_skip.shape[-1], hg_norm.shape[-1]
    H = hgw // HG_HEAD
    E = w_router.shape[-1]
    off_hg = 3 * hyw
    off_f = off_hg + hgw
    off_gate = off_hg + 5 * hgw
    in_cols = off_gate + 2 * D
    assert w_in.shape[-1] == in_cols and L % GRID_W == 0
    cap = EC_CAPACITY * L // E

    lb_all = jnp.cumsum(jax.nn.softmax(hg_lb.astype(F32), axis=0), axis=0)
    lb_f, lb_b = lb_all[0, 0], lb_all[0, 1]

    rows = -(-(B + 1) // 8) * 8
    cond = jnp.concatenate([c, c_ctx[None], jnp.zeros((rows - B - 1, D), F32)], axis=0)
    ada = _ada(cond, w_ada[0], b_ada[0])
    sh1, sc1, g1, sh2, sc2, g2 = [ada[:B, i * D:(i + 1) * D] for i in range(6)]
    csh1 = jnp.broadcast_to(ada[B:B + 1, 0:D], (B, D))
    csc1 = jnp.broadcast_to(ada[B:B + 1, D:2 * D], (B, D))

    xc_m = _normmod(ctx, norm1[0], csh1, csc1)
    pc = _mm(xc_m.reshape(B * Lc, D), w_in[0], off_f, 3 * hgw, 1024, 512, BF16, "ctx_proj")
    st_f, st_b = _hgrn_context_states(pc.reshape(B, Lc, 3 * hgw), hgw, lb_f, lb_b)

    x_m, x_mp = _normmod_parity(x, norm1[0], sh1, sc1)
    hyp = _mm_short_conv(x_mp, w_in[0], off_hg, hy_conv_w[0], hy_conv_b[0])
    proj = _mm(x_m.reshape(B * L, D), w_in[0], off_hg, in_cols - off_hg, 512, 1024, BF16,
               "in_proj")

    fwd, fwd_lo, inv = _dft_tables(L)
    hp, hm = _hy_filters(L, hy_w1[0], hy_b1[0], hy_w2[0], hy_b2[0], hy_w3[0], hy_b3[0], hy_freq[0],
                         hy_wout[0], hyw)
    n_orders = hp.shape[0]
    spectra = _filter_spectra(hp.reshape(n_orders, 2, L // 2, hyw), hm.reshape(n_orders, 2, L // 2, hyw),
                              fwd, fwd_lo)
    z = _long_conv_gated(hyp, 0, hyp, hyw, hyw, fwd, inv, spectra, 0, hy_skip[0, 0], False)
    y_hy = _long_conv_gated(z, 0, hyp, 2 * hyw, hyw, fwd, inv, spectra, 1, hy_skip[0, 1], True)

    y_hg = _hgrn(proj.reshape(B, L, in_cols - off_hg), 0, hgw, lb_f, lb_b, hg_norm[0], st_f, st_b)

    merged = _merge(y_hy.reshape(B * L, hyw), y_hg.reshape(B * L, hgw), w_proj_hy[0], w_proj_hg[0],
                    proj, off_gate - off_hg)
    x1 = _mm_res(merged, w_out[0], x.reshape(B * L, D), g1, L, 1024, 512).reshape(B, L, D)

    xm2, aff = _norm_router(x1, norm2[0], sh2, sc2, w_router[0])
    aff_t = jnp.swapaxes(aff[:, :, :E], 1, 2)
    slot_row = _route(aff_t, cap)
    xg = _gather(slot_row, xm2, cap)
    y = _experts(xg.reshape(E, B * cap, D), w_gate[0], w_up[0], w_down[0]).reshape(E, B, cap, D)
    slot_tok = jnp.pad(jnp.swapaxes(slot_row.reshape(B, E, L), 1, 2), ((0, 0), (0, 0), (0, LANES - E)),
                       constant_values=-1)
    x2 = _combine(slot_tok, aff, y, x1, g2)

    return _rmsnorm(x2.reshape(B * L, D), norm_f).reshape(B, L, D)
```
